```python
import jax, jax.numpy as jnp
from jax import lax
import numpy as np

D_MODEL = 1024
BATCH = 4
SEQ = 8192
DEPTH = 2

CHUNK = 64
N_BRANCH = 3
POOL_GROUPS = 4
POOL_GW = 128
POOL_W = POOL_GROUPS * POOL_GW
POOL_WINDOWS = (2, 4, 8, 16)
ML_HEADS = 4
ML_DH = 128
ML_W = ML_HEADS * ML_DH
CONV_W = 4
FOX_HEADS = 8
FOX_DH = 64
FOX_W = FOX_HEADS * FOX_DH
Q_BLOCK = 128
D_FF = 2816
N_EXPERTS = 8
TOP_K = 2
D_FF_EXPERT = 3584
MOE_BLOCK = 128
NORM_EPS = 1e-6
IN_SIZES = (POOL_W, ML_W, ML_W, ML_W, ML_W, ML_HEADS, ML_HEADS, FOX_W, FOX_W, FOX_W, FOX_HEADS, N_BRANCH * D_MODEL)
IN_W = sum(IN_SIZES)
N_DENSE = (DEPTH + 1) // 2
N_MOE = DEPTH // 2

kernel_name = 'hybrid_pool_mlstm_fox_moe_trunk'


def rmsnorm(x, g):
    xf = x.astype(jnp.float32)
    y = xf * lax.rsqrt(jnp.mean(xf * xf, axis=-1, keepdims=True) + NORM_EPS)
    return (y * g.astype(jnp.float32)).astype(x.dtype)


def causal_depthwise_conv(u, w):
    return lax.conv_general_dilated(u, w[:, None, :].astype(u.dtype), window_strides=(1,),
                                    padding=[(CONV_W - 1, 0)], dimension_numbers=('NWC', 'WIO', 'NWC'),
                                    feature_group_count=u.shape[-1])


def pool_mixer(u, w_grp, scale):
    bsz, s_len, _ = u.shape
    uf = u.astype(jnp.float32).reshape(bsz, s_len, POOL_GROUPS, POOL_GW)
    cs = jnp.concatenate([jnp.zeros_like(uf[:, :1]), jnp.cumsum(uf, axis=1)], axis=1)
    pos = jnp.arange(s_len)
    outs = []
    for g, w in enumerate(POOL_WINDOWS):
        csg = cs[:, :, g]
        lower = jnp.concatenate([jnp.zeros_like(csg[:, :w - 1]), csg[:, :s_len - w + 1]], axis=1)
        cnt = jnp.minimum(pos + 1, w).astype(jnp.float32)[None, :, None]
        outs.append((csg[:, 1:] - lower) / cnt - uf[:, :, g])
    d = jnp.stack(outs, axis=2)
    y = jnp.einsum('bsgc,gcd->bsgd', d, w_grp.astype(jnp.float32)).reshape(bsz, s_len, POOL_W)
    return (y * scale.astype(jnp.float32)).astype(u.dtype)


def mlstm_mixer(q, k, v, o_pre, i_pre, f_pre, g_norm):
    bsz, s_len, _ = q.shape
    nc = s_len // CHUNK

    def heads(t):
        return t.astype(jnp.float32).reshape(bsz, nc, CHUNK, ML_HEADS, ML_DH).transpose(1, 0, 3, 2, 4)

    def gates(t):
        return t.astype(jnp.float32).reshape(bsz, nc, CHUNK, ML_HEADS).transpose(1, 0, 3, 2)

    qc, kc, vc = heads(q), heads(k) * (ML_DH ** -0.5), heads(v)
    lic = gates(i_pre)
    lfc = gates(jax.nn.log_sigmoid(f_pre.astype(jnp.float32)))
    tri = jnp.tril(jnp.ones((CHUNK, CHUNK), dtype=bool))

    def step(carry, inp):
        c_mat, n_vec, m = carry
        qb, kb, vb, li, lf = inp
        b = jnp.cumsum(lf, axis=-1)
        dmat = jnp.where(tri, b[..., :, None] - b[..., None, :] + li[..., None, :], -jnp.inf)
        m_inter = b + m[..., None]
        m_t = jnp.maximum(m_inter, jnp.max(dmat, axis=-1))
        w_inter = jnp.exp(m_inter - m_t)
        p = jnp.exp(dmat - m_t[..., None]) * jnp.einsum('bhtd,bhsd->bhts', qb, kb)
        num = w_inter[..., None] * jnp.einsum('bhvk,bhtk->bhtv', c_mat, qb) + jnp.einsum('bhts,bhsv->bhtv', p, vb)
        den = w_inter * jnp.einsum('bhk,bhtk->bht', n_vec, qb) + jnp.sum(p, axis=-1)
        h = num / jnp.maximum(jnp.abs(den), jnp.exp(-m_t))[..., None]
        b_last = b[..., -1]
        g = b_last[..., None] - b + li
        m_new = jnp.maximum(b_last + m, jnp.max(g, axis=-1))
        decay = jnp.exp(b_last + m - m_new)
        wk = jnp.exp(g - m_new[..., None])
        c_mat = decay[..., None, None] * c_mat + jnp.einsum('bhs,bhsv,bhsk->bhvk', wk, vb, kb)
        n_vec = decay[..., None] * n_vec + jnp.einsum('bhs,bhsk->bhk', wk, kb)
        return (c_mat, n_vec, m_new), h

    init = (jnp.zeros((bsz, ML_HEADS, ML_DH, ML_DH), jnp.float32),
            jnp.zeros((bsz, ML_HEADS, ML_DH), jnp.float32),
            jnp.zeros((bsz, ML_HEADS), jnp.float32))
    _, h = lax.scan(step, init, (qc, kc, vc, lic, lfc))
    h = h.transpose(1, 0, 3, 2, 4)
    h = h * lax.rsqrt(jnp.mean(h * h, axis=-1, keepdims=True) + NORM_EPS) * g_norm.astype(jnp.float32).reshape(ML_HEADS, ML_DH)
    h = h.reshape(bsz, s_len, ML_W) * jax.nn.sigmoid(o_pre.astype(jnp.float32))
    return h.astype(q.dtype)


def fox_mixer(q, k, v, f_pre):
    bsz, s_len, _ = q.shape
    nb = s_len // Q_BLOCK
    qh = q.reshape(bsz, s_len, FOX_HEADS, FOX_DH).transpose(0, 2, 1, 3)
    kh = k.reshape(bsz, s_len, FOX_HEADS, FOX_DH).transpose(0, 2, 1, 3)
    vh = v.reshape(bsz, s_len, FOX_HEADS, FOX_DH).transpose(0, 2, 1, 3)
    f_cum = jnp.cumsum(jax.nn.log_sigmoid(f_pre.astype(jnp.float32)), axis=1).transpose(0, 2, 1)
    q_blocks = qh.reshape(bsz, FOX_HEADS, nb, Q_BLOCK, FOX_DH).transpose(2, 0, 1, 3, 4)
    f_blocks = f_cum.reshape(bsz, FOX_HEADS, nb, Q_BLOCK).transpose(2, 0, 1, 3)
    kpos = jnp.arange(s_len)
    scale = FOX_DH ** -0.5

    def block(args):
        qb, fq, i = args
        qpos = i * Q_BLOCK + jnp.arange(Q_BLOCK)
        s = jnp.einsum('bhtd,bhsd->bhts', qb, kh).astype(jnp.float32) * scale + fq[..., :, None] - f_cum[:, :, None, :]
        s = jnp.where(kpos[None, :] <= qpos[:, None], s, -jnp.inf)
        p = jax.nn.softmax(s, axis=-1)
        return jnp.einsum('bhts,bhsd->bhtd', p.astype(vh.dtype), vh)

    o = lax.map(block, (q_blocks, f_blocks, jnp.arange(nb)))
    return o.transpose(1, 0, 3, 2, 4).reshape(bsz, s_len, FOX_W)


def token_mixing(x, norm_g, w_in, pool_w_grp, pool_scale, ml_conv_w, ml_b_i, ml_b_f, ml_norm_g,
                 fox_b_f, w_br_pool, w_br_ml, w_br_fox, w_out):
    h = rmsnorm(x, norm_g)
    proj = h @ w_in
    splits = np.cumsum(IN_SIZES)[:-1].tolist()
    (u_pool, ml_q, ml_k, ml_v, ml_o, ml_i, ml_f, fx_q, fx_k, fx_v, fx_f, gate_pre) = jnp.split(proj, splits, axis=-1)
    qk = jax.nn.silu(causal_depthwise_conv(jnp.concatenate([ml_q, ml_k], axis=-1), ml_conv_w))
    ml_q, ml_k = qk[..., :ML_W], qk[..., ML_W:]
    y_pool = pool_mixer(u_pool, pool_w_grp, pool_scale) @ w_br_pool
    y_ml = mlstm_mixer(ml_q, ml_k, ml_v, ml_o, ml_i + ml_b_i, ml_f + ml_b_f, ml_norm_g) @ w_br_ml
    y_fox = fox_mixer(fx_q, fx_k, fx_v, fx_f + fox_b_f) @ w_br_fox
    gts = jax.nn.sigmoid(gate_pre).reshape(gate_pre.shape[:-1] + (N_BRANCH, D_MODEL))
    merged = gts[..., 0, :] * y_pool + gts[..., 1, :] * y_ml + gts[..., 2, :] * y_fox
    return merged @ w_out


def swiglu(h, w_gate, w_up, w_down):
    return (jax.nn.silu(h @ w_gate) * (h @ w_up)) @ w_down


def moe_swiglu(h, w_router, b_router, w_gate, w_up, w_down):
    bsz, s_len, d = h.shape
    n_tok = bsz * s_len
    t = h.reshape(n_tok, d)
    logits = (t @ w_router).astype(jnp.float32) + b_router.astype(jnp.float32)
    top_v, top_e = lax.top_k(logits, TOP_K)
    gate = jax.nn.softmax(top_v, axis=-1)
    n_asg = n_tok * TOP_K
    e_flat = top_e.reshape(n_asg)
    tok_flat = jnp.repeat(jnp.arange(n_tok, dtype=jnp.int32), TOP_K)
    g_flat = gate.reshape(n_asg)
    order = jnp.argsort(e_flat)
    e_s, tok_s, g_s = e_flat[order], tok_flat[order], g_flat[order]
    counts = jnp.bincount(e_flat, length=N_EXPERTS).astype(jnp.int32)
    start = jnp.cumsum(counts) - counts
    padded = ((counts + MOE_BLOCK - 1) // MOE_BLOCK) * MOE_BLOCK
    pend = jnp.cumsum(padded)
    pstart = pend - padded
    dest = pstart[e_s] + (jnp.arange(n_asg, dtype=jnp.int32) - start[e_s])
    n_blk = (n_asg + N_EXPERTS * (MOE_BLOCK - 1) + MOE_BLOCK - 1) // MOE_BLOCK
    n_rows = n_blk * MOE_BLOCK
    row_tok = jnp.full((n_rows,), n_tok, dtype=jnp.int32).at[dest].set(tok_s)
    row_g = jnp.zeros((n_rows,), jnp.float32).at[dest].set(g_s)
    blk_e = jnp.minimum(jnp.searchsorted(pend, jnp.arange(n_blk, dtype=jnp.int32) * MOE_BLOCK, side='right'), N_EXPERTS - 1)
    t_pad = jnp.concatenate([t, jnp.zeros((1, d), t.dtype)], axis=0)
    xr = t_pad[row_tok].reshape(n_blk, MOE_BLOCK, d)

    def one(args):
        xb, e = args
        return (jax.nn.silu(xb @ w_gate[e]) * (xb @ w_up[e])) @ w_down[e]

    out = lax.map(one, (xr, blk_e)).reshape(n_rows, d)
    y = jnp.zeros((n_tok + 1, d), out.dtype).at[row_tok].add(out * row_g[:, None].astype(out.dtype))[:n_tok]
    return y.reshape(bsz, s_len, d)


def setup_inputs(seed: int = 0) -> dict:
    key = jax.random.key(seed)
    ks = iter(jax.random.split(key, 32))

    def nrm(shape, scale):
        return jax.random.normal(next(ks), shape, jnp.float32) * scale

    return {
        'x': nrm((BATCH, SEQ, D_MODEL), 1.0),
        'mix_norm_g': 1.0 + nrm((DEPTH, D_MODEL), 0.05),
        'w_in': nrm((DEPTH, D_MODEL, IN_W), D_MODEL ** -0.5),
        'pool_w_grp': nrm((DEPTH, POOL_GROUPS, POOL_GW, POOL_GW), POOL_GW ** -0.5),
        'pool_scale': 1.0 + nrm((DEPTH, POOL_W), 0.05),
        'ml_conv_w': nrm((DEPTH, CONV_W, 2 * ML_W), CONV_W ** -0.5),
        'ml_b_i': nrm((DEPTH, ML_HEADS), 0.1),
        'ml_b_f': 3.0 + nrm((DEPTH, ML_HEADS), 0.5),
        'ml_norm_g': 1.0 + nrm((DEPTH, ML_W), 0.05),
        'fox_b_f': 3.0 + nrm((DEPTH, FOX_HEADS), 0.5),
        'w_br_pool': nrm((DEPTH, POOL_W, D_MODEL), POOL_W ** -0.5),
        'w_br_ml': nrm((DEPTH, ML_W, D_MODEL), ML_W ** -0.5),
        'w_br_fox': nrm((DEPTH, FOX_W, D_MODEL), FOX_W ** -0.5),
        'w_out': nrm((DEPTH, D_MODEL, D_MODEL), D_MODEL ** -0.5),
        'ffn_norm_g': 1.0 + nrm((DEPTH, D_MODEL), 0.05),
        'ff_w_gate': nrm((N_DENSE, D_MODEL, D_FF), D_MODEL ** -0.5),
        'ff_w_up': nrm((N_DENSE, D_MODEL, D_FF), D_MODEL ** -0.5),
        'ff_w_down': nrm((N_DENSE, D_FF, D_MODEL), D_FF ** -0.5),
        'moe_w_router': nrm((N_MOE, D_MODEL, N_EXPERTS), D_MODEL ** -0.5),
        'moe_b_router': nrm((N_MOE, N_EXPERTS), 0.01),
        'moe_w_gate': nrm((N_MOE, N_EXPERTS, D_MODEL, D_FF_EXPERT), D_MODEL ** -0.5),
        'moe_w_up': nrm((N_MOE, N_EXPERTS, D_MODEL, D_FF_EXPERT), D_MODEL ** -0.5),
        'moe_w_down': nrm((N_MOE, N_EXPERTS, D_FF_EXPERT, D_MODEL), D_FF_EXPERT ** -0.5),
        'final_norm_g': 1.0 + nrm((D_MODEL,), 0.05),
    }


def reference(x, mix_norm_g, w_in, pool_w_grp, pool_scale, ml_conv_w, ml_b_i, ml_b_f, ml_norm_g, fox_b_f,
              w_br_pool, w_br_ml, w_br_fox, w_out, ffn_norm_g, ff_w_gate, ff_w_up, ff_w_down,
              moe_w_router, moe_b_router, moe_w_gate, moe_w_up, moe_w_down, final_norm_g):
    for l in range(DEPTH):
        x = x + token_mixing(x, mix_norm_g[l], w_in[l], pool_w_grp[l], pool_scale[l], ml_conv_w[l],
                             ml_b_i[l], ml_b_f[l], ml_norm_g[l], fox_b_f[l], w_br_pool[l], w_br_ml[l],
                             w_br_fox[l], w_out[l])
        h = rmsnorm(x, ffn_norm_g[l])
        if l % 2 == 0:
            x = x + swiglu(h, ff_w_gate[l // 2], ff_w_up[l // 2], ff_w_down[l // 2])
        else:
            x = x + moe_swiglu(h, moe_w_router[l // 2], moe_b_router[l // 2], moe_w_gate[l // 2],
                               moe_w_up[l // 2], moe_w_down[l // 2])
    return rmsnorm(x, final_norm_g)
```

```python
import functools
import math

import jax
import jax.numpy as jnp
from jax import lax
from jax.experimental import pallas as pl
from jax.experimental.pallas import tpu as pltpu

F32 = jnp.float32
BF16 = jnp.bfloat16

NORM_EPS = 1e-6
POOL_WINDOWS = (2, 4, 8, 16)
ML_HEADS = 4
FOX_HEADS = 8
TOP_K = 2
N_BRANCH = 3

LANES = 128
BF16_SUBLANES = 16
VMEM_LIMIT_BYTES = 56 * 1024 * 1024

NEG_BIG = -1e30


def _cparams(*sem):
    return pltpu.CompilerParams(dimension_semantics=sem, vmem_limit_bytes=VMEM_LIMIT_BYTES)


def _resident(shape):
    zeros = (0,) * len(shape)
    return pl.BlockSpec(shape, lambda *_: zeros, pipeline_mode=pl.Buffered(1))


def _rmsnorm(x, g):
    return x * lax.rsqrt(jnp.mean(x * x, axis=-1, keepdims=True) + NORM_EPS) * g


def _sigmoid(x):
    return 1.0 / (1.0 + jnp.exp(-x))


def _log_sigmoid(x):
    return jnp.minimum(x, 0.0) - jnp.log(1.0 + jnp.exp(-jnp.abs(x)))


def _split3(x):
    hi = x.astype(BF16)
    r = x - hi.astype(F32)
    mid = r.astype(BF16)
    lo = (r - mid.astype(F32)).astype(BF16)
    return hi, mid, lo


def _cumsum_rows(tril, x):
    return sum(jnp.dot(tril, part, preferred_element_type=F32) for part in _split3(x))


def _cumsum_lanes(x, triu):
    return sum(jnp.dot(part, triu, preferred_element_type=F32) for part in _split3(x))


def _inproj_kernel(x_ref, g_ref, wm_ref, ws_ref, p_ref, s_ref, *, col_chunk):
    h = _rmsnorm(x_ref[...], g_ref[...]).astype(BF16)
    s_ref[...] = jnp.dot(h, ws_ref[...], preferred_element_type=F32)
    for c in range(wm_ref.shape[1] // col_chunk):
        sl = slice(c * col_chunk, (c + 1) * col_chunk)
        p_ref[:, sl] = jnp.dot(h, wm_ref[:, sl], preferred_element_type=F32).astype(BF16)


def _inproj(x2, g, wm, ws, *, tm=512, col_chunk=1024):
    n, d = x2.shape
    wcols = wm.shape[1]
    return pl.pallas_call(
        functools.partial(_inproj_kernel, col_chunk=col_chunk),
        grid=(n // tm,),
        in_specs=[pl.BlockSpec((tm, d), lambda i: (i, 0)),
                  _resident((1, d)), _resident((d, wcols)), _resident((d, LANES))],
        out_specs=[pl.BlockSpec((tm, wcols), lambda i: (i, 0)),
                   pl.BlockSpec((tm, LANES), lambda i: (i, 0))],
        out_shape=[jax.ShapeDtypeStruct((n, wcols), BF16), jax.ShapeDtypeStruct((n, LANES), F32)],
        compiler_params=_cparams("parallel"),
        name="inproj",
    )(x2, g, wm, ws)


def _gateprep_kernel(s_ref, b_ref, col_ref, row_ref, carry_ref):
    t = s_ref.shape[1]

    @pl.when(pl.program_id(1) == 0)
    def _():
        carry_ref[...] = jnp.zeros_like(carry_ref)

    pre = s_ref[0] + b_ref[...]
    lane = lax.broadcasted_iota(jnp.int32, pre.shape, 1)
    ls = _log_sigmoid(pre)
    fox = (lane >= 2 * ML_HEADS) & (lane < 2 * ML_HEADS + FOX_HEADS)
    tril = (lax.broadcasted_iota(jnp.int32, (t, t), 0) >= lax.broadcasted_iota(jnp.int32, (t, t), 1)).astype(BF16)
    fcum = _cumsum_rows(tril, jnp.where(fox, ls, 0.0)) + carry_ref[...]
    carry_ref[...] = fcum[t - 1:t, :]
    col = jnp.where(lane < ML_HEADS, pre, jnp.where(lane < 2 * ML_HEADS, ls, fcum))
    col_ref[0] = col
    row_ref[0] = col.T[0:row_ref.shape[1], :]


def _gateprep(s3, bias, *, tg=512):
    b, s, _ = s3.shape
    n_rows = 2 * ML_HEADS + FOX_HEADS
    return pl.pallas_call(
        _gateprep_kernel,
        grid=(b, s // tg),
        in_specs=[pl.BlockSpec((1, tg, LANES), lambda i, j: (i, j, 0)), _resident((1, LANES))],
        out_specs=[pl.BlockSpec((1, tg, LANES), lambda i, j: (i, j, 0)),
                   pl.BlockSpec((1, n_rows, tg), lambda i, j: (i, 0, j))],
        out_shape=[jax.ShapeDtypeStruct((b, s, LANES), F32), jax.ShapeDtypeStruct((b, n_rows, s), F32)],
        scratch_shapes=[pltpu.VMEM((1, LANES), F32)],
        compiler_params=_cparams("parallel", "arbitrary"),
        name="gateprep",
    )(s3, bias)


def _fox_kernel(q_ref, k_ref, v_ref, fc_ref, fr_ref, o_ref, m_ref, l_ref, acc_ref, *, tq, dh, scale):
    p = pl.program_id(1)
    qi = pl.program_id(2)
    q2 = q_ref[0]
    lane = lax.broadcasted_iota(jnp.int32, (tq, LANES), 1)
    fc = fc_ref[0]
    row0 = 2 * ML_HEADS + 2 * p
    qs, fqs = [], []
    for a in range(2):
        in_head = (lane >= a * dh) & (lane < (a + 1) * dh)
        qs.append(jnp.where(in_head, q2 * scale, jnp.zeros_like(q2)))
        fqs.append(jnp.sum(jnp.where(lane == row0 + a, fc, 0.0), axis=-1, keepdims=True))
    m_ref[...] = jnp.full_like(m_ref, NEG_BIG)
    l_ref[...] = jnp.zeros_like(l_ref)
    acc_ref[...] = jnp.zeros_like(acc_ref)
    causal = (lax.broadcasted_iota(jnp.int32, (tq, tq), 1) <= lax.broadcasted_iota(jnp.int32, (tq, tq), 0))

    def step(j, masked):
        start = pl.multiple_of(j * tq, tq)
        kb = k_ref[0, pl.ds(start, tq), :]
        vb = v_ref[0, pl.ds(start, tq), :]
        for a in range(2):
            s = lax.dot_general(qs[a], kb, (((1,), (1,)), ((), ())), preferred_element_type=F32)
            s = s + (fqs[a] - fr_ref[0, pl.ds(row0 + a, 1), pl.ds(start, tq)])
            if masked:
                s = jnp.where(causal, s, NEG_BIG)
            m_prev = m_ref[a]
            m_new = jnp.maximum(m_prev, jnp.max(s, axis=-1, keepdims=True))
            alpha = jnp.exp(m_prev - m_new)
            pexp = jnp.exp(s - m_new)
            l_ref[a] = alpha * l_ref[a] + jnp.sum(pexp, axis=-1, keepdims=True)
            acc_ref[a] = alpha * acc_ref[a] + jnp.dot(pexp.astype(BF16), vb, preferred_element_type=F32)
            m_ref[a] = m_new

    def body(j, carry):
        step(j, False)
        return carry

    lax.fori_loop(0, qi, body, 0)
    step(qi, True)
    out = jnp.where(lane < dh, acc_ref[0] / l_ref[0], acc_ref[1] / l_ref[1])
    o_ref[0] = out.astype(BF16)


def _fox(p3, gcol, grow, *, q_col, k_col, v_col, dh, tq=512):
    b, s, _ = p3.shape
    pairs = FOX_HEADS * dh // LANES
    scale = dh ** -0.5
    assert math.frexp(scale)[0] == 0.5, "score scale is folded into bf16 q; exact only for a power of two"
    qb, kb, vb = q_col // LANES, k_col // LANES, v_col // LANES
    return pl.pallas_call(
        functools.partial(_fox_kernel, tq=tq, dh=dh, scale=scale),
        grid=(b, pairs, s // tq),
        in_specs=[pl.BlockSpec((1, tq, LANES), lambda i, p, q: (i, q, qb + p)),
                  pl.BlockSpec((1, s, LANES), lambda i, p, q: (i, 0, kb + p)),
                  pl.BlockSpec((1, s, LANES), lambda i, p, q: (i, 0, vb + p)),
                  pl.BlockSpec((1, tq, LANES), lambda i, p, q: (i, q, 0)),
                  pl.BlockSpec((1, grow.shape[1], s), lambda i, p, q: (i, 0, 0))],
        out_specs=pl.BlockSpec((1, tq, LANES), lambda i, p, q: (i, q, p)),
        out_shape=jax.ShapeDtypeStruct((b, s, pairs * LANES), BF16),
        scratch_shapes=[pltpu.VMEM((2, tq, 1), F32), pltpu.VMEM((2, tq, 1), F32),
                        pltpu.VMEM((2, tq, LANES), F32)],
        compiler_params=_cparams("parallel", "parallel", "arbitrary"),
        name="fox_attention",
    )(p3, p3, p3, gcol, grow)


def _mlstm_kernel(qk_ref, halo_ref, v_ref, og_ref, gc_ref, gr_ref, cw_ref, ng_ref, out_ref,
                  ubuf, ct_ref, n_ref, m_ref, *, chunk, dh):
    c = pl.program_id(1)
    width = ML_HEADS * dh
    taps = cw_ref.shape[0]
    pad = halo_ref.shape[1]

    @pl.when(c == 0)
    def _():
        ct_ref[...] = jnp.zeros_like(ct_ref)
        n_ref[...] = jnp.zeros_like(n_ref)
        m_ref[...] = jnp.zeros_like(m_ref)

    ubuf[0:pad, :] = jnp.where(c == 0, 0.0, halo_ref[0].astype(F32))
    ubuf[pad:pad + chunk, :] = qk_ref[0].astype(F32)

    def conv_silu(col0):
        y = sum(ubuf[pad - taps + 1 + j:pad - taps + 1 + j + chunk, col0:col0 + dh] * cw_ref[j:j + 1, col0:col0 + dh]
                for j in range(taps))
        return y * _sigmoid(y)

    ri = lax.broadcasted_iota(jnp.int32, (chunk, chunk), 0)
    ci = lax.broadcasted_iota(jnp.int32, (chunk, chunk), 1)
    lower = ri >= ci
    gc = gc_ref[0]
    gr = gr_ref[0]
    bc_all = _cumsum_rows(lower.astype(BF16), gc)
    br_all = _cumsum_lanes(gr, (ri <= ci).astype(BF16))

    for h in range(ML_HEADS):
        hs = slice(h * dh, (h + 1) * dh)
        q = conv_silu(h * dh)
        k = conv_silu(width + h * dh) * dh ** -0.5
        qb, kb = q.astype(BF16), k.astype(BF16)
        vb = v_ref[0, :, hs]
        li_c = gc[:, h:h + 1]
        b_c = bc_all[:, ML_HEADS + h:ML_HEADS + h + 1]
        li_r = gr[h:h + 1, :]
        b_r = br_all[ML_HEADS + h:ML_HEADS + h + 1, :]
        b_last = b_c[chunk - 1:chunk, :]
        m_prev = m_ref[h:h + 1, 0:1]

        dmat = jnp.where(lower, b_c - (b_r - li_r), NEG_BIG)
        m_inter = b_c + m_prev
        m_t = jnp.maximum(m_inter, jnp.max(dmat, axis=-1, keepdims=True))
        w_inter = jnp.exp(m_inter - m_t)
        pm = jnp.exp(dmat - m_t) * lax.dot_general(qb, kb, (((1,), (1,)), ((), ())), preferred_element_type=F32)
        num = (w_inter * jnp.dot(qb, ct_ref[h].astype(BF16), preferred_element_type=F32)
               + jnp.dot(pm.astype(BF16), vb, preferred_element_type=F32))
        den = (w_inter * jnp.sum(q * n_ref[h], axis=-1, keepdims=True)
               + jnp.sum(pm, axis=-1, keepdims=True))
        hh = num / jnp.maximum(jnp.abs(den), jnp.exp(-m_t))

        g_c = b_last - b_c + li_c
        m_new = jnp.maximum(b_last + m_prev, jnp.max(g_c, axis=0, keepdims=True))
        decay = jnp.exp(b_last + m_prev - m_new)
        kw = k * jnp.exp(g_c - m_new)
        ct_ref[h] = decay * ct_ref[h] + lax.dot_general(kw.astype(BF16), vb, (((0,), (0,)), ((), ())),
                                                       preferred_element_type=F32)
        n_ref[h] = decay * n_ref[h] + jnp.sum(kw, axis=0, keepdims=True)
        m_ref[h:h + 1, :] = jnp.broadcast_to(m_new, (1, LANES))

        hn = hh * lax.rsqrt(jnp.mean(hh * hh, axis=-1, keepdims=True) + NORM_EPS) * ng_ref[:, hs]
        out_ref[0, :, hs] = (hn * _sigmoid(og_ref[0, :, hs].astype(F32))).astype(BF16)


def _mlstm(p3, gcol, grow, conv_w, norm_g, *, qk_col, v_col, o_col, dh, chunk=256):
    b, s, _ = p3.shape
    width = ML_HEADS * dh
    pad = BF16_SUBLANES
    assert conv_w.shape[0] - 1 <= pad
    qkb, vb, ob = qk_col // (2 * width), v_col // width, o_col // width
    halo_per_chunk = chunk // pad
    return pl.pallas_call(
        functools.partial(_mlstm_kernel, chunk=chunk, dh=dh),
        grid=(b, s // chunk),
        in_specs=[pl.BlockSpec((1, chunk, 2 * width), lambda i, c: (i, c, qkb)),
                  pl.BlockSpec((1, pad, 2 * width),
                               lambda i, c: (i, jnp.maximum(c * halo_per_chunk - 1, 0), qkb)),
                  pl.BlockSpec((1, chunk, width), lambda i, c: (i, c, vb)),
                  pl.BlockSpec((1, chunk, width), lambda i, c: (i, c, ob)),
                  pl.BlockSpec((1, chunk, LANES), lambda i, c: (i, c, 0)),
                  pl.BlockSpec((1, grow.shape[1], chunk), lambda i, c: (i, 0, c)),
                  _resident(conv_w.shape), _resident((1, width))],
        out_specs=pl.BlockSpec((1, chunk, width), lambda i, c: (i, c, 0)),
        out_shape=jax.ShapeDtypeStruct((b, s, width), BF16),
        scratch_shapes=[pltpu.VMEM((pad + chunk, 2 * width), F32),
                        pltpu.VMEM((ML_HEADS, dh, dh), F32),
                        pltpu.VMEM((ML_HEADS, 1, dh), F32),
                        pltpu.VMEM((ML_HEADS, LANES), F32)],
        compiler_params=_cparams("parallel", "arbitrary"),
        name="mlstm",
    )(p3, p3, p3, p3, gcol, grow, conv_w, norm_g)


def _merge_kernel(u_ref, halo_ref, hm_ref, fo_ref, gp_ref, x_ref, wg_ref, ps_ref, wbp_ref, wbm_ref, wbf_ref,
                  wo_ref, o_ref, ubuf, *, tm):
    j = pl.program_id(1)
    pad = halo_ref.shape[1]
    gw = wg_ref.shape[1]
    d = x_ref.shape[2]
    ubuf[0:pad, :] = jnp.where(j == 0, 0.0, halo_ref[0].astype(F32))
    ubuf[pad:pad + tm, :] = u_ref[0].astype(F32)
    pos = j * tm + lax.broadcasted_iota(jnp.int32, (tm, 1), 0)
    ys = []
    for g, w in enumerate(POOL_WINDOWS):
        gs = slice(g * gw, (g + 1) * gw)
        wsum = sum(ubuf[pad - k:pad - k + tm, gs] for k in range(w))
        cnt = jnp.minimum(pos + 1, w).astype(F32)
        dlt = wsum / cnt - ubuf[pad:pad + tm, gs]
        ys.append(jnp.dot(dlt.astype(BF16), wg_ref[g], preferred_element_type=F32))
    y_pool = (jnp.concatenate(ys, axis=-1) * ps_ref[...]).astype(BF16)
    merged = _sigmoid(gp_ref[0, :, 0:d].astype(F32)) * jnp.dot(y_pool, wbp_ref[...], preferred_element_type=F32)
    merged += _sigmoid(gp_ref[0, :, d:2 * d].astype(F32)) * jnp.dot(hm_ref[0], wbm_ref[...],
                                                                   preferred_element_type=F32)
    merged += _sigmoid(gp_ref[0, :, 2 * d:3 * d].astype(F32)) * jnp.dot(fo_ref[0], wbf_ref[...],
                                                                       preferred_element_type=F32)
    o_ref[0] = x_ref[0] + jnp.dot(merged.astype(BF16), wo_ref[...], preferred_element_type=F32)


def _merge(p3, hm, fo, x3, wgrp, pscale, wbp, wbm, wbf, wo, *, pool_col, gate_col, tm=256):
    b, s, d = x3.shape
    pw = wbp.shape[0]
    pad = BF16_SUBLANES
    assert max(POOL_WINDOWS) - 1 <= pad and gate_col == 0
    pcb = pool_col // pw
    halo_per_blk = tm // pad
    return pl.pallas_call(
        functools.partial(_merge_kernel, tm=tm),
        grid=(b, s // tm),
        in_specs=[pl.BlockSpec((1, tm, pw), lambda i, j: (i, j, pcb)),
                  pl.BlockSpec((1, pad, pw), lambda i, j: (i, jnp.maximum(j * halo_per_blk - 1, 0), pcb)),
                  pl.BlockSpec((1, tm, hm.shape[2]), lambda i, j: (i, j, 0)),
                  pl.BlockSpec((1, tm, fo.shape[2]), lambda i, j: (i, j, 0)),
                  pl.BlockSpec((1, tm, N_BRANCH * d), lambda i, j: (i, j, 0)),
                  pl.BlockSpec((1, tm, d), lambda i, j: (i, j, 0)),
                  _resident(wgrp.shape), _resident(pscale.shape), _resident(wbp.shape),
                  _resident(wbm.shape), _resident(wbf.shape), _resident(wo.shape)],
        out_specs=pl.BlockSpec((1, tm, d), lambda i, j: (i, j, 0)),
        out_shape=jax.ShapeDtypeStruct((b, s, d), F32),
        scratch_shapes=[pltpu.VMEM((pad + tm, pw), F32)],
        compiler_params=_cparams("parallel", "parallel"),
        name="merge",
    )(p3, p3, hm, fo, p3, x3, wgrp, pscale, wbp, wbm, wbf, wo)


def _swiglu_kernel(x_ref, g_ref, wg_ref, wu_ref, wd_ref, o_ref, a_ref, *, ff_chunk):
    x = x_ref[...]
    h = _rmsnorm(x, g_ref[...]).astype(BF16)
    for c in range(wg_ref.shape[1] // ff_chunk):
        sl = slice(c * ff_chunk, (c + 1) * ff_chunk)
        gate = jnp.dot(h, wg_ref[:, sl], preferred_element_type=F32)
        up = jnp.dot(h, wu_ref[:, sl], preferred_element_type=F32)
        a_ref[:, sl] = (gate * _sigmoid(gate) * up).astype(BF16)
    o_ref[...] = x + jnp.dot(a_ref[...], wd_ref[...], preferred_element_type=F32)


def _swiglu(x2, g, wg, wu, wd, *, tm=512, ff_chunk=256):
    n, d = x2.shape
    ff = wg.shape[1]
    assert ff % ff_chunk == 0
    return pl.pallas_call(
        functools.partial(_swiglu_kernel, ff_chunk=ff_chunk),
        grid=(n // tm,),
        in_specs=[pl.BlockSpec((tm, d), lambda i: (i, 0)), _resident((1, d)),
                  _resident(wg.shape), _resident(wu.shape), _resident(wd.shape)],
        out_specs=pl.BlockSpec((tm, d), lambda i: (i, 0)),
        out_shape=jax.ShapeDtypeStruct((n, d), F32),
        scratch_shapes=[pltpu.VMEM((tm, ff), BF16)],
        compiler_params=_cparams("parallel"),
        name="dense_swiglu",
    )(x2, g, wg, wu, wd)


def _router_kernel(x_ref, g_ref, wr_ref, br_ref, e_ref, gt_ref):
    h = _rmsnorm(x_ref[...], g_ref[...])
    logits = jnp.dot(h, wr_ref[...], preferred_element_type=F32, precision=lax.Precision.HIGHEST) + br_ref[...]
    lane = lax.broadcasted_iota(jnp.int32, logits.shape, 1)
    m1 = jnp.max(logits, axis=-1, keepdims=True)
    i1 = jnp.min(jnp.where(logits == m1, lane, LANES), axis=-1, keepdims=True)
    rest = jnp.where(lane == i1, NEG_BIG, logits)
    m2 = jnp.max(rest, axis=-1, keepdims=True)
    i2 = jnp.min(jnp.where(rest == m2, lane, LANES), axis=-1, keepdims=True)
    e2 = jnp.exp(m2 - m1)
    g1 = 1.0 / (1.0 + e2)
    e_ref[...] = jnp.where(lane == 0, i1, jnp.where(lane == 1, i2, 0))
    gt_ref[...] = jnp.where(lane == 0, g1, jnp.where(lane == 1, e2 * g1, 0.0))


def _router(x2, g, wr, br, *, tm=512):
    n, d = x2.shape
    return pl.pallas_call(
        _router_kernel,
        grid=(n // tm,),
        in_specs=[pl.BlockSpec((tm, d), lambda i: (i, 0)), _resident((1, d)),
                  _resident((d, LANES)), _resident((1, LANES))],
        out_specs=[pl.BlockSpec((tm, LANES), lambda i: (i, 0)), pl.BlockSpec((tm, LANES), lambda i: (i, 0))],
        out_shape=[jax.ShapeDtypeStruct((n, LANES), jnp.int32), jax.ShapeDtypeStruct((n, LANES), F32)],
        compiler_params=_cparams("parallel"),
        name="router",
    )(x2, g, wr, br)


def _dispatch_kernel(dest_ref, x_ref, xs_in_ref, xs_ref, sem):
    del xs_in_ref
    tm = x_ref.shape[0]

    def row_copy(t, k):
        return pltpu.make_async_copy(x_ref.at[pl.ds(t, 1)], xs_ref.at[pl.ds(dest_ref[0, 0, TOP_K * t + k], 1)], sem)

    def start(t, carry):
        for k in range(TOP_K):
            row_copy(t, k).start()
        return carry

    def wait(t, carry):
        for k in range(TOP_K):
            row_copy(t, k).wait()
        return carry

    lax.fori_loop(0, tm, start, 0, unroll=8)
    lax.fori_loop(0, tm, wait, 0, unroll=8)


def _dispatch(dest2, x2, xs_init, *, tm=256):
    n, d = x2.shape
    return pl.pallas_call(
        _dispatch_kernel,
        grid=(n // tm,),
        in_specs=[pl.BlockSpec((1, 1, TOP_K * tm), lambda i: (i, 0, 0), memory_space=pltpu.SMEM),
                  pl.BlockSpec((tm, d), lambda i: (i, 0)),
                  pl.BlockSpec(memory_space=pl.ANY)],
        out_specs=pl.BlockSpec(memory_space=pl.ANY),
        out_shape=jax.ShapeDtypeStruct(xs_init.shape, xs_init.dtype),
        scratch_shapes=[pltpu.SemaphoreType.DMA(())],
        input_output_aliases={2: 0},
        compiler_params=_cparams("arbitrary"),
        name="moe_dispatch",
    )(dest2, x2, xs_init)


def _experts_kernel(blk_e_ref, nact_ref, xs_ref, g_ref, wg_ref, wu_ref, wd_ref, ys_ref, h_ref, acc_ref):
    del blk_e_ref
    i = pl.program_id(0)
    f = pl.program_id(1)

    @pl.when(i < nact_ref[0])
    def _():
        @pl.when(f == 0)
        def _():
            h_ref[...] = _rmsnorm(xs_ref[...], g_ref[...]).astype(BF16)
            acc_ref[...] = jnp.zeros_like(acc_ref)

        h = h_ref[...]
        gate = jnp.dot(h, wg_ref[0], preferred_element_type=F32)
        up = jnp.dot(h, wu_ref[0], preferred_element_type=F32)
        act = (gate * _sigmoid(gate) * up).astype(BF16)
        acc_ref[...] += jnp.dot(act, wd_ref[0], preferred_element_type=F32)

        @pl.when(f == pl.num_programs(1) - 1)
        def _():
            ys_ref[...] = acc_ref[...]

    @pl.when((i >= nact_ref[0]) & (f == 0))
    def _():
        ys_ref[...] = jnp.zeros_like(ys_ref)


def _experts(blk_e, nact, xs, g, wg, wu, wd, *, rows, tf=512):
    n_rows, d = xs.shape
    ff = wg.shape[2]
    nf = ff // tf
    assert ff % tf == 0 and n_rows % rows == 0

    def blk(i, nact):
        return jnp.minimum(i, nact[0] - 1)

    def ff_tile(i, f, nact):
        return jnp.where(i < nact[0], f, nf - 1)

    grid_spec = pltpu.PrefetchScalarGridSpec(
        num_scalar_prefetch=2,
        grid=(n_rows // rows, nf),
        in_specs=[pl.BlockSpec((rows, d), lambda i, f, be, na: (blk(i, na), 0)),
                  pl.BlockSpec((1, d), lambda i, f, be, na: (0, 0)),
                  pl.BlockSpec((1, d, tf), lambda i, f, be, na: (be[blk(i, na)], 0, ff_tile(i, f, na))),
                  pl.BlockSpec((1, d, tf), lambda i, f, be, na: (be[blk(i, na)], 0, ff_tile(i, f, na))),
                  pl.BlockSpec((1, tf, d), lambda i, f, be, na: (be[blk(i, na)], ff_tile(i, f, na), 0))],
        out_specs=pl.BlockSpec((rows, d), lambda i, f, be, na: (i, 0)),
        scratch_shapes=[pltpu.VMEM((rows, d), BF16), pltpu.VMEM((rows, d), F32)],
    )
    return pl.pallas_call(
        _experts_kernel,
        grid_spec=grid_spec,
        out_shape=jax.ShapeDtypeStruct((n_rows, d), F32),
        compiler_params=_cparams("arbitrary", "arbitrary"),
        name="moe_experts",
    )(blk_e, nact, xs, g, wg, wu, wd)


def _combine_kernel(dest_ref, x_ref, gt_ref, fg_ref, ys_ref, o_ref, buf, sem, *, final_norm):
    tm = x_ref.shape[0]

    def row_copy(t, k):
        return pltpu.make_async_copy(ys_ref.at[pl.ds(dest_ref[0, 0, TOP_K * t + k], 1)], buf.at[k, pl.ds(t, 1)], sem)

    def start(t, carry):
        for k in range(TOP_K):
            row_copy(t, k).start()
        return carry

    def wait(t, carry):
        for k in range(TOP_K):
            row_copy(t, k).wait()
        return carry

    lax.fori_loop(0, tm, start, 0, unroll=8)
    lax.fori_loop(0, tm, wait, 0, unroll=8)
    gt = gt_ref[...]
    y = x_ref[...]
    for k in range(TOP_K):
        y = y + buf[k] * gt[:, k:k + 1]
    o_ref[...] = _rmsnorm(y, fg_ref[...]) if final_norm else y


def _combine(dest2, x2, gates, fg, ys, *, final_norm, tm=256):
    n, d = x2.shape
    return pl.pallas_call(
        functools.partial(_combine_kernel, final_norm=final_norm),
        grid=(n // tm,),
        in_specs=[pl.BlockSpec((1, 1, TOP_K * tm), lambda i: (i, 0, 0), memory_space=pltpu.SMEM),
                  pl.BlockSpec((tm, d), lambda i: (i, 0)),
                  pl.BlockSpec((tm, LANES), lambda i: (i, 0)),
                  _resident((1, d)),
                  pl.BlockSpec(memory_space=pl.ANY)],
        out_specs=pl.BlockSpec((tm, d), lambda i: (i, 0)),
        out_shape=jax.ShapeDtypeStruct((n, d), F32),
        scratch_shapes=[pltpu.VMEM((TOP_K, tm, d), F32), pltpu.SemaphoreType.DMA(())],
        compiler_params=_cparams("arbitrary"),
        name="moe_combine",
    )(dest2, x2, gates, fg, ys)


def _final_norm_kernel(x_ref, g_ref, o_ref):
    o_ref[...] = _rmsnorm(x_ref[...], g_ref[...])


def _final_norm(x2, g, *, tm=512):
    n, d = x2.shape
    return pl.pallas_call(
        _final_norm_kernel,
        grid=(n // tm,),
        in_specs=[pl.BlockSpec((tm, d), lambda i: (i, 0)), _resident((1, d))],
        out_specs=pl.BlockSpec((tm, d), lambda i: (i, 0)),
        out_shape=jax.ShapeDtypeStruct((n, d), F32),
        compiler_params=_cparams("parallel"),
        name="final_norm",
    )(x2, g)


def _token_mixing(x3, norm_g, w_in, pool_w_grp, pool_scale, conv_w, b_i, b_f, ml_norm_g, fox_b_f,
                  w_br_pool, w_br_ml, w_br_fox, w_out):
    b, s, d = x3.shape
    pool_w = w_br_pool.shape[0]
    ml_w = w_br_ml.shape[0]
    fox_w = w_br_fox.shape[0]
    ml_dh = ml_w // ML_HEADS
    fox_dh = fox_w // FOX_HEADS
    n_small = 2 * ML_HEADS + FOX_HEADS

    sizes = (pool_w, ml_w, ml_w, ml_w, ml_w, ML_HEADS, ML_HEADS, fox_w, fox_w, fox_w, FOX_HEADS, N_BRANCH * d)
    offs = [0]
    for sz in sizes:
        offs.append(offs[-1] + sz)
    assert offs[-1] == w_in.shape[1]
    (o_pool, o_q, o_k, o_v, o_o, o_i, o_f, o_fq, o_fk, o_fv, o_ff, o_g) = offs[:-1]

    def cols(o, sz):
        return w_in[:, o:o + sz]

    wm = jnp.concatenate([cols(o_g, N_BRANCH * d), cols(o_q, ml_w), cols(o_k, ml_w), cols(o_v, ml_w),
                          cols(o_o, ml_w), cols(o_pool, pool_w), cols(o_fq, fox_w), cols(o_fk, fox_w),
                          cols(o_fv, fox_w)], axis=1).astype(BF16)
    c_gate = 0
    c_qk = c_gate + N_BRANCH * d
    c_v = c_qk + 2 * ml_w
    c_o = c_v + ml_w
    c_pool = c_o + ml_w
    c_fq = c_pool + pool_w
    c_fk = c_fq + fox_w
    c_fv = c_fk + fox_w
    ws = jnp.concatenate([cols(o_i, ML_HEADS), cols(o_f, ML_HEADS), cols(o_ff, FOX_HEADS),
                          jnp.zeros((d, LANES - n_small), w_in.dtype)], axis=1).astype(BF16)
    bias = jnp.concatenate([b_i, b_f, fox_b_f, jnp.zeros((LANES - n_small,), F32)]).reshape(1, LANES)

    proj, small = _inproj(x3.reshape(b * s, d), norm_g.reshape(1, d), wm, ws)
    p3 = proj.reshape(b, s, -1)
    gcol, grow = _gateprep(small.reshape(b, s, LANES), bias)
    fo = _fox(p3, gcol, grow, q_col=c_fq, k_col=c_fk, v_col=c_fv, dh=fox_dh)
    hm = _mlstm(p3, gcol, grow, conv_w, ml_norm_g.reshape(1, ml_w), qk_col=c_qk, v_col=c_v, o_col=c_o, dh=ml_dh)
    return _merge(p3, hm, fo, x3, pool_w_grp.astype(BF16), pool_scale.reshape(1, pool_w),
                  w_br_pool.astype(BF16), w_br_ml.astype(BF16), w_br_fox.astype(BF16), w_out.astype(BF16),
                  pool_col=c_pool, gate_col=c_gate)


def _moe(x2, norm_g, w_router, b_router, w_gate, w_up, w_down, final_g, *, rows=512):
    n, d = x2.shape
    n_exp = w_router.shape[1]
    wr = jnp.concatenate([w_router, jnp.zeros((d, LANES - n_exp), F32)], axis=1)
    br = jnp.concatenate([b_router.astype(F32), jnp.full((LANES - n_exp,), NEG_BIG, F32)]).reshape(1, LANES)
    top_e, gates = _router(x2, norm_g.reshape(1, d), wr, br)

    e_flat = top_e[:, :TOP_K].reshape(n * TOP_K)
    onehot = (e_flat[:, None] == jnp.arange(n_exp, dtype=jnp.int32)[None, :]).astype(jnp.int32)
    csum = jnp.cumsum(onehot, axis=0)
    rank = jnp.sum(onehot * csum, axis=1) - 1
    counts = csum[-1]
    padded = ((counts + rows - 1) // rows) * rows
    pend = jnp.cumsum(padded)
    dest = ((pend - padded)[e_flat] + rank).astype(jnp.int32)
    n_blk = (n * TOP_K + n_exp * (rows - 1) + rows - 1) // rows
    blk_e = jnp.minimum(jnp.searchsorted(pend, jnp.arange(n_blk, dtype=jnp.int32) * rows, side='right'),
                        n_exp - 1).astype(jnp.int32)
    nact = (pend[-1:] // rows).astype(jnp.int32)

    tm = 256
    dest2 = dest.reshape(n // tm, 1, TOP_K * tm)
    xs = _dispatch(dest2, x2, jnp.zeros((n_blk * rows, d), F32), tm=tm)
    ys = _experts(blk_e, nact, xs, norm_g.reshape(1, d), w_gate.astype(BF16), w_up.astype(BF16),
                  w_down.astype(BF16), rows=rows)
    fg = jnp.ones((1, d), F32) if final_g is None else final_g.reshape(1, d)
    return _combine(dest2, x2, gates, fg, ys, final_norm=final_g is not None, tm=tm)


def kernel(x, mix_norm_g, w_in, pool_w_grp, pool_scale, ml_conv_w, ml_b_i, ml_b_f, ml_norm_g, fox_b_f,
           w_br_pool, w_br_ml, w_br_fox, w_out, ffn_norm_g, ff_w_gate, ff_w_up, ff_w_down,
           moe_w_router, moe_b_router, moe_w_gate, moe_w_up, moe_w_down, final_norm_g):
    b, s, d = x.shape
    depth = mix_norm_g.shape[0]
    fused_final = False
    for l in range(depth):
        x = _token_mixing(x, mix_norm_g[l], w_in[l], pool_w_grp[l], pool_scale[l], ml_conv_w[l], ml_b_i[l],
                          ml_b_f[l], ml_norm_g[l], fox_b_f[l], w_br_pool[l], w_br_ml[l], w_br_fox[l], w_out[l])
        x2 = x.reshape(b * s, d)
        if l % 2 == 0:
            x2 = _swiglu(x2, ffn_norm_g[l].reshape(1, d), ff_w_gate[l // 2].astype(BF16),
                         ff_w_up[l // 2].astype(BF16), ff_w_down[l // 2].astype(BF16))
        else:
            fused_final = l == depth - 1
            x2 = _moe(x2, ffn_norm_g[l], moe_w_router[l // 2], moe_b_router[l // 2], moe_w_gate[l // 2],
                      moe_w_up[l // 2], moe_w_down[l // 2], final_norm_g if fused_final else None)
        x = x2.reshape(b, s, d)
    if not fused_final:
        x = _final_norm(x.reshape(b * s, d), final_norm_g.reshape(1, d)).reshape(b, s, d)
    return x
```

```python
import functools
import math

import jax
import jax.numpy as jnp
import numpy as np
from jax import lax
from jax.experimental import pallas as pl
from jax.experimental.pallas import tpu as pltpu

F32 = jnp.float32
BF16 = jnp.bfloat16

NORM_EPS = 1e-6
POOL_WINDOWS = (2, 4, 8, 16)
ML_HEADS = 4
FOX_HEADS = 8
TOP_K = 2
N_BRANCH = 3

LANES = 128
BF16_SUBLANES = 16
VMEM_LIMIT_BYTES = 56 * 1024 * 1024

NEG_BIG = -1e30


def _cparams(*sem):
    return pltpu.CompilerParams(dimension_semantics=sem, vmem_limit_bytes=VMEM_LIMIT_BYTES)


def _resident(shape):
    zeros = (0,) * len(shape)
    return pl.BlockSpec(shape, lambda *_: zeros, pipeline_mode=pl.Buffered(1))


def _rmsnorm(x, g):
    return x * lax.rsqrt(jnp.mean(x * x, axis=-1, keepdims=True) + NORM_EPS) * g


def _sigmoid(x):
    return 1.0 / (1.0 + jnp.exp(-x))


def _log_sigmoid(x):
    return jnp.minimum(x, 0.0) - jnp.log(1.0 + jnp.exp(-jnp.abs(x)))


def _split3(x):
    hi = x.astype(BF16)
    r = x - hi.astype(F32)
    mid = r.astype(BF16)
    lo = (r - mid.astype(F32)).astype(BF16)
    return hi, mid, lo


def _cumsum_rows(tril, x):
    return sum(jnp.dot(tril, part, preferred_element_type=F32) for part in _split3(x))


def _cumsum_lanes(x, triu):
    return sum(jnp.dot(part, triu, preferred_element_type=F32) for part in _split3(x))


def _inproj_kernel(x_ref, g_ref, wm_ref, ws_ref, p_ref, s_ref, *, col_chunk):
    h = _rmsnorm(x_ref[...], g_ref[...]).astype(BF16)
    s_ref[...] = jnp.dot(h, ws_ref[...], preferred_element_type=F32)
    for c in range(wm_ref.shape[1] // col_chunk):
        sl = slice(c * col_chunk, (c + 1) * col_chunk)
        p_ref[:, sl] = jnp.dot(h, wm_ref[:, sl], preferred_element_type=F32).astype(BF16)


def _inproj(x2, g, wm, ws, *, tm=512, col_chunk=1024):
    n, d = x2.shape
    wcols = wm.shape[1]
    return pl.pallas_call(
        functools.partial(_inproj_kernel, col_chunk=col_chunk),
        grid=(n // tm,),
        in_specs=[pl.BlockSpec((tm, d), lambda i: (i, 0)),
                  _resident((1, d)), _resident((d, wcols)), _resident((d, LANES))],
        out_specs=[pl.BlockSpec((tm, wcols), lambda i: (i, 0)),
                   pl.BlockSpec((tm, LANES), lambda i: (i, 0))],
        out_shape=[jax.ShapeDtypeStruct((n, wcols), BF16), jax.ShapeDtypeStruct((n, LANES), F32)],
        compiler_params=_cparams("parallel"),
        name="inproj",
    )(x2, g, wm, ws)


AUG_TERMS = 3
AUG_STRIDE = 8


def _aug_placement(dh):
    pairs = FOX_HEADS * dh // LANES
    width = pairs * LANES
    pq = np.zeros((AUG_TERMS * LANES, width), np.float32)
    pk = np.zeros((AUG_TERMS * LANES, width), np.float32)
    cq = np.zeros((1, width), np.float32)
    ck = np.zeros((1, width), np.float32)
    for h in range(FOX_HEADS):
        base = (h // 2) * LANES + (h % 2) * AUG_STRIDE
        for t in range(AUG_TERMS):
            src = t * LANES + 2 * ML_HEADS + h
            pq[src, base + t] = 1.0
            pk[src, base + AUG_TERMS + t] = -1.0
            cq[0, base + AUG_TERMS + t] = 1.0
            ck[0, base + t] = 1.0
    return jnp.asarray(pq, BF16), jnp.asarray(pk, BF16), jnp.asarray(cq), jnp.asarray(ck)


def _gateprep_kernel(s_ref, b_ref, pq_ref, pk_ref, cq_ref, ck_ref, col_ref, row_ref, aq_ref, ak_ref, carry_ref):
    t = s_ref.shape[1]

    @pl.when(pl.program_id(1) == 0)
    def _():
        carry_ref[...] = jnp.zeros_like(carry_ref)

    pre = s_ref[0] + b_ref[...]
    lane = lax.broadcasted_iota(jnp.int32, pre.shape, 1)
    ls = _log_sigmoid(pre)
    fox = (lane >= 2 * ML_HEADS) & (lane < 2 * ML_HEADS + FOX_HEADS)
    tril = (lax.broadcasted_iota(jnp.int32, (t, t), 0) >= lax.broadcasted_iota(jnp.int32, (t, t), 1)).astype(BF16)
    fcum = _cumsum_rows(tril, jnp.where(fox, ls, 0.0)) + carry_ref[...]
    carry_ref[...] = fcum[t - 1:t, :]
    col = jnp.where(lane < ML_HEADS, pre, jnp.where(lane < 2 * ML_HEADS, ls, fcum))
    col_ref[0] = col
    row_ref[0] = col.T[0:row_ref.shape[1], :]
    parts = jnp.concatenate(_split3(fcum), axis=-1)
    aq_ref[0] = (jnp.dot(parts, pq_ref[...], preferred_element_type=F32) + cq_ref[...]).astype(BF16)
    ak_ref[0] = (jnp.dot(parts, pk_ref[...], preferred_element_type=F32) + ck_ref[...]).astype(BF16)


def _gateprep(s3, bias, dh, *, tg=512):
    b, s, _ = s3.shape
    n_rows = 2 * ML_HEADS + FOX_HEADS
    pq, pk, cq, ck = _aug_placement(dh)
    width = pq.shape[1]
    return pl.pallas_call(
        _gateprep_kernel,
        grid=(b, s // tg),
        in_specs=[pl.BlockSpec((1, tg, LANES), lambda i, j: (i, j, 0)), _resident((1, LANES)),
                  _resident(pq.shape), _resident(pk.shape), _resident(cq.shape), _resident(ck.shape)],
        out_specs=[pl.BlockSpec((1, tg, LANES), lambda i, j: (i, j, 0)),
                   pl.BlockSpec((1, n_rows, tg), lambda i, j: (i, 0, j)),
                   pl.BlockSpec((1, tg, width), lambda i, j: (i, j, 0)),
                   pl.BlockSpec((1, tg, width), lambda i, j: (i, j, 0))],
        out_shape=[jax.ShapeDtypeStruct((b, s, LANES), F32), jax.ShapeDtypeStruct((b, n_rows, s), F32),
                   jax.ShapeDtypeStruct((b, s, width), BF16), jax.ShapeDtypeStruct((b, s, width), BF16)],
        scratch_shapes=[pltpu.VMEM((1, LANES), F32)],
        compiler_params=_cparams("parallel", "arbitrary"),
        name="gateprep",
    )(s3, bias, pq, pk, cq, ck)


def _fox_kernel(q_ref, aq_ref, k_ref, ak_ref, v_ref, o_ref, m_ref, acc_ref, *, tq, tk, dh, scale):
    qi = pl.program_id(2)
    lane = lax.broadcasted_iota(jnp.int32, (tq, LANES), 1)
    q2 = q_ref[0] * scale
    aq = aq_ref[0]
    in_head = [(lane >= a * dh) & (lane < (a + 1) * dh) for a in range(2)]
    q_ops = []
    for a in range(2):
        in_aug = (lane >= a * AUG_STRIDE) & (lane < a * AUG_STRIDE + 2 * AUG_TERMS)
        q_ops.append(jnp.concatenate([jnp.where(in_head[a], q2, jnp.zeros_like(q2)),
                                      jnp.where(in_aug, aq, jnp.zeros_like(aq))], axis=-1))
    m_ref[...] = jnp.full_like(m_ref, NEG_BIG)
    acc_ref[...] = jnp.zeros_like(acc_ref)
    causal = (lax.broadcasted_iota(jnp.int32, (tq, tq), 1) <= lax.broadcasted_iota(jnp.int32, (tq, tq), 0))

    def step(start, width, masked):
        kb = jnp.concatenate([k_ref[0, pl.ds(start, width), :], ak_ref[0, pl.ds(start, width), :]], axis=-1)
        vb = v_ref[0, pl.ds(start, width), :]
        key_lane = lax.broadcasted_iota(jnp.int32, (width, LANES), 1)
        key_head = [(key_lane >= a * dh) & (key_lane < (a + 1) * dh) for a in range(2)]
        for a in range(2):
            s = lax.dot_general(q_ops[a], kb, (((1,), (1,)), ((), ())), preferred_element_type=F32)
            if masked:
                s = jnp.where(causal, s, NEG_BIG)
            m_prev = m_ref[a]
            m_new = jnp.maximum(m_prev, jnp.max(s, axis=-1, keepdims=True))
            alpha = jnp.exp(m_prev - m_new)
            pexp = jnp.exp(s - jnp.concatenate([m_new] * (width // LANES), axis=-1))
            v_op = jnp.where(key_head[a], vb, jnp.ones_like(vb))
            acc_ref[a] = alpha * acc_ref[a] + jnp.dot(pexp.astype(BF16), v_op, preferred_element_type=F32)
            m_ref[a] = m_new

    n_wide = (qi * tq) // tk
    n_narrow = qi - n_wide * (tk // tq)

    def wide_body(j, carry):
        step(pl.multiple_of(j * tk, tk), tk, False)
        return carry

    def narrow_body(j, carry):
        step(pl.multiple_of((n_wide * (tk // tq) + j) * tq, tq), tq, False)
        return carry

    lax.fori_loop(0, n_wide, wide_body, 0)
    if tk != tq:
        lax.fori_loop(0, n_narrow, narrow_body, 0)
    step(pl.multiple_of(qi * tq, tq), tq, True)
    outs = [acc_ref[a] / pltpu.roll(acc_ref[a], LANES // 2, 1) for a in range(2)]
    o_ref[0] = jnp.where(in_head[0], outs[0], outs[1]).astype(BF16)


def _fox(p3, aq, ak, *, q_col, k_col, v_col, dh, tq=1024, tk=1024):
    b, s, _ = p3.shape
    assert 2 * dh == LANES, "two heads share one 128-lane block"
    assert tk % tq == 0 and s % tq == 0
    pairs = FOX_HEADS * dh // LANES
    scale = dh ** -0.5
    assert math.frexp(scale)[0] == 0.5, "score scale is folded into bf16 q; exact only for a power of two"
    qb, kb, vb = q_col // LANES, k_col // LANES, v_col // LANES
    return pl.pallas_call(
        functools.partial(_fox_kernel, tq=tq, tk=tk, dh=dh, scale=scale),
        grid=(b, pairs, s // tq),
        in_specs=[pl.BlockSpec((1, tq, LANES), lambda i, p, q: (i, q, qb + p)),
                  pl.BlockSpec((1, tq, LANES), lambda i, p, q: (i, q, p)),
                  pl.BlockSpec((1, s, LANES), lambda i, p, q: (i, 0, kb + p)),
                  pl.BlockSpec((1, s, LANES), lambda i, p, q: (i, 0, p)),
                  pl.BlockSpec((1, s, LANES), lambda i, p, q: (i, 0, vb + p))],
        out_specs=pl.BlockSpec((1, tq, LANES), lambda i, p, q: (i, q, p)),
        out_shape=jax.ShapeDtypeStruct((b, s, pairs * LANES), BF16),
        scratch_shapes=[pltpu.VMEM((2, tq, LANES), F32), pltpu.VMEM((2, tq, LANES), F32)],
        compiler_params=_cparams("parallel", "parallel", "arbitrary"),
        name="fox_attention",
    )(p3, aq, p3, ak, p3)


def _mlstm_kernel(qk_ref, halo_ref, v_ref, og_ref, gc_ref, gr_ref, cw_ref, ng_ref, out_ref,
                  ubuf, ct_ref, n_ref, m_ref, *, chunk, dh):
    c = pl.program_id(1)
    width = ML_HEADS * dh
    taps = cw_ref.shape[0]
    pad = halo_ref.shape[1]

    @pl.when(c == 0)
    def _():
        ct_ref[...] = jnp.zeros_like(ct_ref)
        n_ref[...] = jnp.zeros_like(n_ref)
        m_ref[...] = jnp.zeros_like(m_ref)

    ubuf[0:pad, :] = jnp.where(c == 0, 0.0, halo_ref[0].astype(F32))
    ubuf[pad:pad + chunk, :] = qk_ref[0].astype(F32)

    def conv_silu(col0):
        y = sum(ubuf[pad - taps + 1 + j:pad - taps + 1 + j + chunk, col0:col0 + dh] * cw_ref[j:j + 1, col0:col0 + dh]
                for j in range(taps))
        return y * _sigmoid(y)

    ri = lax.broadcasted_iota(jnp.int32, (chunk, chunk), 0)
    ci = lax.broadcasted_iota(jnp.int32, (chunk, chunk), 1)
    lower = ri >= ci
    gc = gc_ref[0]
    gr = gr_ref[0]
    bc_all = _cumsum_rows(lower.astype(BF16), gc)
    br_all = _cumsum_lanes(gr, (ri <= ci).astype(BF16))

    for h in range(ML_HEADS):
        hs = slice(h * dh, (h + 1) * dh)
        q = conv_silu(h * dh)
        k = conv_silu(width + h * dh) * dh ** -0.5
        qb, kb = q.astype(BF16), k.astype(BF16)
        vb = v_ref[0, :, hs]
        li_c = gc[:, h:h + 1]
        b_c = bc_all[:, ML_HEADS + h:ML_HEADS + h + 1]
        li_r = gr[h:h + 1, :]
        b_r = br_all[ML_HEADS + h:ML_HEADS + h + 1, :]
        b_last = b_c[chunk - 1:chunk, :]
        m_prev = m_ref[h:h + 1, 0:1]

        dmat = jnp.where(lower, b_c - (b_r - li_r), NEG_BIG)
        m_inter = b_c + m_prev
        m_t = jnp.maximum(m_inter, jnp.max(dmat, axis=-1, keepdims=True))
        w_inter = jnp.exp(m_inter - m_t)
        pm = jnp.exp(dmat - m_t) * lax.dot_general(qb, kb, (((1,), (1,)), ((), ())), preferred_element_type=F32)
        num = (w_inter * jnp.dot(qb, ct_ref[h].astype(BF16), preferred_element_type=F32)
               + jnp.dot(pm.astype(BF16), vb, preferred_element_type=F32))
        den = (w_inter * jnp.sum(q * n_ref[h], axis=-1, keepdims=True)
               + jnp.sum(pm, axis=-1, keepdims=True))
        hh = num / jnp.maximum(jnp.abs(den), jnp.exp(-m_t))

        g_c = b_last - b_c + li_c
        m_new = jnp.maximum(b_last + m_prev, jnp.max(g_c, axis=0, keepdims=True))
        decay = jnp.exp(b_last + m_prev - m_new)
        kw = k * jnp.exp(g_c - m_new)
        ct_ref[h] = decay * ct_ref[h] + lax.dot_general(kw.astype(BF16), vb, (((0,), (0,)), ((), ())),
                                                       preferred_element_type=F32)
        n_ref[h] = decay * n_ref[h] + jnp.sum(kw, axis=0, keepdims=True)
        m_ref[h:h + 1, :] = jnp.broadcast_to(m_new, (1, LANES))

        hn = hh * lax.rsqrt(jnp.mean(hh * hh, axis=-1, keepdims=True) + NORM_EPS) * ng_ref[:, hs]
        out_ref[0, :, hs] = (hn * _sigmoid(og_ref[0, :, hs].astype(F32))).astype(BF16)


def _mlstm(p3, gcol, grow, conv_w, norm_g, *, qk_col, v_col, o_col, dh, chunk=256):
    b, s, _ = p3.shape
    width = ML_HEADS * dh
    pad = BF16_SUBLANES
    assert conv_w.shape[0] - 1 <= pad
    qkb, vb, ob = qk_col // (2 * width), v_col // width, o_col // width
    halo_per_chunk = chunk // pad
    return pl.pallas_call(
        functools.partial(_mlstm_kernel, chunk=chunk, dh=dh),
        grid=(b, s // chunk),
        in_specs=[pl.BlockSpec((1, chunk, 2 * width), lambda i, c: (i, c, qkb)),
                  pl.BlockSpec((1, pad, 2 * width),
                               lambda i, c: (i, jnp.maximum(c * halo_per_chunk - 1, 0), qkb)),
                  pl.BlockSpec((1, chunk, width), lambda i, c: (i, c, vb)),
                  pl.BlockSpec((1, chunk, width), lambda i, c: (i, c, ob)),
                  pl.BlockSpec((1, chunk, LANES), lambda i, c: (i, c, 0)),
                  pl.BlockSpec((1, grow.shape[1], chunk), lambda i, c: (i, 0, c)),
                  _resident(conv_w.shape), _resident((1, width))],
        out_specs=pl.BlockSpec((1, chunk, width), lambda i, c: (i, c, 0)),
        out_shape=jax.ShapeDtypeStruct((b, s, width), BF16),
        scratch_shapes=[pltpu.VMEM((pad + chunk, 2 * width), F32),
                        pltpu.VMEM((ML_HEADS, dh, dh), F32),
                        pltpu.VMEM((ML_HEADS, 1, dh), F32),
                        pltpu.VMEM((ML_HEADS, LANES), F32)],
        compiler_params=_cparams("parallel", "arbitrary"),
        name="mlstm",
    )(p3, p3, p3, p3, gcol, grow, conv_w, norm_g)


def _merge_kernel(u_ref, halo_ref, hm_ref, fo_ref, gp_ref, x_ref, wg_ref, ps_ref, wbp_ref, wbm_ref, wbf_ref,
                  wo_ref, o_ref, ubuf, *, tm):
    j = pl.program_id(1)
    pad = halo_ref.shape[1]
    gw = wg_ref.shape[1]
    d = x_ref.shape[2]
    ubuf[0:pad, :] = jnp.where(j == 0, 0.0, halo_ref[0].astype(F32))
    ubuf[pad:pad + tm, :] = u_ref[0].astype(F32)
    pos = j * tm + lax.broadcasted_iota(jnp.int32, (tm, 1), 0)
    ys = []
    for g, w in enumerate(POOL_WINDOWS):
        gs = slice(g * gw, (g + 1) * gw)
        wsum = sum(ubuf[pad - k:pad - k + tm, gs] for k in range(w))
        cnt = jnp.minimum(pos + 1, w).astype(F32)
        dlt = wsum / cnt - ubuf[pad:pad + tm, gs]
        ys.append(jnp.dot(dlt.astype(BF16), wg_ref[g], preferred_element_type=F32))
    y_pool = (jnp.concatenate(ys, axis=-1) * ps_ref[...]).astype(BF16)
    merged = _sigmoid(gp_ref[0, :, 0:d].astype(F32)) * jnp.dot(y_pool, wbp_ref[...], preferred_element_type=F32)
    merged += _sigmoid(gp_ref[0, :, d:2 * d].astype(F32)) * jnp.dot(hm_ref[0], wbm_ref[...],
                                                                   preferred_element_type=F32)
    merged += _sigmoid(gp_ref[0, :, 2 * d:3 * d].astype(F32)) * jnp.dot(fo_ref[0], wbf_ref[...],
                                                                       preferred_element_type=F32)
    o_ref[0] = x_ref[0] + jnp.dot(merged.astype(BF16), wo_ref[...], preferred_element_type=F32)


def _merge(p3, hm, fo, x3, wgrp, pscale, wbp, wbm, wbf, wo, *, pool_col, gate_col, tm=256):
    b, s, d = x3.shape
    pw = wbp.shape[0]
    pad = BF16_SUBLANES
    assert max(POOL_WINDOWS) - 1 <= pad and gate_col == 0
    pcb = pool_col // pw
    halo_per_blk = tm // pad
    return pl.pallas_call(
        functools.partial(_merge_kernel, tm=tm),
        grid=(b, s // tm),
        in_specs=[pl.BlockSpec((1, tm, pw), lambda i, j: (i, j, pcb)),
                  pl.BlockSpec((1, pad, pw), lambda i, j: (i, jnp.maximum(j * halo_per_blk - 1, 0), pcb)),
                  pl.BlockSpec((1, tm, hm.shape[2]), lambda i, j: (i, j, 0)),
                  pl.BlockSpec((1, tm, fo.shape[2]), lambda i, j: (i, j, 0)),
                  pl.BlockSpec((1, tm, N_BRANCH * d), lambda i, j: (i, j, 0)),
                  pl.BlockSpec((1, tm, d), lambda i, j: (i, j, 0)),
                  _resident(wgrp.shape), _resident(pscale.shape), _resident(wbp.shape),
                  _resident(wbm.shape), _resident(wbf.shape), _resident(wo.shape)],
        out_specs=pl.BlockSpec((1, tm, d), lambda i, j: (i, j, 0)),
        out_shape=jax.ShapeDtypeStruct((b, s, d), F32),
        scratch_shapes=[pltpu.VMEM((pad + tm, pw), F32)],
        compiler_params=_cparams("parallel", "parallel"),
        name="merge",
    )(p3, p3, hm, fo, p3, x3, wgrp, pscale, wbp, wbm, wbf, wo)


def _swiglu_kernel(x_ref, g_ref, wg_ref, wu_ref, wd_ref, o_ref, a_ref, *, ff_chunk):
    x = x_ref[...]
    h = _rmsnorm(x, g_ref[...]).astype(BF16)
    for c in range(wg_ref.shape[1] // ff_chunk):
        sl = slice(c * ff_chunk, (c + 1) * ff_chunk)
        gate = jnp.dot(h, wg_ref[:, sl], preferred_element_type=F32)
        up = jnp.dot(h, wu_ref[:, sl], preferred_element_type=F32)
        a_ref[:, sl] = (gate * _sigmoid(gate) * up).astype(BF16)
    o_ref[...] = x + jnp.dot(a_ref[...], wd_ref[...], preferred_element_type=F32)


def _swiglu(x2, g, wg, wu, wd, *, tm=512, ff_chunk=256):
    n, d = x2.shape
    ff = wg.shape[1]
    assert ff % ff_chunk == 0
    return pl.pallas_call(
        functools.partial(_swiglu_kernel, ff_chunk=ff_chunk),
        grid=(n // tm,),
        in_specs=[pl.BlockSpec((tm, d), lambda i: (i, 0)), _resident((1, d)),
                  _resident(wg.shape), _resident(wu.shape), _resident(wd.shape)],
        out_specs=pl.BlockSpec((tm, d), lambda i: (i, 0)),
        out_shape=jax.ShapeDtypeStruct((n, d), F32),
        scratch_shapes=[pltpu.VMEM((tm, ff), BF16)],
        compiler_params=_cparams("parallel"),
        name="dense_swiglu",
    )(x2, g, wg, wu, wd)


def _router_kernel(x_ref, g_ref, wr_ref, br_ref, e_ref, gt_ref):
    h = _rmsnorm(x_ref[...], g_ref[...])
    logits = jnp.dot(h, wr_ref[...], preferred_element_type=F32, precision=lax.Precision.HIGHEST) + br_ref[...]
    lane = lax.broadcasted_iota(jnp.int32, logits.shape, 1)
    m1 = jnp.max(logits, axis=-1, keepdims=True)
    i1 = jnp.min(jnp.where(logits == m1, lane, LANES), axis=-1, keepdims=True)
    rest = jnp.where(lane == i1, NEG_BIG, logits)
    m2 = jnp.max(rest, axis=-1, keepdims=True)
    i2 = jnp.min(jnp.where(rest == m2, lane, LANES), axis=-1, keepdims=True)
    e2 = jnp.exp(m2 - m1)
    g1 = 1.0 / (1.0 + e2)
    e_ref[...] = jnp.where(lane == 0, i1, jnp.where(lane == 1, i2, 0))
    gt_ref[...] = jnp.where(lane == 0, g1, jnp.where(lane == 1, e2 * g1, 0.0))


def _router(x2, g, wr, br, *, tm=512):
    n, d = x2.shape
    return pl.pallas_call(
        _router_kernel,
        grid=(n // tm,),
        in_specs=[pl.BlockSpec((tm, d), lambda i: (i, 0)), _resident((1, d)),
                  _resident((d, LANES)), _resident((1, LANES))],
        out_specs=[pl.BlockSpec((tm, LANES), lambda i: (i, 0)), pl.BlockSpec((tm, LANES), lambda i: (i, 0))],
        out_shape=[jax.ShapeDtypeStruct((n, LANES), jnp.int32), jax.ShapeDtypeStruct((n, LANES), F32)],
        compiler_params=_cparams("parallel"),
        name="router",
    )(x2, g, wr, br)


def _dispatch_kernel(dest_ref, x_ref, xs_in_ref, xs_ref, sem):
    del xs_in_ref
    tm = x_ref.shape[0]

    def row_copy(t, k):
        return pltpu.make_async_copy(x_ref.at[pl.ds(t, 1)], xs_ref.at[pl.ds(dest_ref[0, 0, TOP_K * t + k], 1)], sem)

    def start(t, carry):
        for k in range(TOP_K):
            row_copy(t, k).start()
        return carry

    def wait(t, carry):
        for k in range(TOP_K):
            row_copy(t, k).wait()
        return carry

    lax.fori_loop(0, tm, start, 0, unroll=8)
    lax.fori_loop(0, tm, wait, 0, unroll=8)


def _dispatch(dest2, x2, xs_init, *, tm=256):
    n, d = x2.shape
    return pl.pallas_call(
        _dispatch_kernel,
        grid=(n // tm,),
        in_specs=[pl.BlockSpec((1, 1, TOP_K * tm), lambda i: (i, 0, 0), memory_space=pltpu.SMEM),
                  pl.BlockSpec((tm, d), lambda i: (i, 0)),
                  pl.BlockSpec(memory_space=pl.ANY)],
        out_specs=pl.BlockSpec(memory_space=pl.ANY),
        out_shape=jax.ShapeDtypeStruct(xs_init.shape, xs_init.dtype),
        scratch_shapes=[pltpu.SemaphoreType.DMA(())],
        input_output_aliases={2: 0},
        compiler_params=_cparams("arbitrary"),
        name="moe_dispatch",
    )(dest2, x2, xs_init)


def _experts_kernel(blk_e_ref, nact_ref, xs_ref, g_ref, wg_ref, wu_ref, wd_ref, ys_ref, h_ref, acc_ref):
    del blk_e_ref
    i = pl.program_id(0)
    f = pl.program_id(1)

    @pl.when(i < nact_ref[0])
    def _():
        @pl.when(f == 0)
        def _():
            h_ref[...] = _rmsnorm(xs_ref[...], g_ref[...]).astype(BF16)
            acc_ref[...] = jnp.zeros_like(acc_ref)

        h = h_ref[...]
        gate = jnp.dot(h, wg_ref[0], preferred_element_type=F32)
        up = jnp.dot(h, wu_ref[0], preferred_element_type=F32)
        act = (gate * _sigmoid(gate) * up).astype(BF16)
        acc_ref[...] += jnp.dot(act, wd_ref[0], preferred_element_type=F32)

        @pl.when(f == pl.num_programs(1) - 1)
        def _():
            ys_ref[...] = acc_ref[...]

    @pl.when((i >= nact_ref[0]) & (f == 0))
    def _():
        ys_ref[...] = jnp.zeros_like(ys_ref)


def _experts(blk_e, nact, xs, g, wg, wu, wd, *, rows, tf=512):
    n_rows, d = xs.shape
    ff = wg.shape[2]
    nf = ff // tf
    assert ff % tf == 0 and n_rows % rows == 0

    def blk(i, nact):
        return jnp.minimum(i, nact[0] - 1)

    def ff_tile(i, f, nact):
        return jnp.where(i < nact[0], f, nf - 1)

    grid_spec = pltpu.PrefetchScalarGridSpec(
        num_scalar_prefetch=2,
        grid=(n_rows // rows, nf),
        in_specs=[pl.BlockSpec((rows, d), lambda i, f, be, na: (blk(i, na), 0)),
                  pl.BlockSpec((1, d), lambda i, f, be, na: (0, 0)),
                  pl.BlockSpec((1, d, tf), lambda i, f, be, na: (be[blk(i, na)], 0, ff_tile(i, f, na))),
                  pl.BlockSpec((1, d, tf), lambda i, f, be, na: (be[blk(i, na)], 0, ff_tile(i, f, na))),
                  pl.BlockSpec((1, tf, d), lambda i, f, be, na: (be[blk(i, na)], ff_tile(i, f, na), 0))],
        out_specs=pl.BlockSpec((rows, d), lambda i, f, be, na: (i, 0)),
        scratch_shapes=[pltpu.VMEM((rows, d), BF16), pltpu.VMEM((rows, d), F32)],
    )
    return pl.pallas_call(
        _experts_kernel,
        grid_spec=grid_spec,
        out_shape=jax.ShapeDtypeStruct((n_rows, d), F32),
        compiler_params=_cparams("arbitrary", "arbitrary"),
        name="moe_experts",
    )(blk_e, nact, xs, g, wg, wu, wd)


def _combine_kernel(dest_ref, x_ref, gt_ref, fg_ref, ys_ref, o_ref, buf, sem, *, final_norm):
    tm = x_ref.shape[0]

    def row_copy(t, k):
        return pltpu.make_async_copy(ys_ref.at[pl.ds(dest_ref[0, 0, TOP_K * t + k], 1)], buf.at[k, pl.ds(t, 1)], sem)

    def start(t, carry):
        for k in range(TOP_K):
            row_copy(t, k).start()
        return carry

    def wait(t, carry):
        for k in range(TOP_K):
            row_copy(t, k).wait()
        return carry

    lax.fori_loop(0, tm, start, 0, unroll=8)
    lax.fori_loop(0, tm, wait, 0, unroll=8)
    gt = gt_ref[...]
    y = x_ref[...]
    for k in range(TOP_K):
        y = y + buf[k] * gt[:, k:k + 1]
    o_ref[...] = _rmsnorm(y, fg_ref[...]) if final_norm else y


def _combine(dest2, x2, gates, fg, ys, *, final_norm, tm=256):
    n, d = x2.shape
    return pl.pallas_call(
        functools.partial(_combine_kernel, final_norm=final_norm),
        grid=(n // tm,),
        in_specs=[pl.BlockSpec((1, 1, TOP_K * tm), lambda i: (i, 0, 0), memory_space=pltpu.SMEM),
                  pl.BlockSpec((tm, d), lambda i: (i, 0)),
                  pl.BlockSpec((tm, LANES), lambda i: (i, 0)),
                  _resident((1, d)),
                  pl.BlockSpec(memory_space=pl.ANY)],
        out_specs=pl.BlockSpec((tm, d), lambda i: (i, 0)),
        out_shape=jax.ShapeDtypeStruct((n, d), F32),
        scratch_shapes=[pltpu.VMEM((TOP_K, tm, d), F32), pltpu.SemaphoreType.DMA(())],
        compiler_params=_cparams("arbitrary"),
        name="moe_combine",
    )(dest2, x2, gates, fg, ys)


def _final_norm_kernel(x_ref, g_ref, o_ref):
    o_ref[...] = _rmsnorm(x_ref[...], g_ref[...])


def _final_norm(x2, g, *, tm=512):
    n, d = x2.shape
    return pl.pallas_call(
        _final_norm_kernel,
        grid=(n // tm,),
        in_specs=[pl.BlockSpec((tm, d), lambda i: (i, 0)), _resident((1, d))],
        out_specs=pl.BlockSpec((tm, d), lambda i: (i, 0)),
        out_shape=jax.ShapeDtypeStruct((n, d), F32),
        compiler_params=_cparams("parallel"),
        name="final_norm",
    )(x2, g)


def _token_mixing(x3, norm_g, w_in, pool_w_grp, pool_scale, conv_w, b_i, b_f, ml_norm_g, fox_b_f,
                  w_br_pool, w_br_ml, w_br_fox, w_out):
    b, s, d = x3.shape
    pool_w = w_br_pool.shape[0]
    ml_w = w_br_ml.shape[0]
    fox_w = w_br_fox.shape[0]
    ml_dh = ml_w // ML_HEADS
    fox_dh = fox_w // FOX_HEADS
    n_small = 2 * ML_HEADS + FOX_HEADS

    sizes = (pool_w, ml_w, ml_w, ml_w, ml_w, ML_HEADS, ML_HEADS, fox_w, fox_w, fox_w, FOX_HEADS, N_BRANCH * d)
    offs = [0]
    for sz in sizes:
        offs.append(offs[-1] + sz)
    assert offs[-1] == w_in.shape[1]
    (o_pool, o_q, o_k, o_v, o_o, o_i, o_f, o_fq, o_fk, o_fv, o_ff, o_g) = offs[:-1]

    def cols(o, sz):
        return w_in[:, o:o + sz]

    wm = jnp.concatenate([cols(o_g, N_BRANCH * d), cols(o_q, ml_w), cols(o_k, ml_w), cols(o_v, ml_w),
                          cols(o_o, ml_w), cols(o_pool, pool_w), cols(o_fq, fox_w), cols(o_fk, fox_w),
                          cols(o_fv, fox_w)], axis=1).astype(BF16)
    c_gate = 0
    c_qk = c_gate + N_BRANCH * d
    c_v = c_qk + 2 * ml_w
    c_o = c_v + ml_w
    c_pool = c_o + ml_w
    c_fq = c_pool + pool_w
    c_fk = c_fq + fox_w
    c_fv = c_fk + fox_w
    ws = jnp.concatenate([cols(o_i, ML_HEADS), cols(o_f, ML_HEADS), cols(o_ff, FOX_HEADS),
                          jnp.zeros((d, LANES - n_small), w_in.dtype)], axis=1).astype(BF16)
    bias = jnp.concatenate([b_i, b_f, fox_b_f, jnp.zeros((LANES - n_small,), F32)]).reshape(1, LANES)

    proj, small = _inproj(x3.reshape(b * s, d), norm_g.reshape(1, d), wm, ws)
    p3 = proj.reshape(b, s, -1)
    gcol, grow, aq, ak = _gateprep(small.reshape(b, s, LANES), bias, fox_dh)
    fo = _fox(p3, aq, ak, q_col=c_fq, k_col=c_fk, v_col=c_fv, dh=fox_dh)
    hm = _mlstm(p3, gcol, grow, conv_w, ml_norm_g.reshape(1, ml_w), qk_col=c_qk, v_col=c_v, o_col=c_o, dh=ml_dh)
    return _merge(p3, hm, fo, x3, pool_w_grp.astype(BF16), pool_scale.reshape(1, pool_w),
                  w_br_pool.astype(BF16), w_br_ml.astype(BF16), w_br_fox.astype(BF16), w_out.astype(BF16),
                  pool_col=c_pool, gate_col=c_gate)


def _moe(x2, norm_g, w_router, b_router, w_gate, w_up, w_down, final_g, *, rows=512):
    n, d = x2.shape
    n_exp = w_router.shape[1]
    wr = jnp.concatenate([w_router, jnp.zeros((d, LANES - n_exp), F32)], axis=1)
    br = jnp.concatenate([b_router.astype(F32), jnp.full((LANES - n_exp,), NEG_BIG, F32)]).reshape(1, LANES)
    top_e, gates = _router(x2, norm_g.reshape(1, d), wr, br)

    e_flat = top_e[:, :TOP_K].reshape(n * TOP_K)
    onehot = (e_flat[:, None] == jnp.arange(n_exp, dtype=jnp.int32)[None, :]).astype(jnp.int32)
    csum = jnp.cumsum(onehot, axis=0)
    rank = jnp.sum(onehot * csum, axis=1) - 1
    counts = csum[-1]
    padded = ((counts + rows - 1) // rows) * rows
    pend = jnp.cumsum(padded)
    dest = ((pend - padded)[e_flat] + rank).astype(jnp.int32)
    n_blk = (n * TOP_K + n_exp * (rows - 1) + rows - 1) // rows
    blk_e = jnp.minimum(jnp.searchsorted(pend, jnp.arange(n_blk, dtype=jnp.int32) * rows, side='right'),
                        n_exp - 1).astype(jnp.int32)
    nact = (pend[-1:] // rows).astype(jnp.int32)

    tm = 256
    dest2 = dest.reshape(n // tm, 1, TOP_K * tm)
    xs = _dispatch(dest2, x2, jnp.zeros((n_blk * rows, d), F32), tm=tm)
    ys = _experts(blk_e, nact, xs, norm_g.reshape(1, d), w_gate.astype(BF16), w_up.astype(BF16),
                  w_down.astype(BF16), rows=rows)
    fg = jnp.ones((1, d), F32) if final_g is None else final_g.reshape(1, d)
    return _combine(dest2, x2, gates, fg, ys, final_norm=final_g is not None, tm=tm)


def kernel(x, mix_norm_g, w_in, pool_w_grp, pool_scale, ml_conv_w, ml_b_i, ml_b_f, ml_norm_g, fox_b_f,
           w_br_pool, w_br_ml, w_br_fox, w_out, ffn_norm_g, ff_w_gate, ff_w_up, ff_w_down,
           moe_w_router, moe_b_router, moe_w_gate, moe_w_up, moe_w_down, final_norm_g):
    b, s, d = x.shape
    depth = mix_norm_g.shape[0]
    fused_final = False
    for l in range(depth):
        x = _token_mixing(x, mix_norm_g[l], w_in[l], pool_w_grp[l], pool_scale[l], ml_conv_w[l], ml_b_i[l],
                          ml_b_f[l], ml_norm_g[l], fox_b_f[l], w_br_pool[l], w_br_ml[l], w_br_fox[l], w_out[l])
        x2 = x.reshape(b * s, d)
        if l % 2 == 0:
            x2 = _swiglu(x2, ffn_norm_g[l].reshape(1, d), ff_w_gate[l // 2].astype(BF16),
                         ff_w_up[l // 2].astype(BF16), ff_w_down[l // 2].astype(BF16))
        else:
            fused_final = l == depth - 1
            x2 = _moe(x2, ffn_norm_g[l], moe_w_router[l // 2], moe_b_router[l // 2], moe_w_gate[l // 2],
                      moe_w_up[l // 2], moe_w_down[l // 2], final_norm_g if fused_final else None)
        x = x2.reshape(b, s, d)
    if not fused_final:
        x = _final_norm(x.reshape(b * s, d), final_norm_g.reshape(1, d)).reshape(b, s, d)
    return x
```

```python
import functools
import math

import jax
import jax.numpy as jnp
import numpy as np
from jax import lax
from jax.experimental import pallas as pl
from jax.experimental.pallas import tpu as pltpu

F32 = jnp.float32
BF16 = jnp.bfloat16

NORM_EPS = 1e-6
POOL_WINDOWS = (2, 4, 8, 16)
ML_HEADS = 4
FOX_HEADS = 8
TOP_K = 2
N_BRANCH = 3

LANES = 128
BF16_SUBLANES = 16
VMEM_LIMIT_BYTES = 56 * 1024 * 1024

NEG_BIG = -1e30


def _cparams(*sem):
    return pltpu.CompilerParams(dimension_semantics=sem, vmem_limit_bytes=VMEM_LIMIT_BYTES)


def _resident(shape):
    zeros = (0,) * len(shape)
    return pl.BlockSpec(shape, lambda *_: zeros, pipeline_mode=pl.Buffered(1))


def _rmsnorm(x, g):
    return x * lax.rsqrt(jnp.mean(x * x, axis=-1, keepdims=True) + NORM_EPS) * g


def _sigmoid(x):
    return 1.0 / (1.0 + jnp.exp(-x))


def _log_sigmoid(x):
    return jnp.minimum(x, 0.0) - jnp.log(1.0 + jnp.exp(-jnp.abs(x)))


def _split3(x):
    hi = x.astype(BF16)
    r = x - hi.astype(F32)
    mid = r.astype(BF16)
    lo = (r - mid.astype(F32)).astype(BF16)
    return hi, mid, lo


def _cumsum_rows(tril, x):
    return sum(jnp.dot(tril, part, preferred_element_type=F32) for part in _split3(x))


def _inproj_kernel(x_ref, g_ref, wm_ref, ws_ref, p_ref, s_ref, *, col_chunk):
    h = _rmsnorm(x_ref[...], g_ref[...]).astype(BF16)
    s_ref[...] = jnp.dot(h, ws_ref[...], preferred_element_type=F32)
    for c in range(wm_ref.shape[1] // col_chunk):
        sl = slice(c * col_chunk, (c + 1) * col_chunk)
        p_ref[:, sl] = jnp.dot(h, wm_ref[:, sl], preferred_element_type=F32).astype(BF16)


def _inproj(x2, g, wm, ws, *, tm=512, col_chunk=1024):
    n, d = x2.shape
    wcols = wm.shape[1]
    return pl.pallas_call(
        functools.partial(_inproj_kernel, col_chunk=col_chunk),
        grid=(n // tm,),
        in_specs=[pl.BlockSpec((tm, d), lambda i: (i, 0)),
                  _resident((1, d)), _resident((d, wcols)), _resident((d, LANES))],
        out_specs=[pl.BlockSpec((tm, wcols), lambda i: (i, 0)),
                   pl.BlockSpec((tm, LANES), lambda i: (i, 0))],
        out_shape=[jax.ShapeDtypeStruct((n, wcols), BF16), jax.ShapeDtypeStruct((n, LANES), F32)],
        compiler_params=_cparams("parallel"),
        name="inproj",
    )(x2, g, wm, ws)


AUG_TERMS = 3
AUG_STRIDE = 8


def _aug_placement(dh):
    pairs = FOX_HEADS * dh // LANES
    width = pairs * LANES
    pq = np.zeros((AUG_TERMS * LANES, width), np.float32)
    pk = np.zeros((AUG_TERMS * LANES, width), np.float32)
    cq = np.zeros((1, width), np.float32)
    ck = np.zeros((1, width), np.float32)
    for h in range(FOX_HEADS):
        base = (h // 2) * LANES + (h % 2) * AUG_STRIDE
        for t in range(AUG_TERMS):
            src = t * LANES + 2 * ML_HEADS + h
            pq[src, base + t] = 1.0
            pk[src, base + AUG_TERMS + t] = -1.0
            cq[0, base + AUG_TERMS + t] = 1.0
            ck[0, base + t] = 1.0
    return jnp.asarray(pq, BF16), jnp.asarray(pk, BF16), jnp.asarray(cq), jnp.asarray(ck)


def _gateprep_kernel(s_ref, b_ref, pq_ref, pk_ref, cq_ref, ck_ref, col_ref, row_ref, aq_ref, ak_ref, carry_ref):
    t = s_ref.shape[1]

    @pl.when(pl.program_id(1) == 0)
    def _():
        carry_ref[...] = jnp.zeros_like(carry_ref)

    pre = s_ref[0] + b_ref[...]
    lane = lax.broadcasted_iota(jnp.int32, pre.shape, 1)
    ls = _log_sigmoid(pre)
    forget = (lane >= ML_HEADS) & (lane < 2 * ML_HEADS + FOX_HEADS)
    tril = (lax.broadcasted_iota(jnp.int32, (t, t), 0) >= lax.broadcasted_iota(jnp.int32, (t, t), 1)).astype(BF16)
    fcum = _cumsum_rows(tril, jnp.where(forget, ls, 0.0)) + carry_ref[...]
    carry_ref[...] = fcum[t - 1:t, :]
    col = jnp.where(lane < ML_HEADS, pre, fcum)
    col_ref[0] = col
    row_ref[0] = col.T[0:row_ref.shape[1], :]
    parts = jnp.concatenate(_split3(fcum), axis=-1)
    aq_ref[0] = (jnp.dot(parts, pq_ref[...], preferred_element_type=F32) + cq_ref[...]).astype(BF16)
    ak_ref[0] = (jnp.dot(parts, pk_ref[...], preferred_element_type=F32) + ck_ref[...]).astype(BF16)


def _gateprep(s3, bias, dh, *, tg=512):
    b, s, _ = s3.shape
    n_rows = 2 * ML_HEADS + FOX_HEADS
    pq, pk, cq, ck = _aug_placement(dh)
    width = pq.shape[1]
    return pl.pallas_call(
        _gateprep_kernel,
        grid=(b, s // tg),
        in_specs=[pl.BlockSpec((1, tg, LANES), lambda i, j: (i, j, 0)), _resident((1, LANES)),
                  _resident(pq.shape), _resident(pk.shape), _resident(cq.shape), _resident(ck.shape)],
        out_specs=[pl.BlockSpec((1, tg, LANES), lambda i, j: (i, j, 0)),
                   pl.BlockSpec((1, n_rows, tg), lambda i, j: (i, 0, j)),
                   pl.BlockSpec((1, tg, width), lambda i, j: (i, j, 0)),
                   pl.BlockSpec((1, tg, width), lambda i, j: (i, j, 0))],
        out_shape=[jax.ShapeDtypeStruct((b, s, LANES), F32), jax.ShapeDtypeStruct((b, n_rows, s), F32),
                   jax.ShapeDtypeStruct((b, s, width), BF16), jax.ShapeDtypeStruct((b, s, width), BF16)],
        scratch_shapes=[pltpu.VMEM((1, LANES), F32)],
        compiler_params=_cparams("parallel", "arbitrary"),
        name="gateprep",
    )(s3, bias, pq, pk, cq, ck)


def _fox_kernel(q_ref, aq_ref, k_ref, ak_ref, v_ref, o_ref, m_ref, acc_ref, *, tq, tk, dh, scale):
    qi = pl.program_id(2)
    lane = lax.broadcasted_iota(jnp.int32, (tq, LANES), 1)
    q2 = q_ref[0] * scale
    aq = aq_ref[0]
    in_head = [(lane >= a * dh) & (lane < (a + 1) * dh) for a in range(2)]
    q_ops = []
    for a in range(2):
        in_aug = (lane >= a * AUG_STRIDE) & (lane < a * AUG_STRIDE + 2 * AUG_TERMS)
        q_ops.append(jnp.concatenate([jnp.where(in_head[a], q2, jnp.zeros_like(q2)),
                                      jnp.where(in_aug, aq, jnp.zeros_like(aq))], axis=-1))
    m_ref[...] = jnp.full_like(m_ref, NEG_BIG)
    acc_ref[...] = jnp.zeros_like(acc_ref)
    causal = (lax.broadcasted_iota(jnp.int32, (tq, tq), 1) <= lax.broadcasted_iota(jnp.int32, (tq, tq), 0))

    def step(start, width, masked):
        kb = jnp.concatenate([k_ref[0, pl.ds(start, width), :], ak_ref[0, pl.ds(start, width), :]], axis=-1)
        vb = v_ref[0, pl.ds(start, width), :]
        key_lane = lax.broadcasted_iota(jnp.int32, (width, LANES), 1)
        key_head = [(key_lane >= a * dh) & (key_lane < (a + 1) * dh) for a in range(2)]
        for a in range(2):
            s = lax.dot_general(q_ops[a], kb, (((1,), (1,)), ((), ())), preferred_element_type=F32)
            if masked:
                s = jnp.where(causal, s, NEG_BIG)
            m_prev = m_ref[a]
            m_new = jnp.maximum(m_prev, jnp.max(s, axis=-1, keepdims=True))
            alpha = jnp.exp(m_prev - m_new)
            pexp = jnp.exp(s - jnp.concatenate([m_new] * (width // LANES), axis=-1))
            v_op = jnp.where(key_head[a], vb, jnp.ones_like(vb))
            acc_ref[a] = alpha * acc_ref[a] + jnp.dot(pexp.astype(BF16), v_op, preferred_element_type=F32)
            m_ref[a] = m_new

    n_wide = (qi * tq) // tk
    n_narrow = qi - n_wide * (tk // tq)

    def wide_body(j, carry):
        step(pl.multiple_of(j * tk, tk), tk, False)
        return carry

    def narrow_body(j, carry):
        step(pl.multiple_of((n_wide * (tk // tq) + j) * tq, tq), tq, False)
        return carry

    lax.fori_loop(0, n_wide, wide_body, 0)
    if tk != tq:
        lax.fori_loop(0, n_narrow, narrow_body, 0)
    step(pl.multiple_of(qi * tq, tq), tq, True)
    outs = [acc_ref[a] / pltpu.roll(acc_ref[a], LANES // 2, 1) for a in range(2)]
    o_ref[0] = jnp.where(in_head[0], outs[0], outs[1]).astype(BF16)


def _fox(p3, aq, ak, *, q_col, k_col, v_col, dh, tq=1024, tk=1024):
    b, s, _ = p3.shape
    assert 2 * dh == LANES, "two heads share one 128-lane block"
    assert tk % tq == 0 and s % tq == 0
    pairs = FOX_HEADS * dh // LANES
    scale = dh ** -0.5
    assert math.frexp(scale)[0] == 0.5, "score scale is folded into bf16 q; exact only for a power of two"
    qb, kb, vb = q_col // LANES, k_col // LANES, v_col // LANES
    return pl.pallas_call(
        functools.partial(_fox_kernel, tq=tq, tk=tk, dh=dh, scale=scale),
        grid=(b, pairs, s // tq),
        in_specs=[pl.BlockSpec((1, tq, LANES), lambda i, p, q: (i, q, qb + p)),
                  pl.BlockSpec((1, tq, LANES), lambda i, p, q: (i, q, p)),
                  pl.BlockSpec((1, s, LANES), lambda i, p, q: (i, 0, kb + p)),
                  pl.BlockSpec((1, s, LANES), lambda i, p, q: (i, 0, p)),
                  pl.BlockSpec((1, s, LANES), lambda i, p, q: (i, 0, vb + p))],
        out_specs=pl.BlockSpec((1, tq, LANES), lambda i, p, q: (i, q, p)),
        out_shape=jax.ShapeDtypeStruct((b, s, pairs * LANES), BF16),
        scratch_shapes=[pltpu.VMEM((2, tq, LANES), F32), pltpu.VMEM((2, tq, LANES), F32)],
        compiler_params=_cparams("parallel", "parallel", "arbitrary"),
        name="fox_attention",
    )(p3, aq, p3, ak, p3)


def _mlstm_kernel(qk_ref, halo_ref, v_ref, og_ref, gc_ref, gr_ref, cw_ref, ng_ref, out_ref,
                  ubuf, ct_ref, n_ref, m_ref, g0_ref, *, chunk, dh):
    c = pl.program_id(1)
    width = ML_HEADS * dh
    taps = cw_ref.shape[0]
    pad = halo_ref.shape[1]
    assert 3 * ML_HEADS <= BF16_SUBLANES

    @pl.when(c == 0)
    def _():
        ct_ref[...] = jnp.zeros_like(ct_ref)
        n_ref[...] = jnp.zeros_like(n_ref)
        m_ref[...] = jnp.zeros_like(m_ref)
        g0_ref[...] = jnp.zeros_like(g0_ref)

    ubuf[0:pad, :] = jnp.where(c == 0, 0.0, halo_ref[0].astype(F32))
    ubuf[pad:pad + chunk, :] = qk_ref[0].astype(F32)

    def conv_silu(col0):
        y = sum(ubuf[pad - taps + 1 + j:pad - taps + 1 + j + chunk, col0:col0 + dh] * cw_ref[j:j + 1, col0:col0 + dh]
                for j in range(taps))
        return y * _sigmoid(y)

    ri = lax.broadcasted_iota(jnp.int32, (chunk, chunk), 0)
    ci = lax.broadcasted_iota(jnp.int32, (chunk, chunk), 1)
    visible = ri <= ci
    gc = gc_ref[0]
    gr = gr_ref[0]
    g0 = g0_ref[...]
    g0_ref[...] = gc[chunk - 1:chunk, :]
    contract_last = (((1,), (1,)), ((), ()))
    contract_first = (((0,), (0,)), ((), ()))
    part_row = lax.broadcasted_iota(jnp.int32, (BF16_SUBLANES, dh), 0)
    slab_row = lax.broadcasted_iota(jnp.int32, (BF16_SUBLANES, chunk), 0)
    ones_dh = jnp.ones((dh, dh), BF16)

    slab = jnp.zeros((BF16_SUBLANES, chunk), F32)
    heads = []
    for h in range(ML_HEADS):
        hs = slice(h * dh, (h + 1) * dh)
        q = conv_silu(h * dh)
        k = conv_silu(width + h * dh) * dh ** -0.5
        qb, kb = q.astype(BF16), k.astype(BF16)
        vb = v_ref[0, :, hs]
        g0_h = g0[:, ML_HEADS + h:ML_HEADS + h + 1]
        key_col = gc[:, h:h + 1] - (gc[:, ML_HEADS + h:ML_HEADS + h + 1] - g0_h)
        li_r = gr[h:h + 1, :]
        b_r = gr[ML_HEADS + h:ML_HEADS + h + 1, :] - g0_h
        b_last = b_r[:, chunk - 1:chunk]
        m_prev = m_ref[h:h + 1, 0:1]

        d_t = jnp.where(visible, b_r + key_col, NEG_BIG)
        m_inter = b_r + m_prev
        m_t = jnp.maximum(m_inter, jnp.max(d_t, axis=0, keepdims=True))
        w_inter = jnp.exp(m_inter - m_t)
        p_t = jnp.exp(d_t - m_t) * lax.dot_general(kb, qb, contract_last, preferred_element_type=F32)
        n_hi, n_mid, n_lo = (part.astype(F32) for part in _split3(n_ref[h]))
        n_parts = jnp.where(part_row == 0, n_hi, jnp.where(part_row == 1, n_mid,
                                                           jnp.where(part_row == 2, n_lo, 0.0))).astype(BF16)
        nq = jnp.sum(lax.dot_general(n_parts, qb, contract_last, preferred_element_type=F32), axis=0, keepdims=True)
        den = w_inter * nq + jnp.sum(p_t, axis=0, keepdims=True)
        inv = 1.0 / jnp.maximum(jnp.abs(den), jnp.exp(-m_t))

        g_r = b_last - b_r + li_r
        m_new = jnp.maximum(b_last + m_prev, jnp.max(g_r, axis=-1, keepdims=True))
        decay = jnp.exp(b_last + m_prev - m_new)
        wk_r = jnp.exp(g_r - m_new)
        for slot, vec in enumerate((w_inter, inv, wk_r)):
            slab = jnp.where(slab_row == 3 * h + slot, vec, slab)
        heads.append((hs, k, qb, vb, p_t.astype(BF16), decay, m_new))

    cols = jnp.concatenate([slab, jnp.zeros((LANES - BF16_SUBLANES, chunk), F32)], axis=0).T

    for h, (hs, k, qb, vb, p_tb, decay, m_new) in enumerate(heads):
        w_col, inv_col, wk_col = (cols[:, 3 * h + slot:3 * h + slot + 1] for slot in range(3))
        num = (w_col * jnp.dot(qb, ct_ref[h].astype(BF16), preferred_element_type=F32)
               + lax.dot_general(p_tb, vb, contract_first, preferred_element_type=F32))
        hh = num * inv_col
        kw = k * wk_col
        ct_ref[h] = decay * ct_ref[h] + lax.dot_general(kw.astype(BF16), vb, contract_first,
                                                       preferred_element_type=F32)
        n_ref[h] = decay * n_ref[h] + jnp.sum(kw, axis=0, keepdims=True)
        m_ref[h:h + 1, :] = jnp.broadcast_to(m_new, (1, LANES))

        sq = hh * hh
        sq_hi = sq.astype(BF16)
        sq_lo = (sq - sq_hi.astype(F32)).astype(BF16)
        mean_sq = (jnp.dot(sq_hi, ones_dh, preferred_element_type=F32)
                   + jnp.dot(sq_lo, ones_dh, preferred_element_type=F32)) * (1.0 / dh)
        hn = hh * lax.rsqrt(mean_sq + NORM_EPS) * ng_ref[:, hs]
        out_ref[0, :, hs] = (hn * _sigmoid(og_ref[0, :, hs].astype(F32))).astype(BF16)


def _mlstm(p3, gcol, grow, conv_w, norm_g, *, qk_col, v_col, o_col, dh, chunk=256):
    b, s, _ = p3.shape
    width = ML_HEADS * dh
    pad = BF16_SUBLANES
    assert conv_w.shape[0] - 1 <= pad
    qkb, vb, ob = qk_col // (2 * width), v_col // width, o_col // width
    halo_per_chunk = chunk // pad
    return pl.pallas_call(
        functools.partial(_mlstm_kernel, chunk=chunk, dh=dh),
        grid=(b, s // chunk),
        in_specs=[pl.BlockSpec((1, chunk, 2 * width), lambda i, c: (i, c, qkb)),
                  pl.BlockSpec((1, pad, 2 * width),
                               lambda i, c: (i, jnp.maximum(c * halo_per_chunk - 1, 0), qkb)),
                  pl.BlockSpec((1, chunk, width), lambda i, c: (i, c, vb)),
                  pl.BlockSpec((1, chunk, width), lambda i, c: (i, c, ob)),
                  pl.BlockSpec((1, chunk, LANES), lambda i, c: (i, c, 0)),
                  pl.BlockSpec((1, grow.shape[1], chunk), lambda i, c: (i, 0, c)),
                  _resident(conv_w.shape), _resident((1, width))],
        out_specs=pl.BlockSpec((1, chunk, width), lambda i, c: (i, c, 0)),
        out_shape=jax.ShapeDtypeStruct((b, s, width), BF16),
        scratch_shapes=[pltpu.VMEM((pad + chunk, 2 * width), F32),
                        pltpu.VMEM((ML_HEADS, dh, dh), F32),
                        pltpu.VMEM((ML_HEADS, 1, dh), F32),
                        pltpu.VMEM((ML_HEADS, LANES), F32),
                        pltpu.VMEM((1, LANES), F32)],
        compiler_params=_cparams("parallel", "arbitrary"),
        name="mlstm",
    )(p3, p3, p3, p3, gcol, grow, conv_w, norm_g)


def _merge_kernel(u_ref, halo_ref, hm_ref, fo_ref, gp_ref, x_ref, wg_ref, ps_ref, wbp_ref, wbm_ref, wbf_ref,
                  wo_ref, o_ref, ubuf, *, tm):
    j = pl.program_id(1)
    pad = halo_ref.shape[1]
    gw = wg_ref.shape[1]
    d = x_ref.shape[2]
    ubuf[0:pad, :] = jnp.where(j == 0, 0.0, halo_ref[0].astype(F32))
    ubuf[pad:pad + tm, :] = u_ref[0].astype(F32)
    pos = j * tm + lax.broadcasted_iota(jnp.int32, (tm, 1), 0)
    ys = []
    for g, w in enumerate(POOL_WINDOWS):
        gs = slice(g * gw, (g + 1) * gw)
        wsum = sum(ubuf[pad - k:pad - k + tm, gs] for k in range(w))
        cnt = jnp.minimum(pos + 1, w).astype(F32)
        dlt = wsum / cnt - ubuf[pad:pad + tm, gs]
        ys.append(jnp.dot(dlt.astype(BF16), wg_ref[g], preferred_element_type=F32))
    y_pool = (jnp.concatenate(ys, axis=-1) * ps_ref[...]).astype(BF16)
    merged = _sigmoid(gp_ref[0, :, 0:d].astype(F32)) * jnp.dot(y_pool, wbp_ref[...], preferred_element_type=F32)
    merged += _sigmoid(gp_ref[0, :, d:2 * d].astype(F32)) * jnp.dot(hm_ref[0], wbm_ref[...],
                                                                   preferred_element_type=F32)
    merged += _sigmoid(gp_ref[0, :, 2 * d:3 * d].astype(F32)) * jnp.dot(fo_ref[0], wbf_ref[...],
                                                                       preferred_element_type=F32)
    o_ref[0] = x_ref[0] + jnp.dot(merged.astype(BF16), wo_ref[...], preferred_element_type=F32)


def _merge(p3, hm, fo, x3, wgrp, pscale, wbp, wbm, wbf, wo, *, pool_col, gate_col, tm=512):
    b, s, d = x3.shape
    pw = wbp.shape[0]
    pad = BF16_SUBLANES
    assert max(POOL_WINDOWS) - 1 <= pad and gate_col == 0
    pcb = pool_col // pw
    halo_per_blk = tm // pad
    return pl.pallas_call(
        functools.partial(_merge_kernel, tm=tm),
        grid=(b, s // tm),
        in_specs=[pl.BlockSpec((1, tm, pw), lambda i, j: (i, j, pcb)),
                  pl.BlockSpec((1, pad, pw), lambda i, j: (i, jnp.maximum(j * halo_per_blk - 1, 0), pcb)),
                  pl.BlockSpec((1, tm, hm.shape[2]), lambda i, j: (i, j, 0)),
                  pl.BlockSpec((1, tm, fo.shape[2]), lambda i, j: (i, j, 0)),
                  pl.BlockSpec((1, tm, N_BRANCH * d), lambda i, j: (i, j, 0)),
                  pl.BlockSpec((1, tm, d), lambda i, j: (i, j, 0)),
                  _resident(wgrp.shape), _resident(pscale.shape), _resident(wbp.shape),
                  _resident(wbm.shape), _resident(wbf.shape), _resident(wo.shape)],
        out_specs=pl.BlockSpec((1, tm, d), lambda i, j: (i, j, 0)),
        out_shape=jax.ShapeDtypeStruct((b, s, d), F32),
        scratch_shapes=[pltpu.VMEM((pad + tm, pw), F32)],
        compiler_params=_cparams("parallel", "parallel"),
        name="merge",
    )(p3, p3, hm, fo, p3, x3, wgrp, pscale, wbp, wbm, wbf, wo)


def _swiglu_kernel(x_ref, g_ref, wg_ref, wu_ref, wd_ref, o_ref, a_ref, *, ff_chunk):
    x = x_ref[...]
    h = _rmsnorm(x, g_ref[...]).astype(BF16)
    for c in range(wg_ref.shape[1] // ff_chunk):
        sl = slice(c * ff_chunk, (c + 1) * ff_chunk)
        gate = jnp.dot(h, wg_ref[:, sl], preferred_element_type=F32)
        up = jnp.dot(h, wu_ref[:, sl], preferred_element_type=F32)
        a_ref[:, sl] = (gate * _sigmoid(gate) * up).astype(BF16)
    o_ref[...] = x + jnp.dot(a_ref[...], wd_ref[...], preferred_element_type=F32)


def _swiglu(x2, g, wg, wu, wd, *, tm=512, ff_chunk=256):
    n, d = x2.shape
    ff = wg.shape[1]
    assert ff % ff_chunk == 0
    return pl.pallas_call(
        functools.partial(_swiglu_kernel, ff_chunk=ff_chunk),
        grid=(n // tm,),
        in_specs=[pl.BlockSpec((tm, d), lambda i: (i, 0)), _resident((1, d)),
                  _resident(wg.shape), _resident(wu.shape), _resident(wd.shape)],
        out_specs=pl.BlockSpec((tm, d), lambda i: (i, 0)),
        out_shape=jax.ShapeDtypeStruct((n, d), F32),
        scratch_shapes=[pltpu.VMEM((tm, ff), BF16)],
        compiler_params=_cparams("parallel"),
        name="dense_swiglu",
    )(x2, g, wg, wu, wd)


def _router_kernel(x_ref, g_ref, wr_ref, br_ref, e_ref, gt_ref):
    h = _rmsnorm(x_ref[...], g_ref[...])
    h_hi = h.astype(BF16)
    h_lo = (h - h_hi.astype(F32)).astype(BF16)
    logits = jnp.dot(jnp.concatenate([h_hi, h_hi, h_lo], axis=-1), wr_ref[...],
                     preferred_element_type=F32) + br_ref[...]
    lane = lax.broadcasted_iota(jnp.int32, logits.shape, 1)
    m1 = jnp.max(logits, axis=-1, keepdims=True)
    i1 = jnp.min(jnp.where(logits == m1, lane, LANES), axis=-1, keepdims=True)
    rest = jnp.where(lane == i1, NEG_BIG, logits)
    m2 = jnp.max(rest, axis=-1, keepdims=True)
    i2 = jnp.min(jnp.where(rest == m2, lane, LANES), axis=-1, keepdims=True)
    e2 = jnp.exp(m2 - m1)
    g1 = 1.0 / (1.0 + e2)
    e_ref[...] = jnp.where(lane == 0, i1, jnp.where(lane == 1, i2, 0))
    gt_ref[...] = jnp.where(lane == 0, g1, jnp.where(lane == 1, e2 * g1, 0.0))


def _router(x2, g, wr, br, *, tm=512):
    n, d = x2.shape
    return pl.pallas_call(
        _router_kernel,
        grid=(n // tm,),
        in_specs=[pl.BlockSpec((tm, d), lambda i: (i, 0)), _resident((1, d)),
                  _resident(wr.shape), _resident((1, LANES))],
        out_specs=[pl.BlockSpec((tm, LANES), lambda i: (i, 0)), pl.BlockSpec((tm, LANES), lambda i: (i, 0))],
        out_shape=[jax.ShapeDtypeStruct((n, LANES), jnp.int32), jax.ShapeDtypeStruct((n, LANES), F32)],
        compiler_params=_cparams("parallel"),
        name="router",
    )(x2, g, wr, br)


def _dispatch_kernel(pend_ref, dest_ref, x_ref, xs_ref, zero_ref, sem, zsem):
    tm = x_ref.shape[0]
    rows = zero_ref.shape[0]

    @pl.when(pl.program_id(0) == 0)
    def _():
        zero_ref[...] = jnp.zeros_like(zero_ref)

        def zero_copy(e):
            start = pl.multiple_of(pend_ref[e] - rows, rows)
            return pltpu.make_async_copy(zero_ref, xs_ref.at[pl.ds(start, rows)], zsem)

        def has_rows(e):
            return pend_ref[e] > (pend_ref[e - 1] if e else 0)

        n_exp = pend_ref.shape[0]
        n_rows = xs_ref.shape[0]
        tail_starts = [n_rows - (j + 1) * rows for j in range(min(n_exp, n_rows // rows))]

        def tail_copy(start):
            return pltpu.make_async_copy(zero_ref, xs_ref.at[pl.ds(start, rows)], zsem)

        for e in range(n_exp):
            @pl.when(has_rows(e))
            def _():
                zero_copy(e).start()
        for start in tail_starts:
            @pl.when(start >= pend_ref[n_exp - 1])
            def _():
                tail_copy(start).start()
        for e in range(n_exp):
            @pl.when(has_rows(e))
            def _():
                zero_copy(e).wait()
        for start in tail_starts:
            @pl.when(start >= pend_ref[n_exp - 1])
            def _():
                tail_copy(start).wait()

    def row_copy(t, k):
        return pltpu.make_async_copy(x_ref.at[pl.ds(t, 1)], xs_ref.at[pl.ds(dest_ref[0, 0, TOP_K * t + k], 1)], sem)

    def start(t, carry):
        for k in range(TOP_K):
            row_copy(t, k).start()
        return carry

    def wait(t, carry):
        for k in range(TOP_K):
            row_copy(t, k).wait()
        return carry

    lax.fori_loop(0, tm, start, 0, unroll=8)
    lax.fori_loop(0, tm, wait, 0, unroll=8)


def _dispatch(pend, dest2, x2, *, n_rows, rows, tm=256):
    n, d = x2.shape
    grid_spec = pltpu.PrefetchScalarGridSpec(
        num_scalar_prefetch=1,
        grid=(n // tm,),
        in_specs=[pl.BlockSpec((1, 1, TOP_K * tm), lambda i, pe: (i, 0, 0), memory_space=pltpu.SMEM),
                  pl.BlockSpec((tm, d), lambda i, pe: (i, 0))],
        out_specs=pl.BlockSpec(memory_space=pl.ANY),
        scratch_shapes=[pltpu.VMEM((rows, d), F32), pltpu.SemaphoreType.DMA(()), pltpu.SemaphoreType.DMA(())],
    )
    return pl.pallas_call(
        _dispatch_kernel,
        grid_spec=grid_spec,
        out_shape=jax.ShapeDtypeStruct((n_rows, d), F32),
        compiler_params=_cparams("arbitrary"),
        name="moe_dispatch",
    )(pend, dest2, x2)


def _experts_kernel(blk_e_ref, nact_ref, xs_ref, g_ref, wg_ref, wu_ref, wd_ref, ys_ref, h_ref, a_ref, acc_ref,
                    *, ff_chunk):
    del blk_e_ref
    i = pl.program_id(0)
    f = pl.program_id(1)

    @pl.when(i < nact_ref[0])
    def _():
        @pl.when(f == 0)
        def _():
            h_ref[...] = _rmsnorm(xs_ref[...], g_ref[...]).astype(BF16)
            acc_ref[...] = jnp.zeros_like(acc_ref)

        h = h_ref[...]
        for c in range(wg_ref.shape[2] // ff_chunk):
            sl = slice(c * ff_chunk, (c + 1) * ff_chunk)
            gate = jnp.dot(h, wg_ref[0, :, sl], preferred_element_type=F32)
            up = jnp.dot(h, wu_ref[0, :, sl], preferred_element_type=F32)
            a_ref[:, sl] = (gate * _sigmoid(gate) * up).astype(BF16)
        acc_ref[...] += jnp.dot(a_ref[...], wd_ref[0], preferred_element_type=F32)

        @pl.when(f == pl.num_programs(1) - 1)
        def _():
            ys_ref[...] = acc_ref[...]

    @pl.when((i >= nact_ref[0]) & (f == 0))
    def _():
        ys_ref[...] = jnp.zeros_like(ys_ref)


def _experts(blk_e, nact, xs, g, wg, wu, wd, *, rows, tf=1792, ff_chunk=256):
    n_rows, d = xs.shape
    ff = wg.shape[2]
    nf = ff // tf
    assert ff % tf == 0 and tf % ff_chunk == 0 and n_rows % rows == 0

    def blk(i, nact):
        return jnp.minimum(i, nact[0] - 1)

    def ff_tile(i, f, nact):
        return jnp.where(i < nact[0], f, nf - 1)

    grid_spec = pltpu.PrefetchScalarGridSpec(
        num_scalar_prefetch=2,
        grid=(n_rows // rows, nf),
        in_specs=[pl.BlockSpec((rows, d), lambda i, f, be, na: (blk(i, na), 0)),
                  pl.BlockSpec((1, d), lambda i, f, be, na: (0, 0)),
                  pl.BlockSpec((1, d, tf), lambda i, f, be, na: (be[blk(i, na)], 0, ff_tile(i, f, na))),
                  pl.BlockSpec((1, d, tf), lambda i, f, be, na: (be[blk(i, na)], 0, ff_tile(i, f, na))),
                  pl.BlockSpec((1, tf, d), lambda i, f, be, na: (be[blk(i, na)], ff_tile(i, f, na), 0))],
        out_specs=pl.BlockSpec((rows, d), lambda i, f, be, na: (i, 0)),
        scratch_shapes=[pltpu.VMEM((rows, d), BF16), pltpu.VMEM((rows, tf), BF16), pltpu.VMEM((rows, d), F32)],
    )
    return pl.pallas_call(
        functools.partial(_experts_kernel, ff_chunk=ff_chunk),
        grid_spec=grid_spec,
        out_shape=jax.ShapeDtypeStruct((n_rows, d), F32),
        compiler_params=_cparams("arbitrary", "arbitrary"),
        name="moe_experts",
    )(blk_e, nact, xs, g, wg, wu, wd)


def _combine_kernel(dest_ref, x_ref, gt_ref, fg_ref, ys_ref, o_ref, buf, sem, *, final_norm):
    tm = x_ref.shape[0]

    def row_copy(t, k):
        return pltpu.make_async_copy(ys_ref.at[pl.ds(dest_ref[0, 0, TOP_K * t + k], 1)], buf.at[k, pl.ds(t, 1)], sem)

    def start(t, carry):
        for k in range(TOP_K):
            row_copy(t, k).start()
        return carry

    def wait(t, carry):
        for k in range(TOP_K):
            row_copy(t, k).wait()
        return carry

    lax.fori_loop(0, tm, start, 0, unroll=8)
    lax.fori_loop(0, tm, wait, 0, unroll=8)
    gt = gt_ref[...]
    y = x_ref[...]
    for k in range(TOP_K):
        y = y + buf[k] * gt[:, k:k + 1]
    o_ref[...] = _rmsnorm(y, fg_ref[...]) if final_norm else y


def _combine(dest2, x2, gates, fg, ys, *, final_norm, tm=256):
    n, d = x2.shape
    return pl.pallas_call(
        functools.partial(_combine_kernel, final_norm=final_norm),
        grid=(n // tm,),
        in_specs=[pl.BlockSpec((1, 1, TOP_K * tm), lambda i: (i, 0, 0), memory_space=pltpu.SMEM),
                  pl.BlockSpec((tm, d), lambda i: (i, 0)),
                  pl.BlockSpec((tm, LANES), lambda i: (i, 0)),
                  _resident((1, d)),
                  pl.BlockSpec(memory_space=pl.ANY)],
        out_specs=pl.BlockSpec((tm, d), lambda i: (i, 0)),
        out_shape=jax.ShapeDtypeStruct((n, d), F32),
        scratch_shapes=[pltpu.VMEM((TOP_K, tm, d), F32), pltpu.SemaphoreType.DMA(())],
        compiler_params=_cparams("arbitrary"),
        name="moe_combine",
    )(dest2, x2, gates, fg, ys)


def _final_norm_kernel(x_ref, g_ref, o_ref):
    o_ref[...] = _rmsnorm(x_ref[...], g_ref[...])


def _final_norm(x2, g, *, tm=512):
    n, d = x2.shape
    return pl.pallas_call(
        _final_norm_kernel,
        grid=(n // tm,),
        in_specs=[pl.BlockSpec((tm, d), lambda i: (i, 0)), _resident((1, d))],
        out_specs=pl.BlockSpec((tm, d), lambda i: (i, 0)),
        out_shape=jax.ShapeDtypeStruct((n, d), F32),
        compiler_params=_cparams("parallel"),
        name="final_norm",
    )(x2, g)


def _token_mixing(x3, norm_g, w_in, pool_w_grp, pool_scale, conv_w, b_i, b_f, ml_norm_g, fox_b_f,
                  w_br_pool, w_br_ml, w_br_fox, w_out):
    b, s, d = x3.shape
    pool_w = w_br_pool.shape[0]
    ml_w = w_br_ml.shape[0]
    fox_w = w_br_fox.shape[0]
    ml_dh = ml_w // ML_HEADS
    fox_dh = fox_w // FOX_HEADS
    n_small = 2 * ML_HEADS + FOX_HEADS

    sizes = (pool_w, ml_w, ml_w, ml_w, ml_w, ML_HEADS, ML_HEADS, fox_w, fox_w, fox_w, FOX_HEADS, N_BRANCH * d)
    offs = [0]
    for sz in sizes:
        offs.append(offs[-1] + sz)
    assert offs[-1] == w_in.shape[1]
    (o_pool, o_q, o_k, o_v, o_o, o_i, o_f, o_fq, o_fk, o_fv, o_ff, o_g) = offs[:-1]

    def cols(o, sz):
        return w_in[:, o:o + sz]

    wm = jnp.concatenate([cols(o_g, N_BRANCH * d), cols(o_q, ml_w), cols(o_k, ml_w), cols(o_v, ml_w),
                          cols(o_o, ml_w), cols(o_pool, pool_w), cols(o_fq, fox_w), cols(o_fk, fox_w),
                          cols(o_fv, fox_w)], axis=1).astype(BF16)
    c_gate = 0
    c_qk = c_gate + N_BRANCH * d
    c_v = c_qk + 2 * ml_w
    c_o = c_v + ml_w
    c_pool = c_o + ml_w
    c_fq = c_pool + pool_w
    c_fk = c_fq + fox_w
    c_fv = c_fk + fox_w
    ws = jnp.concatenate([cols(o_i, ML_HEADS), cols(o_f, ML_HEADS), cols(o_ff, FOX_HEADS),
                          jnp.zeros((d, LANES - n_small), w_in.dtype)], axis=1).astype(BF16)
    bias = jnp.concatenate([b_i, b_f, fox_b_f, jnp.zeros((LANES - n_small,), F32)]).reshape(1, LANES)

    proj, small = _inproj(x3.reshape(b * s, d), norm_g.reshape(1, d), wm, ws)
    p3 = proj.reshape(b, s, -1)
    gcol, grow, aq, ak = _gateprep(small.reshape(b, s, LANES), bias, fox_dh)
    fo = _fox(p3, aq, ak, q_col=c_fq, k_col=c_fk, v_col=c_fv, dh=fox_dh)
    hm = _mlstm(p3, gcol, grow, conv_w, ml_norm_g.reshape(1, ml_w), qk_col=c_qk, v_col=c_v, o_col=c_o, dh=ml_dh)
    return _merge(p3, hm, fo, x3, pool_w_grp.astype(BF16), pool_scale.reshape(1, pool_w),
                  w_br_pool.astype(BF16), w_br_ml.astype(BF16), w_br_fox.astype(BF16), w_out.astype(BF16),
                  pool_col=c_pool, gate_col=c_gate)


def _moe(x2, norm_g, w_router, b_router, w_gate, w_up, w_down, final_g, *, rows=512):
    n, d = x2.shape
    n_exp = w_router.shape[1]
    wr = jnp.concatenate([w_router.astype(F32), jnp.zeros((d, LANES - n_exp), F32)], axis=1)
    wr_hi = wr.astype(BF16)
    wr_lo = (wr - wr_hi.astype(F32)).astype(BF16)
    br = jnp.concatenate([b_router.astype(F32), jnp.full((LANES - n_exp,), NEG_BIG, F32)]).reshape(1, LANES)
    top_e, gates = _router(x2, norm_g.reshape(1, d), jnp.concatenate([wr_hi, wr_lo, wr_hi], axis=0), br)

    e_flat = top_e[:, :TOP_K].reshape(n * TOP_K)
    onehot = (e_flat[:, None] == jnp.arange(n_exp, dtype=jnp.int32)[None, :]).astype(jnp.int32)
    csum = jnp.cumsum(onehot, axis=0)
    rank = jnp.sum(onehot * csum, axis=1) - 1
    counts = csum[-1]
    padded = ((counts + rows - 1) // rows) * rows
    pend = jnp.cumsum(padded)
    dest = ((pend - padded)[e_flat] + rank).astype(jnp.int32)
    n_blk = (n * TOP_K + n_exp * (rows - 1) + rows - 1) // rows
    blk_e = jnp.minimum(jnp.searchsorted(pend, jnp.arange(n_blk, dtype=jnp.int32) * rows, side='right'),
                        n_exp - 1).astype(jnp.int32)
    nact = (pend[-1:] // rows).astype(jnp.int32)

    tm = 256
    dest2 = dest.reshape(n // tm, 1, TOP_K * tm)
    xs = _dispatch(pend.astype(jnp.int32), dest2, x2, n_rows=n_blk * rows, rows=rows, tm=tm)
    ys = _experts(blk_e, nact, xs, norm_g.reshape(1, d), w_gate.astype(BF16), w_up.astype(BF16),
                  w_down.astype(BF16), rows=rows)
    fg = jnp.ones((1, d), F32) if final_g is None else final_g.reshape(1, d)
    return _combine(dest2, x2, gates, fg, ys, final_norm=final_g is not None, tm=tm)


def kernel(x, mix_norm_g, w_in, pool_w_grp, pool_scale, ml_conv_w, ml_b_i, ml_b_f, ml_norm_g, fox_b_f,
           w_br_pool, w_br_ml, w_br_fox, w_out, ffn_norm_g, ff_w_gate, ff_w_up, ff_w_down,
           moe_w_router, moe_b_router, moe_w_gate, moe_w_up, moe_w_down, final_norm_g):
    b, s, d = x.shape
    depth = mix_norm_g.shape[0]
    fused_final = False
    for l in range(depth):
        x = _token_mixing(x, mix_norm_g[l], w_in[l], pool_w_grp[l], pool_scale[l], ml_conv_w[l], ml_b_i[l],
                          ml_b_f[l], ml_norm_g[l], fox_b_f[l], w_br_pool[l], w_br_ml[l], w_br_fox[l], w_out[l])
        x2 = x.reshape(b * s, d)
        if l % 2 == 0:
            x2 = _swiglu(x2, ffn_norm_g[l].reshape(1, d), ff_w_gate[l // 2].astype(BF16),
                         ff_w_up[l // 2].astype(BF16), ff_w_down[l // 2].astype(BF16))
        else:
            fused_final = l == depth - 1
            x2 = _moe(x2, ffn_norm_g[l], moe_w_router[l // 2], moe_b_router[l // 2], moe_w_gate[l // 2],
                      moe_w_up[l // 2], moe_w_down[l // 2], final_norm_g if fused_final else None)
        x = x2.reshape(b, s, d)
    if not fused_final:
        x = _final_norm(x.reshape(b * s, d), final_norm_g.reshape(1, d)).reshape(b, s, d)
    return x
```

```python
import functools
import math

import jax
import jax.numpy as jnp
import numpy as np
from jax import lax
from jax.experimental import pallas as pl
from jax.experimental.pallas import tpu as pltpu

F32 = jnp.float32
BF16 = jnp.bfloat16

NORM_EPS = 1e-6
POOL_WINDOWS = (2, 4, 8, 16)
ML_HEADS = 4
FOX_HEADS = 8
TOP_K = 2
N_BRANCH = 3

LANES = 128
BF16_SUBLANES = 16
VMEM_LIMIT_BYTES = 56 * 1024 * 1024

NEG_BIG = -1e30


def _cparams(*sem):
    return pltpu.CompilerParams(dimension_semantics=sem, vmem_limit_bytes=VMEM_LIMIT_BYTES)


def _resident(shape):
    zeros = (0,) * len(shape)
    return pl.BlockSpec(shape, lambda *_: zeros, pipeline_mode=pl.Buffered(1))


def _rmsnorm(x, g):
    return x * lax.rsqrt(jnp.mean(x * x, axis=-1, keepdims=True) + NORM_EPS) * g


def _sigmoid(x):
    return 1.0 / (1.0 + jnp.exp(-x))


def _log_sigmoid(x):
    return jnp.minimum(x, 0.0) - jnp.log(1.0 + jnp.exp(-jnp.abs(x)))


def _split3(x):
    hi = x.astype(BF16)
    r = x - hi.astype(F32)
    mid = r.astype(BF16)
    lo = (r - mid.astype(F32)).astype(BF16)
    return hi, mid, lo


def _cumsum_rows(tril, x):
    return sum(jnp.dot(tril, part, preferred_element_type=F32) for part in _split3(x))


def _inproj_kernel(x_ref, g_ref, wm_ref, ws_ref, p_ref, s_ref, *, col_chunk):
    h = _rmsnorm(x_ref[...], g_ref[...]).astype(BF16)
    s_ref[...] = jnp.dot(h, ws_ref[...], preferred_element_type=F32)
    for c in range(wm_ref.shape[1] // col_chunk):
        sl = slice(c * col_chunk, (c + 1) * col_chunk)
        p_ref[:, sl] = jnp.dot(h, wm_ref[:, sl], preferred_element_type=F32).astype(BF16)


def _inproj(x2, g, wm, ws, *, tm=512, col_chunk=1024):
    n, d = x2.shape
    wcols = wm.shape[1]
    return pl.pallas_call(
        functools.partial(_inproj_kernel, col_chunk=col_chunk),
        grid=(n // tm,),
        in_specs=[pl.BlockSpec((tm, d), lambda i: (i, 0)),
                  _resident((1, d)), _resident((d, wcols)), _resident((d, LANES))],
        out_specs=[pl.BlockSpec((tm, wcols), lambda i: (i, 0)),
                   pl.BlockSpec((tm, LANES), lambda i: (i, 0))],
        out_shape=[jax.ShapeDtypeStruct((n, wcols), BF16), jax.ShapeDtypeStruct((n, LANES), F32)],
        compiler_params=_cparams("parallel"),
        name="inproj",
    )(x2, g, wm, ws)


AUG_TERMS = 3
AUG_STRIDE = 8


def _aug_placement(dh):
    pairs = FOX_HEADS * dh // LANES
    width = pairs * LANES
    pq = np.zeros((AUG_TERMS * LANES, width), np.float32)
    pk = np.zeros((AUG_TERMS * LANES, width), np.float32)
    cq = np.zeros((1, width), np.float32)
    ck = np.zeros((1, width), np.float32)
    for h in range(FOX_HEADS):
        base = (h // 2) * LANES + (h % 2) * AUG_STRIDE
        for t in range(AUG_TERMS):
            src = t * LANES + 2 * ML_HEADS + h
            pq[src, base + t] = 1.0
            pk[src, base + AUG_TERMS + t] = -1.0
            cq[0, base + AUG_TERMS + t] = 1.0
            ck[0, base + t] = 1.0
    return jnp.asarray(pq, BF16), jnp.asarray(pk, BF16), jnp.asarray(cq), jnp.asarray(ck)


def _gateprep_kernel(s_ref, b_ref, pq_ref, pk_ref, cq_ref, ck_ref, col_ref, row_ref, aq_ref, ak_ref, carry_ref):
    t = s_ref.shape[1]

    @pl.when(pl.program_id(1) == 0)
    def _():
        carry_ref[...] = jnp.zeros_like(carry_ref)

    pre = s_ref[0] + b_ref[...]
    lane = lax.broadcasted_iota(jnp.int32, pre.shape, 1)
    ls = _log_sigmoid(pre)
    forget = (lane >= ML_HEADS) & (lane < 2 * ML_HEADS + FOX_HEADS)
    tril = (lax.broadcasted_iota(jnp.int32, (t, t), 0) >= lax.broadcasted_iota(jnp.int32, (t, t), 1)).astype(BF16)
    fcum = _cumsum_rows(tril, jnp.where(forget, ls, 0.0)) + carry_ref[...]
    carry_ref[...] = fcum[t - 1:t, :]
    col = jnp.where(lane < ML_HEADS, pre, fcum)
    col_ref[0] = col
    row_ref[0] = col.T[0:row_ref.shape[1], :]
    parts = jnp.concatenate(_split3(fcum), axis=-1)
    aq_ref[0] = (jnp.dot(parts, pq_ref[...], preferred_element_type=F32) + cq_ref[...]).astype(BF16)
    ak_ref[0] = (jnp.dot(parts, pk_ref[...], preferred_element_type=F32) + ck_ref[...]).astype(BF16)


def _gateprep(s3, bias, dh, *, tg=512):
    b, s, _ = s3.shape
    n_rows = 2 * ML_HEADS + FOX_HEADS
    pq, pk, cq, ck = _aug_placement(dh)
    width = pq.shape[1]
    return pl.pallas_call(
        _gateprep_kernel,
        grid=(b, s // tg),
        in_specs=[pl.BlockSpec((1, tg, LANES), lambda i, j: (i, j, 0)), _resident((1, LANES)),
                  _resident(pq.shape), _resident(pk.shape), _resident(cq.shape), _resident(ck.shape)],
        out_specs=[pl.BlockSpec((1, tg, LANES), lambda i, j: (i, j, 0)),
                   pl.BlockSpec((1, n_rows, tg), lambda i, j: (i, 0, j)),
                   pl.BlockSpec((1, tg, width), lambda i, j: (i, j, 0)),
                   pl.BlockSpec((1, tg, width), lambda i, j: (i, j, 0))],
        out_shape=[jax.ShapeDtypeStruct((b, s, LANES), F32), jax.ShapeDtypeStruct((b, n_rows, s), F32),
                   jax.ShapeDtypeStruct((b, s, width), BF16), jax.ShapeDtypeStruct((b, s, width), BF16)],
        scratch_shapes=[pltpu.VMEM((1, LANES), F32)],
        compiler_params=_cparams("parallel", "arbitrary"),
        name="gateprep",
    )(s3, bias, pq, pk, cq, ck)


def _fox_kernel(q_ref, aq_ref, k_ref, ak_ref, v_ref, o_ref, m_ref, acc_ref, *, tq, tk, dh, scale):
    qi = pl.program_id(2)
    lane = lax.broadcasted_iota(jnp.int32, (tq, LANES), 1)
    q2 = q_ref[0] * scale
    aq = aq_ref[0]
    in_head = [(lane >= a * dh) & (lane < (a + 1) * dh) for a in range(2)]
    q_ops = []
    for a in range(2):
        in_aug = (lane >= a * AUG_STRIDE) & (lane < a * AUG_STRIDE + 2 * AUG_TERMS)
        q_ops.append(jnp.concatenate([jnp.where(in_head[a], q2, jnp.zeros_like(q2)),
                                      jnp.where(in_aug, aq, jnp.zeros_like(aq))], axis=-1))
    m_ref[...] = jnp.full_like(m_ref, NEG_BIG)
    acc_ref[...] = jnp.zeros_like(acc_ref)
    causal = (lax.broadcasted_iota(jnp.int32, (tq, tq), 1) <= lax.broadcasted_iota(jnp.int32, (tq, tq), 0))

    def step(start, width, masked):
        kb = jnp.concatenate([k_ref[0, pl.ds(start, width), :], ak_ref[0, pl.ds(start, width), :]], axis=-1)
        vb = v_ref[0, pl.ds(start, width), :]
        key_lane = lax.broadcasted_iota(jnp.int32, (width, LANES), 1)
        key_head = [(key_lane >= a * dh) & (key_lane < (a + 1) * dh) for a in range(2)]
        for a in range(2):
            s = lax.dot_general(q_ops[a], kb, (((1,), (1,)), ((), ())), preferred_element_type=F32)
            if masked:
                s = jnp.where(causal, s, NEG_BIG)
            m_prev = m_ref[a]
            m_new = jnp.maximum(m_prev, jnp.max(s, axis=-1, keepdims=True))
            alpha = jnp.exp(m_prev - m_new)
            pexp = jnp.exp(s - jnp.concatenate([m_new] * (width // LANES), axis=-1))
            v_op = jnp.where(key_head[a], vb, jnp.ones_like(vb))
            acc_ref[a] = alpha * acc_ref[a] + jnp.dot(pexp.astype(BF16), v_op, preferred_element_type=F32)
            m_ref[a] = m_new

    n_wide = (qi * tq) // tk
    n_narrow = qi - n_wide * (tk // tq)

    def wide_body(j, carry):
        step(pl.multiple_of(j * tk, tk), tk, False)
        return carry

    def narrow_body(j, carry):
        step(pl.multiple_of((n_wide * (tk // tq) + j) * tq, tq), tq, False)
        return carry

    lax.fori_loop(0, n_wide, wide_body, 0)
    if tk != tq:
        lax.fori_loop(0, n_narrow, narrow_body, 0)
    step(pl.multiple_of(qi * tq, tq), tq, True)
    outs = [acc_ref[a] / pltpu.roll(acc_ref[a], LANES // 2, 1) for a in range(2)]
    o_ref[0] = jnp.where(in_head[0], outs[0], outs[1]).astype(BF16)


def _fox(p3, aq, ak, *, q_col, k_col, v_col, dh, tq=1024, tk=1024):
    b, s, _ = p3.shape
    assert 2 * dh == LANES, "two heads share one 128-lane block"
    assert tk % tq == 0 and s % tq == 0
    pairs = FOX_HEADS * dh // LANES
    scale = dh ** -0.5
    assert math.frexp(scale)[0] == 0.5, "score scale is folded into bf16 q; exact only for a power of two"
    qb, kb, vb = q_col // LANES, k_col // LANES, v_col // LANES
    return pl.pallas_call(
        functools.partial(_fox_kernel, tq=tq, tk=tk, dh=dh, scale=scale),
        grid=(b, pairs, s // tq),
        in_specs=[pl.BlockSpec((1, tq, LANES), lambda i, p, q: (i, q, qb + p)),
                  pl.BlockSpec((1, tq, LANES), lambda i, p, q: (i, q, p)),
                  pl.BlockSpec((1, s, LANES), lambda i, p, q: (i, 0, kb + p)),
                  pl.BlockSpec((1, s, LANES), lambda i, p, q: (i, 0, p)),
                  pl.BlockSpec((1, s, LANES), lambda i, p, q: (i, 0, vb + p))],
        out_specs=pl.BlockSpec((1, tq, LANES), lambda i, p, q: (i, q, p)),
        out_shape=jax.ShapeDtypeStruct((b, s, pairs * LANES), BF16),
        scratch_shapes=[pltpu.VMEM((2, tq, LANES), F32), pltpu.VMEM((2, tq, LANES), F32)],
        compiler_params=_cparams("parallel", "parallel", "arbitrary"),
        name="fox_attention",
    )(p3, aq, p3, ak, p3)


def _mlstm_kernel(qk_ref, halo_ref, v_ref, og_ref, gc_ref, gr_ref, cw_ref, ng_ref, out_ref,
                  ubuf, ct_ref, n_ref, m_ref, g0_ref, *, chunk, dh):
    c = pl.program_id(1)
    width = ML_HEADS * dh
    taps = cw_ref.shape[0]
    pad = halo_ref.shape[1]
    assert 3 * ML_HEADS <= BF16_SUBLANES

    @pl.when(c == 0)
    def _():
        ct_ref[...] = jnp.zeros_like(ct_ref)
        n_ref[...] = jnp.zeros_like(n_ref)
        m_ref[...] = jnp.zeros_like(m_ref)
        g0_ref[...] = jnp.zeros_like(g0_ref)

    ubuf[0:pad, :] = jnp.where(c == 0, 0.0, halo_ref[0].astype(F32))
    ubuf[pad:pad + chunk, :] = qk_ref[0].astype(F32)

    def conv_silu(col0):
        y = sum(ubuf[pad - taps + 1 + j:pad - taps + 1 + j + chunk, col0:col0 + dh] * cw_ref[j:j + 1, col0:col0 + dh]
                for j in range(taps))
        return y * _sigmoid(y)

    ri = lax.broadcasted_iota(jnp.int32, (chunk, chunk), 0)
    ci = lax.broadcasted_iota(jnp.int32, (chunk, chunk), 1)
    visible = ri <= ci
    gc = gc_ref[0]
    gr = gr_ref[0]
    g0 = g0_ref[...]
    g0_ref[...] = gc[chunk - 1:chunk, :]
    contract_last = (((1,), (1,)), ((), ()))
    contract_first = (((0,), (0,)), ((), ()))
    part_row = lax.broadcasted_iota(jnp.int32, (BF16_SUBLANES, dh), 0)
    slab_row = lax.broadcasted_iota(jnp.int32, (BF16_SUBLANES, chunk), 0)
    ones_dh = jnp.ones((dh, dh), BF16)

    slab = jnp.zeros((BF16_SUBLANES, chunk), F32)
    heads = []
    for h in range(ML_HEADS):
        hs = slice(h * dh, (h + 1) * dh)
        q = conv_silu(h * dh)
        k = conv_silu(width + h * dh) * dh ** -0.5
        qb, kb = q.astype(BF16), k.astype(BF16)
        vb = v_ref[0, :, hs]
        g0_h = g0[:, ML_HEADS + h:ML_HEADS + h + 1]
        key_col = gc[:, h:h + 1] - (gc[:, ML_HEADS + h:ML_HEADS + h + 1] - g0_h)
        li_r = gr[h:h + 1, :]
        b_r = gr[ML_HEADS + h:ML_HEADS + h + 1, :] - g0_h
        b_last = b_r[:, chunk - 1:chunk]
        m_prev = m_ref[h:h + 1, 0:1]

        d_t = jnp.where(visible, b_r + key_col, NEG_BIG)
        m_inter = b_r + m_prev
        m_t = jnp.maximum(m_inter, jnp.max(d_t, axis=0, keepdims=True))
        w_inter = jnp.exp(m_inter - m_t)
        p_t = jnp.exp(d_t - m_t) * lax.dot_general(kb, qb, contract_last, preferred_element_type=F32)
        n_hi, n_mid, n_lo = (part.astype(F32) for part in _split3(n_ref[h]))
        n_parts = jnp.where(part_row == 0, n_hi, jnp.where(part_row == 1, n_mid,
                                                           jnp.where(part_row == 2, n_lo, 0.0))).astype(BF16)
        nq = jnp.sum(lax.dot_general(n_parts, qb, contract_last, preferred_element_type=F32), axis=0, keepdims=True)
        den = w_inter * nq + jnp.sum(p_t, axis=0, keepdims=True)
        inv = 1.0 / jnp.maximum(jnp.abs(den), jnp.exp(-m_t))

        g_r = b_last - b_r + li_r
        m_new = jnp.maximum(b_last + m_prev, jnp.max(g_r, axis=-1, keepdims=True))
        decay = jnp.exp(b_last + m_prev - m_new)
        wk_r = jnp.exp(g_r - m_new)
        for slot, vec in enumerate((w_inter, inv, wk_r)):
            slab = jnp.where(slab_row == 3 * h + slot, vec, slab)
        heads.append((hs, k, qb, vb, p_t.astype(BF16), decay, m_new))

    cols = jnp.concatenate([slab, jnp.zeros((LANES - BF16_SUBLANES, chunk), F32)], axis=0).T

    for h, (hs, k, qb, vb, p_tb, decay, m_new) in enumerate(heads):
        w_col, inv_col, wk_col = (cols[:, 3 * h + slot:3 * h + slot + 1] for slot in range(3))
        num = (w_col * jnp.dot(qb, ct_ref[h].astype(BF16), preferred_element_type=F32)
               + lax.dot_general(p_tb, vb, contract_first, preferred_element_type=F32))
        hh = num * inv_col
        kw = k * wk_col
        ct_ref[h] = decay * ct_ref[h] + lax.dot_general(kw.astype(BF16), vb, contract_first,
                                                       preferred_element_type=F32)
        n_ref[h] = decay * n_ref[h] + jnp.sum(kw, axis=0, keepdims=True)
        m_ref[h:h + 1, :] = jnp.broadcast_to(m_new, (1, LANES))

        sq = hh * hh
        sq_hi = sq.astype(BF16)
        sq_lo = (sq - sq_hi.astype(F32)).astype(BF16)
        mean_sq = (jnp.dot(sq_hi, ones_dh, preferred_element_type=F32)
                   + jnp.dot(sq_lo, ones_dh, preferred_element_type=F32)) * (1.0 / dh)
        hn = hh * lax.rsqrt(mean_sq + NORM_EPS) * ng_ref[:, hs]
        out_ref[0, :, hs] = (hn * _sigmoid(og_ref[0, :, hs].astype(F32))).astype(BF16)


def _mlstm(p3, gcol, grow, conv_w, norm_g, *, qk_col, v_col, o_col, dh, chunk=256):
    b, s, _ = p3.shape
    width = ML_HEADS * dh
    pad = BF16_SUBLANES
    assert conv_w.shape[0] - 1 <= pad
    qkb, vb, ob = qk_col // (2 * width), v_col // width, o_col // width
    halo_per_chunk = chunk // pad
    return pl.pallas_call(
        functools.partial(_mlstm_kernel, chunk=chunk, dh=dh),
        grid=(b, s // chunk),
        in_specs=[pl.BlockSpec((1, chunk, 2 * width), lambda i, c: (i, c, qkb)),
                  pl.BlockSpec((1, pad, 2 * width),
                               lambda i, c: (i, jnp.maximum(c * halo_per_chunk - 1, 0), qkb)),
                  pl.BlockSpec((1, chunk, width), lambda i, c: (i, c, vb)),
                  pl.BlockSpec((1, chunk, width), lambda i, c: (i, c, ob)),
                  pl.BlockSpec((1, chunk, LANES), lambda i, c: (i, c, 0)),
                  pl.BlockSpec((1, grow.shape[1], chunk), lambda i, c: (i, 0, c)),
                  _resident(conv_w.shape), _resident((1, width))],
        out_specs=pl.BlockSpec((1, chunk, width), lambda i, c: (i, c, 0)),
        out_shape=jax.ShapeDtypeStruct((b, s, width), BF16),
        scratch_shapes=[pltpu.VMEM((pad + chunk, 2 * width), F32),
                        pltpu.VMEM((ML_HEADS, dh, dh), F32),
                        pltpu.VMEM((ML_HEADS, 1, dh), F32),
                        pltpu.VMEM((ML_HEADS, LANES), F32),
                        pltpu.VMEM((1, LANES), F32)],
        compiler_params=_cparams("parallel", "arbitrary"),
        name="mlstm",
    )(p3, p3, p3, p3, gcol, grow, conv_w, norm_g)


def _merge_kernel(u_ref, halo_ref, hm_ref, fo_ref, gp_ref, x_ref, wg_ref, ps_ref, wbp_ref, wbm_ref, wbf_ref,
                  wo_ref, o_ref, ubuf, *, tm):
    j = pl.program_id(1)
    pad = halo_ref.shape[1]
    gw = wg_ref.shape[1]
    d = x_ref.shape[2]
    ubuf[0:pad, :] = jnp.where(j == 0, 0.0, halo_ref[0].astype(F32))
    ubuf[pad:pad + tm, :] = u_ref[0].astype(F32)
    pos = j * tm + lax.broadcasted_iota(jnp.int32, (tm, 1), 0)
    ys = []
    for g, w in enumerate(POOL_WINDOWS):
        gs = slice(g * gw, (g + 1) * gw)
        wsum = sum(ubuf[pad - k:pad - k + tm, gs] for k in range(w))
        cnt = jnp.minimum(pos + 1, w).astype(F32)
        dlt = wsum / cnt - ubuf[pad:pad + tm, gs]
        ys.append(jnp.dot(dlt.astype(BF16), wg_ref[g], preferred_element_type=F32))
    y_pool = (jnp.concatenate(ys, axis=-1) * ps_ref[...]).astype(BF16)
    merged = _sigmoid(gp_ref[0, :, 0:d].astype(F32)) * jnp.dot(y_pool, wbp_ref[...], preferred_element_type=F32)
    merged += _sigmoid(gp_ref[0, :, d:2 * d].astype(F32)) * jnp.dot(hm_ref[0], wbm_ref[...],
                                                                   preferred_element_type=F32)
    merged += _sigmoid(gp_ref[0, :, 2 * d:3 * d].astype(F32)) * jnp.dot(fo_ref[0], wbf_ref[...],
                                                                       preferred_element_type=F32)
    o_ref[0] = x_ref[0] + jnp.dot(merged.astype(BF16), wo_ref[...], preferred_element_type=F32)


def _merge(p3, hm, fo, x3, wgrp, pscale, wbp, wbm, wbf, wo, *, pool_col, gate_col, tm=512):
    b, s, d = x3.shape
    pw = wbp.shape[0]
    pad = BF16_SUBLANES
    assert max(POOL_WINDOWS) - 1 <= pad and gate_col == 0
    pcb = pool_col // pw
    halo_per_blk = tm // pad
    return pl.pallas_call(
        functools.partial(_merge_kernel, tm=tm),
        grid=(b, s // tm),
        in_specs=[pl.BlockSpec((1, tm, pw), lambda i, j: (i, j, pcb)),
                  pl.BlockSpec((1, pad, pw), lambda i, j: (i, jnp.maximum(j * halo_per_blk - 1, 0), pcb)),
                  pl.BlockSpec((1, tm, hm.shape[2]), lambda i, j: (i, j, 0)),
                  pl.BlockSpec((1, tm, fo.shape[2]), lambda i, j: (i, j, 0)),
                  pl.BlockSpec((1, tm, N_BRANCH * d), lambda i, j: (i, j, 0)),
                  pl.BlockSpec((1, tm, d), lambda i, j: (i, j, 0)),
                  _resident(wgrp.shape), _resident(pscale.shape), _resident(wbp.shape),
                  _resident(wbm.shape), _resident(wbf.shape), _resident(wo.shape)],
        out_specs=pl.BlockSpec((1, tm, d), lambda i, j: (i, j, 0)),
        out_shape=jax.ShapeDtypeStruct((b, s, d), F32),
        scratch_shapes=[pltpu.VMEM((pad + tm, pw), F32)],
        compiler_params=_cparams("parallel", "parallel"),
        name="merge",
    )(p3, p3, hm, fo, p3, x3, wgrp, pscale, wbp, wbm, wbf, wo)


def _swiglu_kernel(x_ref, g_ref, wg_ref, wu_ref, wd_ref, o_ref, a_ref, *, ff_chunk):
    x = x_ref[...]
    h = _rmsnorm(x, g_ref[...]).astype(BF16)
    for c in range(wg_ref.shape[1] // ff_chunk):
        sl = slice(c * ff_chunk, (c + 1) * ff_chunk)
        gate = jnp.dot(h, wg_ref[:, sl], preferred_element_type=F32)
        up = jnp.dot(h, wu_ref[:, sl], preferred_element_type=F32)
        a_ref[:, sl] = (gate * _sigmoid(gate) * up).astype(BF16)
    o_ref[...] = x + jnp.dot(a_ref[...], wd_ref[...], preferred_element_type=F32)


def _swiglu(x2, g, wg, wu, wd, *, tm=512, ff_chunk=256):
    n, d = x2.shape
    ff = wg.shape[1]
    assert ff % ff_chunk == 0
    return pl.pallas_call(
        functools.partial(_swiglu_kernel, ff_chunk=ff_chunk),
        grid=(n // tm,),
        in_specs=[pl.BlockSpec((tm, d), lambda i: (i, 0)), _resident((1, d)),
                  _resident(wg.shape), _resident(wu.shape), _resident(wd.shape)],
        out_specs=pl.BlockSpec((tm, d), lambda i: (i, 0)),
        out_shape=jax.ShapeDtypeStruct((n, d), F32),
        scratch_shapes=[pltpu.VMEM((tm, ff), BF16)],
        compiler_params=_cparams("parallel"),
        name="dense_swiglu",
    )(x2, g, wg, wu, wd)


def _router_kernel(x_ref, g_ref, wr_ref, br_ref, e_ref, gt_ref, cnt_ref, carry_ref):
    h = _rmsnorm(x_ref[...], g_ref[...])
    h_hi = h.astype(BF16)
    h_lo = (h - h_hi.astype(F32)).astype(BF16)
    logits = jnp.dot(jnp.concatenate([h_hi, h_hi, h_lo], axis=-1), wr_ref[...],
                     preferred_element_type=F32) + br_ref[...]
    lane = lax.broadcasted_iota(jnp.int32, logits.shape, 1)
    m1 = jnp.max(logits, axis=-1, keepdims=True)
    i1 = jnp.min(jnp.where(logits == m1, lane, LANES), axis=-1, keepdims=True)
    rest = jnp.where(lane == i1, NEG_BIG, logits)
    m2 = jnp.max(rest, axis=-1, keepdims=True)
    i2 = jnp.min(jnp.where(rest == m2, lane, LANES), axis=-1, keepdims=True)
    e2 = jnp.exp(m2 - m1)
    g1 = 1.0 / (1.0 + e2)
    gt_ref[...] = jnp.where(lane == 0, g1, jnp.where(lane == 1, e2 * g1, 0.0))

    @pl.when(pl.program_id(0) == 0)
    def _():
        carry_ref[...] = jnp.zeros_like(carry_ref)

    tm = logits.shape[0]
    pick0, pick1 = lane == i1, lane == i2
    picks = jnp.where(pick0 | pick1, 1.0, 0.0)
    earlier = (lax.broadcasted_iota(jnp.int32, (tm, tm), 0) > lax.broadcasted_iota(jnp.int32, (tm, tm), 1))
    prior = jnp.dot(earlier.astype(BF16), picks.astype(BF16), preferred_element_type=F32) + carry_ref[...]
    r0 = jnp.sum(jnp.where(pick0, prior, 0.0), axis=-1, keepdims=True).astype(jnp.int32)
    r1 = jnp.sum(jnp.where(pick1, prior, 0.0), axis=-1, keepdims=True).astype(jnp.int32)
    carry_ref[...] = prior[tm - 1:tm, :] + picks[tm - 1:tm, :]
    cnt_ref[...] = jnp.broadcast_to(carry_ref[...], cnt_ref.shape).astype(jnp.int32)
    e_ref[...] = jnp.where(lane == 0, i1, jnp.where(lane == 1, i2, jnp.where(lane == 2, r0,
                                                                          jnp.where(lane == 3, r1, 0))))


def _router(x2, g, wr, br, *, tm=512):
    n, d = x2.shape
    return pl.pallas_call(
        _router_kernel,
        grid=(n // tm,),
        in_specs=[pl.BlockSpec((tm, d), lambda i: (i, 0)), _resident((1, d)),
                  _resident(wr.shape), _resident((1, LANES))],
        out_specs=[pl.BlockSpec((tm, LANES), lambda i: (i, 0)), pl.BlockSpec((tm, LANES), lambda i: (i, 0)),
                   pl.BlockSpec((8, LANES), lambda i: (0, 0))],
        out_shape=[jax.ShapeDtypeStruct((n, LANES), jnp.int32), jax.ShapeDtypeStruct((n, LANES), F32),
                   jax.ShapeDtypeStruct((8, LANES), jnp.int32)],
        scratch_shapes=[pltpu.VMEM((1, LANES), F32)],
        compiler_params=_cparams("arbitrary"),
        name="router",
    )(x2, g, wr, br)


def _dispatch_kernel(pend_ref, dest_ref, x_ref, xs_ref, zero_ref, sem, zsem):
    tm = dest_ref.shape[2] // TOP_K
    rows = zero_ref.shape[0]
    step = pl.program_id(0)

    @pl.when(step == 0)
    def _():
        zero_ref[...] = jnp.zeros_like(zero_ref)

        def zero_copy(e):
            start = pl.multiple_of(pend_ref[e] - rows, rows)
            return pltpu.make_async_copy(zero_ref, xs_ref.at[pl.ds(start, rows)], zsem)

        def has_rows(e):
            return pend_ref[e] > (pend_ref[e - 1] if e else 0)

        n_exp = pend_ref.shape[0]
        n_rows = xs_ref.shape[0]
        tail_starts = [n_rows - (j + 1) * rows for j in range(min(n_exp, n_rows // rows))]

        def tail_copy(start):
            return pltpu.make_async_copy(zero_ref, xs_ref.at[pl.ds(start, rows)], zsem)

        for e in range(n_exp):
            @pl.when(has_rows(e))
            def _():
                zero_copy(e).start()
        for start in tail_starts:
            @pl.when(start >= pend_ref[n_exp - 1])
            def _():
                tail_copy(start).start()
        for e in range(n_exp):
            @pl.when(has_rows(e))
            def _():
                zero_copy(e).wait()
        for start in tail_starts:
            @pl.when(start >= pend_ref[n_exp - 1])
            def _():
                tail_copy(start).wait()

    def row_copy(t, k):
        return pltpu.make_async_copy(x_ref.at[pl.ds(t, 1)], xs_ref.at[pl.ds(dest_ref[0, 0, TOP_K * t + k], 1)], sem)

    def start(t, carry):
        for k in range(TOP_K):
            row_copy(t, k).start()
        return carry

    def wait(t, carry):
        for k in range(TOP_K):
            row_copy(t, k).wait()
        return carry

    lax.fori_loop(0, tm, start, 0, unroll=8)
    lax.fori_loop(0, tm, wait, 0, unroll=8)


def _dispatch(pend, dest2, x2, *, n_rows, rows, tm=256):
    n, d = x2.shape
    grid_spec = pltpu.PrefetchScalarGridSpec(
        num_scalar_prefetch=1,
        grid=(n // tm,),
        in_specs=[pl.BlockSpec((1, 1, TOP_K * tm), lambda i, pe: (i, 0, 0), memory_space=pltpu.SMEM),
                  pl.BlockSpec((tm, d), lambda i, pe: (i, 0))],
        out_specs=pl.BlockSpec(memory_space=pl.ANY),
        scratch_shapes=[pltpu.VMEM((rows, d), F32), pltpu.SemaphoreType.DMA(()), pltpu.SemaphoreType.DMA(())],
    )
    return pl.pallas_call(
        _dispatch_kernel,
        grid_spec=grid_spec,
        out_shape=jax.ShapeDtypeStruct((n_rows, d), F32),
        compiler_params=_cparams("arbitrary"),
        name="moe_dispatch",
    )(pend, dest2, x2)


def _experts_kernel(blk_e_ref, nact_ref, xs_ref, g_ref, wg_ref, wu_ref, wd_ref, ys_ref, h_ref, a_ref, acc_ref,
                    *, ff_chunk):
    del blk_e_ref
    i = pl.program_id(0)
    f = pl.program_id(1)

    @pl.when(i < nact_ref[0])
    def _():
        @pl.when(f == 0)
        def _():
            h_ref[...] = _rmsnorm(xs_ref[...], g_ref[...]).astype(BF16)
            acc_ref[...] = jnp.zeros_like(acc_ref)

        h = h_ref[...]
        for c in range(wg_ref.shape[2] // ff_chunk):
            sl = slice(c * ff_chunk, (c + 1) * ff_chunk)
            gate = jnp.dot(h, wg_ref[0, :, sl], preferred_element_type=F32)
            up = jnp.dot(h, wu_ref[0, :, sl], preferred_element_type=F32)
            a_ref[:, sl] = (gate * _sigmoid(gate) * up).astype(BF16)
        acc_ref[...] += jnp.dot(a_ref[...], wd_ref[0], preferred_element_type=F32)

        @pl.when(f == pl.num_programs(1) - 1)
        def _():
            ys_ref[...] = acc_ref[...]

    @pl.when((i >= nact_ref[0]) & (f == 0))
    def _():
        ys_ref[...] = jnp.zeros_like(ys_ref)


def _experts(blk_e, nact, xs, g, wg, wu, wd, *, rows, tf=1792, ff_chunk=256):
    n_rows, d = xs.shape
    ff = wg.shape[2]
    nf = ff // tf
    assert ff % tf == 0 and tf % ff_chunk == 0 and n_rows % rows == 0

    def blk(i, nact):
        return jnp.minimum(i, nact[0] - 1)

    def ff_tile(i, f, nact):
        return jnp.where(i < nact[0], f, nf - 1)

    grid_spec = pltpu.PrefetchScalarGridSpec(
        num_scalar_prefetch=2,
        grid=(n_rows // rows, nf),
        in_specs=[pl.BlockSpec((rows, d), lambda i, f, be, na: (blk(i, na), 0)),
                  pl.BlockSpec((1, d), lambda i, f, be, na: (0, 0)),
                  pl.BlockSpec((1, d, tf), lambda i, f, be, na: (be[blk(i, na)], 0, ff_tile(i, f, na))),
                  pl.BlockSpec((1, d, tf), lambda i, f, be, na: (be[blk(i, na)], 0, ff_tile(i, f, na))),
                  pl.BlockSpec((1, tf, d), lambda i, f, be, na: (be[blk(i, na)], ff_tile(i, f, na), 0))],
        out_specs=pl.BlockSpec((rows, d), lambda i, f, be, na: (i, 0)),
        scratch_shapes=[pltpu.VMEM((rows, d), BF16), pltpu.VMEM((rows, tf), BF16), pltpu.VMEM((rows, d), F32)],
    )
    return pl.pallas_call(
        functools.partial(_experts_kernel, ff_chunk=ff_chunk),
        grid_spec=grid_spec,
        out_shape=jax.ShapeDtypeStruct((n_rows, d), F32),
        compiler_params=_cparams("arbitrary", "arbitrary"),
        name="moe_experts",
    )(blk_e, nact, xs, g, wg, wu, wd)


def _combine_kernel(dest_ref, next_dest_ref, x_ref, gt_ref, fg_ref, ys_ref, o_ref, buf, sems, *, final_norm):
    tm = x_ref.shape[0]
    step = pl.program_id(0)
    slot = step % 2

    def row_copy(dref, s, t, k):
        return pltpu.make_async_copy(ys_ref.at[pl.ds(dref[0, 0, TOP_K * t + k], 1)], buf.at[s, k, pl.ds(t, 1)],
                                     sems.at[s])

    def gather(dref, s):
        def start(t, carry):
            for k in range(TOP_K):
                row_copy(dref, s, t, k).start()
            return carry
        lax.fori_loop(0, tm, start, 0, unroll=8)

    @pl.when(step == 0)
    def _():
        gather(dest_ref, 0)

    @pl.when(step + 1 < pl.num_programs(0))
    def _():
        gather(next_dest_ref, 1 - slot)

    def wait(t, carry):
        for k in range(TOP_K):
            row_copy(dest_ref, slot, t, k).wait()
        return carry

    lax.fori_loop(0, tm, wait, 0, unroll=8)
    gt = gt_ref[...]
    y = x_ref[...]
    for k in range(TOP_K):
        y = y + buf[slot, k] * gt[:, k:k + 1]
    o_ref[...] = _rmsnorm(y, fg_ref[...]) if final_norm else y


def _combine(dest2, x2, gates, fg, ys, *, final_norm, tm=256):
    n, d = x2.shape
    steps = n // tm
    return pl.pallas_call(
        functools.partial(_combine_kernel, final_norm=final_norm),
        grid=(steps,),
        in_specs=[pl.BlockSpec((1, 1, TOP_K * tm), lambda i: (i, 0, 0), memory_space=pltpu.SMEM),
                  pl.BlockSpec((1, 1, TOP_K * tm), lambda i: (jnp.minimum(i + 1, steps - 1), 0, 0),
                               memory_space=pltpu.SMEM),
                  pl.BlockSpec((tm, d), lambda i: (i, 0)),
                  pl.BlockSpec((tm, LANES), lambda i: (i, 0)),
                  _resident((1, d)),
                  pl.BlockSpec(memory_space=pl.ANY)],
        out_specs=pl.BlockSpec((tm, d), lambda i: (i, 0)),
        out_shape=jax.ShapeDtypeStruct((n, d), F32),
        scratch_shapes=[pltpu.VMEM((2, TOP_K, tm, d), F32), pltpu.SemaphoreType.DMA((2,))],
        compiler_params=_cparams("arbitrary"),
        name="moe_combine",
    )(dest2, dest2, x2, gates, fg, ys)


def _final_norm_kernel(x_ref, g_ref, o_ref):
    o_ref[...] = _rmsnorm(x_ref[...], g_ref[...])


def _final_norm(x2, g, *, tm=512):
    n, d = x2.shape
    return pl.pallas_call(
        _final_norm_kernel,
        grid=(n // tm,),
        in_specs=[pl.BlockSpec((tm, d), lambda i: (i, 0)), _resident((1, d))],
        out_specs=pl.BlockSpec((tm, d), lambda i: (i, 0)),
        out_shape=jax.ShapeDtypeStruct((n, d), F32),
        compiler_params=_cparams("parallel"),
        name="final_norm",
    )(x2, g)


def _token_mixing(x3, norm_g, w_in, pool_w_grp, pool_scale, conv_w, b_i, b_f, ml_norm_g, fox_b_f,
                  w_br_pool, w_br_ml, w_br_fox, w_out):
    b, s, d = x3.shape
    pool_w = w_br_pool.shape[0]
    ml_w = w_br_ml.shape[0]
    fox_w = w_br_fox.shape[0]
    ml_dh = ml_w // ML_HEADS
    fox_dh = fox_w // FOX_HEADS
    n_small = 2 * ML_HEADS + FOX_HEADS

    sizes = (pool_w, ml_w, ml_w, ml_w, ml_w, ML_HEADS, ML_HEADS, fox_w, fox_w, fox_w, FOX_HEADS, N_BRANCH * d)
    offs = [0]
    for sz in sizes:
        offs.append(offs[-1] + sz)
    assert offs[-1] == w_in.shape[1]
    (o_pool, o_q, o_k, o_v, o_o, o_i, o_f, o_fq, o_fk, o_fv, o_ff, o_g) = offs[:-1]

    def cols(o, sz):
        return w_in[:, o:o + sz]

    wm = jnp.concatenate([cols(o_g, N_BRANCH * d), cols(o_q, ml_w), cols(o_k, ml_w), cols(o_v, ml_w),
                          cols(o_o, ml_w), cols(o_pool, pool_w), cols(o_fq, fox_w), cols(o_fk, fox_w),
                          cols(o_fv, fox_w)], axis=1).astype(BF16)
    c_gate = 0
    c_qk = c_gate + N_BRANCH * d
    c_v = c_qk + 2 * ml_w
    c_o = c_v + ml_w
    c_pool = c_o + ml_w
    c_fq = c_pool + pool_w
    c_fk = c_fq + fox_w
    c_fv = c_fk + fox_w
    ws = jnp.concatenate([cols(o_i, ML_HEADS), cols(o_f, ML_HEADS), cols(o_ff, FOX_HEADS),
                          jnp.zeros((d, LANES - n_small), w_in.dtype)], axis=1).astype(BF16)
    bias = jnp.concatenate([b_i, b_f, fox_b_f, jnp.zeros((LANES - n_small,), F32)]).reshape(1, LANES)

    proj, small = _inproj(x3.reshape(b * s, d), norm_g.reshape(1, d), wm, ws)
    p3 = proj.reshape(b, s, -1)
    gcol, grow, aq, ak = _gateprep(small.reshape(b, s, LANES), bias, fox_dh)
    fo = _fox(p3, aq, ak, q_col=c_fq, k_col=c_fk, v_col=c_fv, dh=fox_dh)
    hm = _mlstm(p3, gcol, grow, conv_w, ml_norm_g.reshape(1, ml_w), qk_col=c_qk, v_col=c_v, o_col=c_o, dh=ml_dh)
    return _merge(p3, hm, fo, x3, pool_w_grp.astype(BF16), pool_scale.reshape(1, pool_w),
                  w_br_pool.astype(BF16), w_br_ml.astype(BF16), w_br_fox.astype(BF16), w_out.astype(BF16),
                  pool_col=c_pool, gate_col=c_gate)


def _moe(x2, norm_g, w_router, b_router, w_gate, w_up, w_down, final_g, *, rows=512):
    n, d = x2.shape
    n_exp = w_router.shape[1]
    wr = jnp.concatenate([w_router.astype(F32), jnp.zeros((d, LANES - n_exp), F32)], axis=1)
    wr_hi = wr.astype(BF16)
    wr_lo = (wr - wr_hi.astype(F32)).astype(BF16)
    br = jnp.concatenate([b_router.astype(F32), jnp.full((LANES - n_exp,), NEG_BIG, F32)]).reshape(1, LANES)
    routed, gates, cnt = _router(x2, norm_g.reshape(1, d), jnp.concatenate([wr_hi, wr_lo, wr_hi], axis=0), br)

    e_flat = routed[:, :TOP_K].reshape(n * TOP_K)
    rank = routed[:, TOP_K:2 * TOP_K].reshape(n * TOP_K)
    counts = cnt[0, :n_exp]
    padded = ((counts + rows - 1) // rows) * rows
    pend = jnp.cumsum(padded)
    onehot = e_flat[:, None] == jnp.arange(n_exp, dtype=jnp.int32)[None, :]
    dest = (jnp.sum(jnp.where(onehot, (pend - padded)[None, :], 0), axis=1) + rank).astype(jnp.int32)
    n_blk = (n * TOP_K + n_exp * (rows - 1) + rows - 1) // rows
    blk_e = jnp.minimum(jnp.searchsorted(pend, jnp.arange(n_blk, dtype=jnp.int32) * rows, side='right'),
                        n_exp - 1).astype(jnp.int32)
    nact = (pend[-1:] // rows).astype(jnp.int32)

    tm = 256
    dest2 = dest.reshape(n // tm, 1, TOP_K * tm)
    xs = _dispatch(pend.astype(jnp.int32), dest2, x2, n_rows=n_blk * rows, rows=rows, tm=tm)
    ys = _experts(blk_e, nact, xs, norm_g.reshape(1, d), w_gate.astype(BF16), w_up.astype(BF16),
                  w_down.astype(BF16), rows=rows)
    fg = jnp.ones((1, d), F32) if final_g is None else final_g.reshape(1, d)
    return _combine(dest2, x2, gates, fg, ys, final_norm=final_g is not None, tm=tm)


def kernel(x, mix_norm_g, w_in, pool_w_grp, pool_scale, ml_conv_w, ml_b_i, ml_b_f, ml_norm_g, fox_b_f,
           w_br_pool, w_br_ml, w_br_fox, w_out, ffn_norm_g, ff_w_gate, ff_w_up, ff_w_down,
           moe_w_router, moe_b_router, moe_w_gate, moe_w_up, moe_w_down, final_norm_g):
    b, s, d = x.shape
    depth = mix_norm_g.shape[0]
    fused_final = False
    for l in range(depth):
        x = _token_mixing(x, mix_norm_g[l], w_in[l], pool_w_grp[l], pool_scale[l], ml_conv_w[l], ml_b_i[l],
                          ml_b_f[l], ml_norm_g[l], fox_b_f[l], w_br_pool[l], w_br_ml[l], w_br_fox[l], w_out[l])
        x2 = x.reshape(b * s, d)
        if l % 2 == 0:
            x2 = _swiglu(x2, ffn_norm_g[l].reshape(1, d), ff_w_gate[l // 2].astype(BF16),
                         ff_w_up[l // 2].astype(BF16), ff_w_down[l // 2].astype(BF16))
        else:
            fused_final = l == depth - 1
            x2 = _moe(x2, ffn_norm_g[l], moe_w_router[l // 2], moe_b_router[l // 2], moe_w_gate[l // 2],
                      moe_w_up[l // 2], moe_w_down[l // 2], final_norm_g if fused_final else None)
        x = x2.reshape(b, s, d)
    if not fused_final:
        x = _final_norm(x.reshape(b * s, d), final_norm_g.reshape(1, d)).reshape(b, s, d)
    return x
```

```python
import functools
import math

import jax
import jax.numpy as jnp
import numpy as np
from jax import lax
from jax.experimental import pallas as pl
from jax.experimental.pallas import tpu as pltpu

F32 = jnp.float32
BF16 = jnp.bfloat16

NORM_EPS = 1e-6
POOL_WINDOWS = (2, 4, 8, 16)
ML_HEADS = 4
FOX_HEADS = 8
TOP_K = 2
N_BRANCH = 3

LANES = 128
BF16_SUBLANES = 16
VMEM_LIMIT_BYTES = 56 * 1024 * 1024

NEG_BIG = -1e30


def _cparams(*sem):
    return pltpu.CompilerParams(dimension_semantics=sem, vmem_limit_bytes=VMEM_LIMIT_BYTES)


def _resident(shape):
    zeros = (0,) * len(shape)
    return pl.BlockSpec(shape, lambda *_: zeros, pipeline_mode=pl.Buffered(1))


def _rmsnorm(x, g):
    return x * lax.rsqrt(jnp.mean(x * x, axis=-1, keepdims=True) + NORM_EPS) * g


def _sigmoid(x):
    return 1.0 / (1.0 + jnp.exp(-x))


def _log_sigmoid(x):
    return jnp.minimum(x, 0.0) - jnp.log(1.0 + jnp.exp(-jnp.abs(x)))


def _split3(x):
    hi = x.astype(BF16)
    r = x - hi.astype(F32)
    mid = r.astype(BF16)
    lo = (r - mid.astype(F32)).astype(BF16)
    return hi, mid, lo


def _cumsum_rows(tril, x):
    return sum(jnp.dot(tril, part, preferred_element_type=F32) for part in _split3(x))


def _inproj_kernel(x_ref, g_ref, wm_ref, ws_ref, p_ref, s_ref, *, col_chunk):
    h = _rmsnorm(x_ref[...], g_ref[...]).astype(BF16)
    s_ref[...] = jnp.dot(h, ws_ref[...], preferred_element_type=F32)
    for c in range(wm_ref.shape[1] // col_chunk):
        sl = slice(c * col_chunk, (c + 1) * col_chunk)
        p_ref[:, sl] = jnp.dot(h, wm_ref[:, sl], preferred_element_type=F32).astype(BF16)


def _inproj(x2, g, wm, ws, *, tm=512, col_chunk=1024):
    n, d = x2.shape
    wcols = wm.shape[1]
    return pl.pallas_call(
        functools.partial(_inproj_kernel, col_chunk=col_chunk),
        grid=(n // tm,),
        in_specs=[pl.BlockSpec((tm, d), lambda i: (i, 0)),
                  _resident((1, d)), _resident((d, wcols)), _resident((d, LANES))],
        out_specs=[pl.BlockSpec((tm, wcols), lambda i: (i, 0)),
                   pl.BlockSpec((tm, LANES), lambda i: (i, 0))],
        out_shape=[jax.ShapeDtypeStruct((n, wcols), BF16), jax.ShapeDtypeStruct((n, LANES), F32)],
        compiler_params=_cparams("parallel"),
        name="inproj",
    )(x2, g, wm, ws)


AUG_TERMS = 3
AUG_STRIDE = 8


def _aug_placement(dh):
    pairs = FOX_HEADS * dh // LANES
    width = pairs * LANES
    pq = np.zeros((AUG_TERMS * LANES, width), np.float32)
    pk = np.zeros((AUG_TERMS * LANES, width), np.float32)
    cq = np.zeros((1, width), np.float32)
    ck = np.zeros((1, width), np.float32)
    for h in range(FOX_HEADS):
        base = (h // 2) * LANES + (h % 2) * AUG_STRIDE
        for t in range(AUG_TERMS):
            src = t * LANES + 2 * ML_HEADS + h
            pq[src, base + t] = 1.0
            pk[src, base + AUG_TERMS + t] = -1.0
            cq[0, base + AUG_TERMS + t] = 1.0
            ck[0, base + t] = 1.0
    return jnp.asarray(pq, BF16), jnp.asarray(pk, BF16), jnp.asarray(cq), jnp.asarray(ck)


def _gateprep_kernel(s_ref, b_ref, pq_ref, pk_ref, cq_ref, ck_ref, col_ref, row_ref, aq_ref, ak_ref, carry_ref):
    t = s_ref.shape[1]

    @pl.when(pl.program_id(1) == 0)
    def _():
        carry_ref[...] = jnp.zeros_like(carry_ref)

    pre = s_ref[0] + b_ref[...]
    lane = lax.broadcasted_iota(jnp.int32, pre.shape, 1)
    ls = _log_sigmoid(pre)
    forget = (lane >= ML_HEADS) & (lane < 2 * ML_HEADS + FOX_HEADS)
    tril = (lax.broadcasted_iota(jnp.int32, (t, t), 0) >= lax.broadcasted_iota(jnp.int32, (t, t), 1)).astype(BF16)
    fcum = _cumsum_rows(tril, jnp.where(forget, ls, 0.0)) + carry_ref[...]
    carry_ref[...] = fcum[t - 1:t, :]
    col = jnp.where(lane < ML_HEADS, pre, fcum)
    col_ref[0] = col
    row_ref[0] = col.T[0:row_ref.shape[1], :]
    parts = jnp.concatenate(_split3(fcum), axis=-1)
    aq_ref[0] = (jnp.dot(parts, pq_ref[...], preferred_element_type=F32) + cq_ref[...]).astype(BF16)
    ak_ref[0] = (jnp.dot(parts, pk_ref[...], preferred_element_type=F32) + ck_ref[...]).astype(BF16)


def _gateprep(s3, bias, dh, *, tg=512):
    b, s, _ = s3.shape
    n_rows = 2 * ML_HEADS + FOX_HEADS
    pq, pk, cq, ck = _aug_placement(dh)
    width = pq.shape[1]
    return pl.pallas_call(
        _gateprep_kernel,
        grid=(b, s // tg),
        in_specs=[pl.BlockSpec((1, tg, LANES), lambda i, j: (i, j, 0)), _resident((1, LANES)),
                  _resident(pq.shape), _resident(pk.shape), _resident(cq.shape), _resident(ck.shape)],
        out_specs=[pl.BlockSpec((1, tg, LANES), lambda i, j: (i, j, 0)),
                   pl.BlockSpec((1, n_rows, tg), lambda i, j: (i, 0, j)),
                   pl.BlockSpec((1, tg, width), lambda i, j: (i, j, 0)),
                   pl.BlockSpec((1, tg, width), lambda i, j: (i, j, 0))],
        out_shape=[jax.ShapeDtypeStruct((b, s, LANES), F32), jax.ShapeDtypeStruct((b, n_rows, s), F32),
                   jax.ShapeDtypeStruct((b, s, width), BF16), jax.ShapeDtypeStruct((b, s, width), BF16)],
        scratch_shapes=[pltpu.VMEM((1, LANES), F32)],
        compiler_params=_cparams("parallel", "arbitrary"),
        name="gateprep",
    )(s3, bias, pq, pk, cq, ck)


def _fox_kernel(q_ref, aq_ref, k_ref, ak_ref, v_ref, o_ref, m_ref, acc_ref, *, tq, tk, diag_strips, dh, scale):
    qi = pl.program_id(2)
    lane = lax.broadcasted_iota(jnp.int32, (tq, LANES), 1)
    q2 = q_ref[0] * scale
    aq = aq_ref[0]
    in_head = [(lane >= a * dh) & (lane < (a + 1) * dh) for a in range(2)]
    q_ops = []
    for a in range(2):
        in_aug = (lane >= a * AUG_STRIDE) & (lane < a * AUG_STRIDE + 2 * AUG_TERMS)
        q_ops.append(jnp.concatenate([jnp.where(in_head[a], q2, jnp.zeros_like(q2)),
                                      jnp.where(in_aug, aq, jnp.zeros_like(aq))], axis=-1))
    m_ref[...] = jnp.full_like(m_ref, NEG_BIG)
    acc_ref[...] = jnp.zeros_like(acc_ref)
    def step(start, width, row0=0, masked=False):
        rows = slice(row0, tq)
        kb = jnp.concatenate([k_ref[0, pl.ds(start, width), :], ak_ref[0, pl.ds(start, width), :]], axis=-1)
        vb = v_ref[0, pl.ds(start, width), :]
        key_lane = lax.broadcasted_iota(jnp.int32, (width, LANES), 1)
        key_head = [(key_lane >= a * dh) & (key_lane < (a + 1) * dh) for a in range(2)]
        if masked:
            causal = (lax.broadcasted_iota(jnp.int32, (tq - row0, width), 1)
                      <= lax.broadcasted_iota(jnp.int32, (tq - row0, width), 0))
        for a in range(2):
            s = lax.dot_general(q_ops[a][rows], kb, (((1,), (1,)), ((), ())), preferred_element_type=F32)
            if masked:
                s = jnp.where(causal, s, NEG_BIG)
            m_prev = m_ref[a, rows]
            m_new = jnp.maximum(m_prev, jnp.max(s, axis=-1, keepdims=True))
            alpha = jnp.exp(m_prev - m_new)
            pexp = jnp.exp(s - jnp.concatenate([m_new] * (width // LANES), axis=-1))
            v_op = jnp.where(key_head[a], vb, jnp.ones_like(vb))
            acc_ref[a, rows] = alpha * acc_ref[a, rows] + jnp.dot(pexp.astype(BF16), v_op,
                                                                  preferred_element_type=F32)
            m_ref[a, rows] = m_new

    n_wide = (qi * tq) // tk
    n_narrow = qi - n_wide * (tk // tq)

    def wide_body(j, carry):
        step(pl.multiple_of(j * tk, tk), tk)
        return carry

    def narrow_body(j, carry):
        step(pl.multiple_of((n_wide * (tk // tq) + j) * tq, tq), tq)
        return carry

    lax.fori_loop(0, n_wide, wide_body, 0)
    if tk != tq:
        lax.fori_loop(0, n_narrow, narrow_body, 0)
    strip = tq // diag_strips
    for c in range(diag_strips):
        step(pl.multiple_of(qi * tq + c * strip, strip), strip, row0=c * strip, masked=True)
    outs = [acc_ref[a] / pltpu.roll(acc_ref[a], LANES // 2, 1) for a in range(2)]
    o_ref[0] = jnp.where(in_head[0], outs[0], outs[1]).astype(BF16)


def _fox(p3, aq, ak, *, q_col, k_col, v_col, dh, tq=1024, tk=1024, diag_strips=2):
    b, s, _ = p3.shape
    assert 2 * dh == LANES, "two heads share one 128-lane block"
    assert tk % tq == 0 and s % tq == 0
    pairs = FOX_HEADS * dh // LANES
    scale = dh ** -0.5
    assert math.frexp(scale)[0] == 0.5, "score scale is folded into bf16 q; exact only for a power of two"
    qb, kb, vb = q_col // LANES, k_col // LANES, v_col // LANES
    return pl.pallas_call(
        functools.partial(_fox_kernel, tq=tq, tk=tk, diag_strips=diag_strips, dh=dh, scale=scale),
        grid=(b, pairs, s // tq),
        in_specs=[pl.BlockSpec((1, tq, LANES), lambda i, p, q: (i, q, qb + p)),
                  pl.BlockSpec((1, tq, LANES), lambda i, p, q: (i, q, p)),
                  pl.BlockSpec((1, s, LANES), lambda i, p, q: (i, 0, kb + p)),
                  pl.BlockSpec((1, s, LANES), lambda i, p, q: (i, 0, p)),
                  pl.BlockSpec((1, s, LANES), lambda i, p, q: (i, 0, vb + p))],
        out_specs=pl.BlockSpec((1, tq, LANES), lambda i, p, q: (i, q, p)),
        out_shape=jax.ShapeDtypeStruct((b, s, pairs * LANES), BF16),
        scratch_shapes=[pltpu.VMEM((2, tq, LANES), F32), pltpu.VMEM((2, tq, LANES), F32)],
        compiler_params=_cparams("parallel", "parallel", "arbitrary"),
        name="fox_attention",
    )(p3, aq, p3, ak, p3)


def _mlstm_kernel(qk_ref, halo_ref, v_ref, og_ref, gc_ref, gr_ref, cw_ref, ng_ref, out_ref,
                  ubuf, ct_ref, n_ref, m_ref, g0_ref, *, chunk, dh):
    c = pl.program_id(1)
    width = ML_HEADS * dh
    taps = cw_ref.shape[0]
    pad = halo_ref.shape[1]
    assert 3 * ML_HEADS <= BF16_SUBLANES

    @pl.when(c == 0)
    def _():
        ct_ref[...] = jnp.zeros_like(ct_ref)
        n_ref[...] = jnp.zeros_like(n_ref)
        m_ref[...] = jnp.zeros_like(m_ref)
        g0_ref[...] = jnp.zeros_like(g0_ref)

    ubuf[0:pad, :] = jnp.where(c == 0, 0.0, halo_ref[0].astype(F32))
    ubuf[pad:pad + chunk, :] = qk_ref[0].astype(F32)

    def conv_silu(col0):
        y = sum(ubuf[pad - taps + 1 + j:pad - taps + 1 + j + chunk, col0:col0 + dh] * cw_ref[j:j + 1, col0:col0 + dh]
                for j in range(taps))
        return y * _sigmoid(y)

    ri = lax.broadcasted_iota(jnp.int32, (chunk, chunk), 0)
    ci = lax.broadcasted_iota(jnp.int32, (chunk, chunk), 1)
    visible = ri <= ci
    gc = gc_ref[0]
    gr = gr_ref[0]
    g0 = g0_ref[...]
    g0_ref[...] = gc[chunk - 1:chunk, :]
    contract_last = (((1,), (1,)), ((), ()))
    contract_first = (((0,), (0,)), ((), ()))
    part_row = lax.broadcasted_iota(jnp.int32, (BF16_SUBLANES, dh), 0)
    slab_row = lax.broadcasted_iota(jnp.int32, (BF16_SUBLANES, chunk), 0)
    ones_dh = jnp.ones((dh, dh), BF16)

    slab = jnp.zeros((BF16_SUBLANES, chunk), F32)
    heads = []
    for h in range(ML_HEADS):
        hs = slice(h * dh, (h + 1) * dh)
        q = conv_silu(h * dh)
        k = conv_silu(width + h * dh) * dh ** -0.5
        qb, kb = q.astype(BF16), k.astype(BF16)
        vb = v_ref[0, :, hs]
        g0_h = g0[:, ML_HEADS + h:ML_HEADS + h + 1]
        key_col = gc[:, h:h + 1] - (gc[:, ML_HEADS + h:ML_HEADS + h + 1] - g0_h)
        li_r = gr[h:h + 1, :]
        b_r = gr[ML_HEADS + h:ML_HEADS + h + 1, :] - g0_h
        b_last = b_r[:, chunk - 1:chunk]
        m_prev = m_ref[h:h + 1, 0:1]

        d_t = jnp.where(visible, b_r + key_col, NEG_BIG)
        m_inter = b_r + m_prev
        m_t = jnp.maximum(m_inter, jnp.max(d_t, axis=0, keepdims=True))
        w_inter = jnp.exp(m_inter - m_t)
        p_t = jnp.exp(d_t - m_t) * lax.dot_general(kb, qb, contract_last, preferred_element_type=F32)
        n_hi, n_mid, n_lo = (part.astype(F32) for part in _split3(n_ref[h]))
        n_parts = jnp.where(part_row == 0, n_hi, jnp.where(part_row == 1, n_mid,
                                                           jnp.where(part_row == 2, n_lo, 0.0))).astype(BF16)
        nq = jnp.sum(lax.dot_general(n_parts, qb, contract_last, preferred_element_type=F32), axis=0, keepdims=True)
        den = w_inter * nq + jnp.sum(p_t, axis=0, keepdims=True)
        inv = 1.0 / jnp.maximum(jnp.abs(den), jnp.exp(-m_t))

        g_r = b_last - b_r + li_r
        m_new = jnp.maximum(b_last + m_prev, jnp.max(g_r, axis=-1, keepdims=True))
        decay = jnp.exp(b_last + m_prev - m_new)
        wk_r = jnp.exp(g_r - m_new)
        for slot, vec in enumerate((w_inter, inv, wk_r)):
            slab = jnp.where(slab_row == 3 * h + slot, vec, slab)
        heads.append((hs, k, qb, vb, p_t.astype(BF16), decay, m_new))

    cols = jnp.concatenate([slab, jnp.zeros((LANES - BF16_SUBLANES, chunk), F32)], axis=0).T

    for h, (hs, k, qb, vb, p_tb, decay, m_new) in enumerate(heads):
        w_col, inv_col, wk_col = (cols[:, 3 * h + slot:3 * h + slot + 1] for slot in range(3))
        num = (w_col * jnp.dot(qb, ct_ref[h].astype(BF16), preferred_element_type=F32)
               + lax.dot_general(p_tb, vb, contract_first, preferred_element_type=F32))
        hh = num * inv_col
        kw = k * wk_col
        ct_ref[h] = decay * ct_ref[h] + lax.dot_general(kw.astype(BF16), vb, contract_first,
                                                       preferred_element_type=F32)
        n_ref[h] = decay * n_ref[h] + jnp.sum(kw, axis=0, keepdims=True)
        m_ref[h:h + 1, :] = jnp.broadcast_to(m_new, (1, LANES))

        sq = hh * hh
        sq_hi = sq.astype(BF16)
        sq_lo = (sq - sq_hi.astype(F32)).astype(BF16)
        mean_sq = (jnp.dot(sq_hi, ones_dh, preferred_element_type=F32)
                   + jnp.dot(sq_lo, ones_dh, preferred_element_type=F32)) * (1.0 / dh)
        hn = hh * lax.rsqrt(mean_sq + NORM_EPS) * ng_ref[:, hs]
        out_ref[0, :, hs] = (hn * _sigmoid(og_ref[0, :, hs].astype(F32))).astype(BF16)


def _mlstm(p3, gcol, grow, conv_w, norm_g, *, qk_col, v_col, o_col, dh, chunk=256):
    b, s, _ = p3.shape
    width = ML_HEADS * dh
    pad = BF16_SUBLANES
    assert conv_w.shape[0] - 1 <= pad
    qkb, vb, ob = qk_col // (2 * width), v_col // width, o_col // width
    halo_per_chunk = chunk // pad
    return pl.pallas_call(
        functools.partial(_mlstm_kernel, chunk=chunk, dh=dh),
        grid=(b, s // chunk),
        in_specs=[pl.BlockSpec((1, chunk, 2 * width), lambda i, c: (i, c, qkb)),
                  pl.BlockSpec((1, pad, 2 * width),
                               lambda i, c: (i, jnp.maximum(c * halo_per_chunk - 1, 0), qkb)),
                  pl.BlockSpec((1, chunk, width), lambda i, c: (i, c, vb)),
                  pl.BlockSpec((1, chunk, width), lambda i, c: (i, c, ob)),
                  pl.BlockSpec((1, chunk, LANES), lambda i, c: (i, c, 0)),
                  pl.BlockSpec((1, grow.shape[1], chunk), lambda i, c: (i, 0, c)),
                  _resident(conv_w.shape), _resident((1, width))],
        out_specs=pl.BlockSpec((1, chunk, width), lambda i, c: (i, c, 0)),
        out_shape=jax.ShapeDtypeStruct((b, s, width), BF16),
        scratch_shapes=[pltpu.VMEM((pad + chunk, 2 * width), F32),
                        pltpu.VMEM((ML_HEADS, dh, dh), F32),
                        pltpu.VMEM((ML_HEADS, 1, dh), F32),
                        pltpu.VMEM((ML_HEADS, LANES), F32),
                        pltpu.VMEM((1, LANES), F32)],
        compiler_params=_cparams("parallel", "arbitrary"),
        name="mlstm",
    )(p3, p3, p3, p3, gcol, grow, conv_w, norm_g)


def _merge_kernel(u_ref, halo_ref, hm_ref, fo_ref, gp_ref, x_ref, wg_ref, ps_ref, wbp_ref, wbm_ref, wbf_ref,
                  wo_ref, o_ref, ubuf, *, tm):
    j = pl.program_id(1)
    pad = halo_ref.shape[1]
    gw = wg_ref.shape[1]
    d = x_ref.shape[2]
    ubuf[0:pad, :] = jnp.where(j == 0, 0.0, halo_ref[0].astype(F32))
    ubuf[pad:pad + tm, :] = u_ref[0].astype(F32)
    pos = j * tm + lax.broadcasted_iota(jnp.int32, (tm, 1), 0)
    ys = []
    for g, w in enumerate(POOL_WINDOWS):
        gs = slice(g * gw, (g + 1) * gw)
        wsum = sum(ubuf[pad - k:pad - k + tm, gs] for k in range(w))
        cnt = jnp.minimum(pos + 1, w).astype(F32)
        dlt = wsum / cnt - ubuf[pad:pad + tm, gs]
        ys.append(jnp.dot(dlt.astype(BF16), wg_ref[g], preferred_element_type=F32))
    y_pool = (jnp.concatenate(ys, axis=-1) * ps_ref[...]).astype(BF16)
    merged = _sigmoid(gp_ref[0, :, 0:d].astype(F32)) * jnp.dot(y_pool, wbp_ref[...], preferred_element_type=F32)
    merged += _sigmoid(gp_ref[0, :, d:2 * d].astype(F32)) * jnp.dot(hm_ref[0], wbm_ref[...],
                                                                   preferred_element_type=F32)
    merged += _sigmoid(gp_ref[0, :, 2 * d:3 * d].astype(F32)) * jnp.dot(fo_ref[0], wbf_ref[...],
                                                                       preferred_element_type=F32)
    o_ref[0] = x_ref[0] + jnp.dot(merged.astype(BF16), wo_ref[...], preferred_element_type=F32)


def _merge(p3, hm, fo, x3, wgrp, pscale, wbp, wbm, wbf, wo, *, pool_col, gate_col, tm=512):
    b, s, d = x3.shape
    pw = wbp.shape[0]
    pad = BF16_SUBLANES
    assert max(POOL_WINDOWS) - 1 <= pad and gate_col == 0
    pcb = pool_col // pw
    halo_per_blk = tm // pad
    return pl.pallas_call(
        functools.partial(_merge_kernel, tm=tm),
        grid=(b, s // tm),
        in_specs=[pl.BlockSpec((1, tm, pw), lambda i, j: (i, j, pcb)),
                  pl.BlockSpec((1, pad, pw), lambda i, j: (i, jnp.maximum(j * halo_per_blk - 1, 0), pcb)),
                  pl.BlockSpec((1, tm, hm.shape[2]), lambda i, j: (i, j, 0)),
                  pl.BlockSpec((1, tm, fo.shape[2]), lambda i, j: (i, j, 0)),
                  pl.BlockSpec((1, tm, N_BRANCH * d), lambda i, j: (i, j, 0)),
                  pl.BlockSpec((1, tm, d), lambda i, j: (i, j, 0)),
                  _resident(wgrp.shape), _resident(pscale.shape), _resident(wbp.shape),
                  _resident(wbm.shape), _resident(wbf.shape), _resident(wo.shape)],
        out_specs=pl.BlockSpec((1, tm, d), lambda i, j: (i, j, 0)),
        out_shape=jax.ShapeDtypeStruct((b, s, d), F32),
        scratch_shapes=[pltpu.VMEM((pad + tm, pw), F32)],
        compiler_params=_cparams("parallel", "parallel"),
        name="merge",
    )(p3, p3, hm, fo, p3, x3, wgrp, pscale, wbp, wbm, wbf, wo)


def _swiglu_kernel(x_ref, g_ref, wg_ref, wu_ref, wd_ref, o_ref, a_ref, *, ff_chunk):
    x = x_ref[...]
    h = _rmsnorm(x, g_ref[...]).astype(BF16)
    for c in range(wg_ref.shape[1] // ff_chunk):
        sl = slice(c * ff_chunk, (c + 1) * ff_chunk)
        gate = jnp.dot(h, wg_ref[:, sl], preferred_element_type=F32)
        up = jnp.dot(h, wu_ref[:, sl], preferred_element_type=F32)
        a_ref[:, sl] = (gate * _sigmoid(gate) * up).astype(BF16)
    o_ref[...] = x + jnp.dot(a_ref[...], wd_ref[...], preferred_element_type=F32)


def _swiglu(x2, g, wg, wu, wd, *, tm=512, ff_chunk=256):
    n, d = x2.shape
    ff = wg.shape[1]
    assert ff % ff_chunk == 0
    return pl.pallas_call(
        functools.partial(_swiglu_kernel, ff_chunk=ff_chunk),
        grid=(n // tm,),
        in_specs=[pl.BlockSpec((tm, d), lambda i: (i, 0)), _resident((1, d)),
                  _resident(wg.shape), _resident(wu.shape), _resident(wd.shape)],
        out_specs=pl.BlockSpec((tm, d), lambda i: (i, 0)),
        out_shape=jax.ShapeDtypeStruct((n, d), F32),
        scratch_shapes=[pltpu.VMEM((tm, ff), BF16)],
        compiler_params=_cparams("parallel"),
        name="dense_swiglu",
    )(x2, g, wg, wu, wd)


def _router_kernel(x_ref, g_ref, wr_ref, br_ref, e_ref, gt_ref, cnt_ref, carry_ref):
    h = _rmsnorm(x_ref[...], g_ref[...])
    h_hi = h.astype(BF16)
    h_lo = (h - h_hi.astype(F32)).astype(BF16)
    logits = jnp.dot(jnp.concatenate([h_hi, h_hi, h_lo], axis=-1), wr_ref[...],
                     preferred_element_type=F32) + br_ref[...]
    lane = lax.broadcasted_iota(jnp.int32, logits.shape, 1)
    m1 = jnp.max(logits, axis=-1, keepdims=True)
    i1 = jnp.min(jnp.where(logits == m1, lane, LANES), axis=-1, keepdims=True)
    rest = jnp.where(lane == i1, NEG_BIG, logits)
    m2 = jnp.max(rest, axis=-1, keepdims=True)
    i2 = jnp.min(jnp.where(rest == m2, lane, LANES), axis=-1, keepdims=True)
    e2 = jnp.exp(m2 - m1)
    g1 = 1.0 / (1.0 + e2)
    gt_ref[...] = jnp.where(lane == 0, g1, jnp.where(lane == 1, e2 * g1, 0.0))

    @pl.when(pl.program_id(0) == 0)
    def _():
        carry_ref[...] = jnp.zeros_like(carry_ref)

    tm = logits.shape[0]
    pick0, pick1 = lane == i1, lane == i2
    picks = jnp.where(pick0 | pick1, 1.0, 0.0)
    earlier = (lax.broadcasted_iota(jnp.int32, (tm, tm), 0) > lax.broadcasted_iota(jnp.int32, (tm, tm), 1))
    prior = jnp.dot(earlier.astype(BF16), picks.astype(BF16), preferred_element_type=F32) + carry_ref[...]
    r0 = jnp.sum(jnp.where(pick0, prior, 0.0), axis=-1, keepdims=True).astype(jnp.int32)
    r1 = jnp.sum(jnp.where(pick1, prior, 0.0), axis=-1, keepdims=True).astype(jnp.int32)
    carry_ref[...] = prior[tm - 1:tm, :] + picks[tm - 1:tm, :]
    cnt_ref[...] = jnp.broadcast_to(carry_ref[...], cnt_ref.shape).astype(jnp.int32)
    e_ref[...] = jnp.where(lane == 0, i1, jnp.where(lane == 1, i2, jnp.where(lane == 2, r0,
                                                                          jnp.where(lane == 3, r1, 0))))


def _router(x2, g, wr, br, *, tm=512):
    n, d = x2.shape
    return pl.pallas_call(
        _router_kernel,
        grid=(n // tm,),
        in_specs=[pl.BlockSpec((tm, d), lambda i: (i, 0)), _resident((1, d)),
                  _resident(wr.shape), _resident((1, LANES))],
        out_specs=[pl.BlockSpec((tm, LANES), lambda i: (i, 0)), pl.BlockSpec((tm, LANES), lambda i: (i, 0)),
                   pl.BlockSpec((8, LANES), lambda i: (0, 0))],
        out_shape=[jax.ShapeDtypeStruct((n, LANES), jnp.int32), jax.ShapeDtypeStruct((n, LANES), F32),
                   jax.ShapeDtypeStruct((8, LANES), jnp.int32)],
        scratch_shapes=[pltpu.VMEM((1, LANES), F32)],
        compiler_params=_cparams("arbitrary"),
        name="router",
    )(x2, g, wr, br)


def _dispatch_kernel(pend_ref, dest_ref, x_ref, xs_ref, zero_ref, sem, zsem):
    tm = dest_ref.shape[2] // TOP_K
    rows = zero_ref.shape[0]
    step = pl.program_id(0)

    @pl.when(step == 0)
    def _():
        zero_ref[...] = jnp.zeros_like(zero_ref)

        def zero_copy(e):
            start = pl.multiple_of(pend_ref[e] - rows, rows)
            return pltpu.make_async_copy(zero_ref, xs_ref.at[pl.ds(start, rows)], zsem)

        def has_rows(e):
            return pend_ref[e] > (pend_ref[e - 1] if e else 0)

        n_exp = pend_ref.shape[0]
        n_rows = xs_ref.shape[0]
        tail_starts = [n_rows - (j + 1) * rows for j in range(min(n_exp, n_rows // rows))]

        def tail_copy(start):
            return pltpu.make_async_copy(zero_ref, xs_ref.at[pl.ds(start, rows)], zsem)

        for e in range(n_exp):
            @pl.when(has_rows(e))
            def _():
                zero_copy(e).start()
        for start in tail_starts:
            @pl.when(start >= pend_ref[n_exp - 1])
            def _():
                tail_copy(start).start()
        for e in range(n_exp):
            @pl.when(has_rows(e))
            def _():
                zero_copy(e).wait()
        for start in tail_starts:
            @pl.when(start >= pend_ref[n_exp - 1])
            def _():
                tail_copy(start).wait()

    def row_copy(t, k):
        return pltpu.make_async_copy(x_ref.at[pl.ds(t, 1)], xs_ref.at[pl.ds(dest_ref[0, 0, TOP_K * t + k], 1)], sem)

    def start(t, carry):
        for k in range(TOP_K):
            row_copy(t, k).start()
        return carry

    def wait(t, carry):
        for k in range(TOP_K):
            row_copy(t, k).wait()
        return carry

    lax.fori_loop(0, tm, start, 0, unroll=True)
    lax.fori_loop(0, tm, wait, 0, unroll=8)


def _dispatch(pend, dest2, x2, *, n_rows, rows, tm=256):
    n, d = x2.shape
    grid_spec = pltpu.PrefetchScalarGridSpec(
        num_scalar_prefetch=1,
        grid=(n // tm,),
        in_specs=[pl.BlockSpec((1, 1, TOP_K * tm), lambda i, pe: (i, 0, 0), memory_space=pltpu.SMEM),
                  pl.BlockSpec((tm, d), lambda i, pe: (i, 0))],
        out_specs=pl.BlockSpec(memory_space=pl.ANY),
        scratch_shapes=[pltpu.VMEM((rows, d), F32), pltpu.SemaphoreType.DMA(()), pltpu.SemaphoreType.DMA(())],
    )
    return pl.pallas_call(
        _dispatch_kernel,
        grid_spec=grid_spec,
        out_shape=jax.ShapeDtypeStruct((n_rows, d), F32),
        compiler_params=_cparams("arbitrary"),
        name="moe_dispatch",
    )(pend, dest2, x2)


def _experts_kernel(blk_e_ref, nact_ref, xs_ref, g_ref, wg_ref, wu_ref, wd_ref, ys_ref, h_ref, a_ref, acc_ref,
                    *, ff_chunk):
    del blk_e_ref
    i = pl.program_id(0)
    f = pl.program_id(1)

    @pl.when(i < nact_ref[0])
    def _():
        @pl.when(f == 0)
        def _():
            h_ref[...] = _rmsnorm(xs_ref[...], g_ref[...]).astype(BF16)
            acc_ref[...] = jnp.zeros_like(acc_ref)

        h = h_ref[...]
        for c in range(wg_ref.shape[2] // ff_chunk):
            sl = slice(c * ff_chunk, (c + 1) * ff_chunk)
            gate = jnp.dot(h, wg_ref[0, :, sl], preferred_element_type=F32)
            up = jnp.dot(h, wu_ref[0, :, sl], preferred_element_type=F32)
            a_ref[:, sl] = (gate * _sigmoid(gate) * up).astype(BF16)
        acc_ref[...] += jnp.dot(a_ref[...], wd_ref[0], preferred_element_type=F32)

        @pl.when(f == pl.num_programs(1) - 1)
        def _():
            ys_ref[...] = acc_ref[...]

    @pl.when((i >= nact_ref[0]) & (f == 0))
    def _():
        ys_ref[...] = jnp.zeros_like(ys_ref)


def _experts(blk_e, nact, xs, g, wg, wu, wd, *, rows, tf=1792, ff_chunk=256):
    n_rows, d = xs.shape
    ff = wg.shape[2]
    nf = ff // tf
    assert ff % tf == 0 and tf % ff_chunk == 0 and n_rows % rows == 0

    def blk(i, nact):
        return jnp.minimum(i, nact[0] - 1)

    def ff_tile(i, f, nact):
        return jnp.where(i < nact[0], f, nf - 1)

    grid_spec = pltpu.PrefetchScalarGridSpec(
        num_scalar_prefetch=2,
        grid=(n_rows // rows, nf),
        in_specs=[pl.BlockSpec((rows, d), lambda i, f, be, na: (blk(i, na), 0)),
                  pl.BlockSpec((1, d), lambda i, f, be, na: (0, 0)),
                  pl.BlockSpec((1, d, tf), lambda i, f, be, na: (be[blk(i, na)], 0, ff_tile(i, f, na))),
                  pl.BlockSpec((1, d, tf), lambda i, f, be, na: (be[blk(i, na)], 0, ff_tile(i, f, na))),
                  pl.BlockSpec((1, tf, d), lambda i, f, be, na: (be[blk(i, na)], ff_tile(i, f, na), 0))],
        out_specs=pl.BlockSpec((rows, d), lambda i, f, be, na: (i, 0)),
        scratch_shapes=[pltpu.VMEM((rows, d), BF16), pltpu.VMEM((rows, tf), BF16), pltpu.VMEM((rows, d), F32)],
    )
    return pl.pallas_call(
        functools.partial(_experts_kernel, ff_chunk=ff_chunk),
        grid_spec=grid_spec,
        out_shape=jax.ShapeDtypeStruct((n_rows, d), F32),
        compiler_params=_cparams("arbitrary", "arbitrary"),
        name="moe_experts",
    )(blk_e, nact, xs, g, wg, wu, wd)


def _combine_kernel(dest_ref, next_dest_ref, x_ref, gt_ref, fg_ref, ys_ref, o_ref, buf, sems, *, final_norm):
    tm = x_ref.shape[0]
    step = pl.program_id(0)
    slot = step % 2

    def row_copy(dref, s, t, k):
        return pltpu.make_async_copy(ys_ref.at[pl.ds(dref[0, 0, TOP_K * t + k], 1)], buf.at[s, k, pl.ds(t, 1)],
                                     sems.at[s])

    def gather(dref, s):
        def start(t, carry):
            for k in range(TOP_K):
                row_copy(dref, s, t, k).start()
            return carry
        lax.fori_loop(0, tm, start, 0, unroll=True)

    @pl.when(step == 0)
    def _():
        gather(dest_ref, 0)

    @pl.when(step + 1 < pl.num_programs(0))
    def _():
        gather(next_dest_ref, 1 - slot)

    def wait(t, carry):
        for k in range(TOP_K):
            row_copy(dest_ref, slot, t, k).wait()
        return carry

    lax.fori_loop(0, tm, wait, 0, unroll=8)
    gt = gt_ref[...]
    y = x_ref[...]
    for k in range(TOP_K):
        y = y + buf[slot, k] * gt[:, k:k + 1]
    o_ref[...] = _rmsnorm(y, fg_ref[...]) if final_norm else y


def _combine(dest2, x2, gates, fg, ys, *, final_norm, tm=256):
    n, d = x2.shape
    steps = n // tm
    return pl.pallas_call(
        functools.partial(_combine_kernel, final_norm=final_norm),
        grid=(steps,),
        in_specs=[pl.BlockSpec((1, 1, TOP_K * tm), lambda i: (i, 0, 0), memory_space=pltpu.SMEM),
                  pl.BlockSpec((1, 1, TOP_K * tm), lambda i: (jnp.minimum(i + 1, steps - 1), 0, 0),
                               memory_space=pltpu.SMEM),
                  pl.BlockSpec((tm, d), lambda i: (i, 0)),
                  pl.BlockSpec((tm, LANES), lambda i: (i, 0)),
                  _resident((1, d)),
                  pl.BlockSpec(memory_space=pl.ANY)],
        out_specs=pl.BlockSpec((tm, d), lambda i: (i, 0)),
        out_shape=jax.ShapeDtypeStruct((n, d), F32),
        scratch_shapes=[pltpu.VMEM((2, TOP_K, tm, d), F32), pltpu.SemaphoreType.DMA((2,))],
        compiler_params=_cparams("arbitrary"),
        name="moe_combine",
    )(dest2, dest2, x2, gates, fg, ys)


def _final_norm_kernel(x_ref, g_ref, o_ref):
    o_ref[...] = _rmsnorm(x_ref[...], g_ref[...])


def _final_norm(x2, g, *, tm=512):
    n, d = x2.shape
    return pl.pallas_call(
        _final_norm_kernel,
        grid=(n // tm,),
        in_specs=[pl.BlockSpec((tm, d), lambda i: (i, 0)), _resident((1, d))],
        out_specs=pl.BlockSpec((tm, d), lambda i: (i, 0)),
        out_shape=jax.ShapeDtypeStruct((n, d), F32),
        compiler_params=_cparams("parallel"),
        name="final_norm",
    )(x2, g)


def _token_mixing(x3, norm_g, w_in, pool_w_grp, pool_scale, conv_w, b_i, b_f, ml_norm_g, fox_b_f,
                  w_br_pool, w_br_ml, w_br_fox, w_out):
    b, s, d = x3.shape
    pool_w = w_br_pool.shape[0]
    ml_w = w_br_ml.shape[0]
    fox_w = w_br_fox.shape[0]
    ml_dh = ml_w // ML_HEADS
    fox_dh = fox_w // FOX_HEADS
    n_small = 2 * ML_HEADS + FOX_HEADS

    sizes = (pool_w, ml_w, ml_w, ml_w, ml_w, ML_HEADS, ML_HEADS, fox_w, fox_w, fox_w, FOX_HEADS, N_BRANCH * d)
    offs = [0]
    for sz in sizes:
        offs.append(offs[-1] + sz)
    assert offs[-1] == w_in.shape[1]
    (o_pool, o_q, o_k, o_v, o_o, o_i, o_f, o_fq, o_fk, o_fv, o_ff, o_g) = offs[:-1]

    def cols(o, sz):
        return w_in[:, o:o + sz]

    wm = jnp.concatenate([cols(o_g, N_BRANCH * d), cols(o_q, ml_w), cols(o_k, ml_w), cols(o_v, ml_w),
                          cols(o_o, ml_w), cols(o_pool, pool_w), cols(o_fq, fox_w), cols(o_fk, fox_w),
                          cols(o_fv, fox_w)], axis=1).astype(BF16)
    c_gate = 0
    c_qk = c_gate + N_BRANCH * d
    c_v = c_qk + 2 * ml_w
    c_o = c_v + ml_w
    c_pool = c_o + ml_w
    c_fq = c_pool + pool_w
    c_fk = c_fq + fox_w
    c_fv = c_fk + fox_w
    ws = jnp.concatenate([cols(o_i, ML_HEADS), cols(o_f, ML_HEADS), cols(o_ff, FOX_HEADS),
                          jnp.zeros((d, LANES - n_small), w_in.dtype)], axis=1).astype(BF16)
    bias = jnp.concatenate([b_i, b_f, fox_b_f, jnp.zeros((LANES - n_small,), F32)]).reshape(1, LANES)

    proj, small = _inproj(x3.reshape(b * s, d), norm_g.reshape(1, d), wm, ws)
    p3 = proj.reshape(b, s, -1)
    gcol, grow, aq, ak = _gateprep(small.reshape(b, s, LANES), bias, fox_dh)
    fo = _fox(p3, aq, ak, q_col=c_fq, k_col=c_fk, v_col=c_fv, dh=fox_dh)
    hm = _mlstm(p3, gcol, grow, conv_w, ml_norm_g.reshape(1, ml_w), qk_col=c_qk, v_col=c_v, o_col=c_o, dh=ml_dh)
    return _merge(p3, hm, fo, x3, pool_w_grp.astype(BF16), pool_scale.reshape(1, pool_w),
                  w_br_pool.astype(BF16), w_br_ml.astype(BF16), w_br_fox.astype(BF16), w_out.astype(BF16),
                  pool_col=c_pool, gate_col=c_gate)


def _moe(x2, norm_g, w_router, b_router, w_gate, w_up, w_down, final_g, *, rows=512):
    n, d = x2.shape
    n_exp = w_router.shape[1]
    wr = jnp.concatenate([w_router.astype(F32), jnp.zeros((d, LANES - n_exp), F32)], axis=1)
    wr_hi = wr.astype(BF16)
    wr_lo = (wr - wr_hi.astype(F32)).astype(BF16)
    br = jnp.concatenate([b_router.astype(F32), jnp.full((LANES - n_exp,), NEG_BIG, F32)]).reshape(1, LANES)
    routed, gates, cnt = _router(x2, norm_g.reshape(1, d), jnp.concatenate([wr_hi, wr_lo, wr_hi], axis=0), br)

    e_flat = routed[:, :TOP_K].reshape(n * TOP_K)
    rank = routed[:, TOP_K:2 * TOP_K].reshape(n * TOP_K)
    counts = cnt[0, :n_exp]
    padded = ((counts + rows - 1) // rows) * rows
    pend = jnp.cumsum(padded)
    onehot = e_flat[:, None] == jnp.arange(n_exp, dtype=jnp.int32)[None, :]
    dest = (jnp.sum(jnp.where(onehot, (pend - padded)[None, :], 0), axis=1) + rank).astype(jnp.int32)
    n_blk = (n * TOP_K + n_exp * (rows - 1) + rows - 1) // rows
    blk_start = jnp.arange(n_blk, dtype=jnp.int32) * rows
    blk_e = jnp.minimum(jnp.sum(blk_start[:, None] >= pend[None, :], axis=1), n_exp - 1).astype(jnp.int32)
    nact = (pend[-1:] // rows).astype(jnp.int32)

    tm = 256
    dest2 = dest.reshape(n // tm, 1, TOP_K * tm)
    xs = _dispatch(pend.astype(jnp.int32), dest2, x2, n_rows=n_blk * rows, rows=rows, tm=tm)
    ys = _experts(blk_e, nact, xs, norm_g.reshape(1, d), w_gate.astype(BF16), w_up.astype(BF16),
                  w_down.astype(BF16), rows=rows)
    fg = jnp.ones((1, d), F32) if final_g is None else final_g.reshape(1, d)
    return _combine(dest2, x2, gates, fg, ys, final_norm=final_g is not None, tm=tm)


def kernel(x, mix_norm_g, w_in, pool_w_grp, pool_scale, ml_conv_w, ml_b_i, ml_b_f, ml_norm_g, fox_b_f,
           w_br_pool, w_br_ml, w_br_fox, w_out, ffn_norm_g, ff_w_gate, ff_w_up, ff_w_down,
           moe_w_router, moe_b_router, moe_w_gate, moe_w_up, moe_w_down, final_norm_g):
    b, s, d = x.shape
    depth = mix_norm_g.shape[0]
    fused_final = False
    for l in range(depth):
        x = _token_mixing(x, mix_norm_g[l], w_in[l], pool_w_grp[l], pool_scale[l], ml_conv_w[l], ml_b_i[l],
                          ml_b_f[l], ml_norm_g[l], fox_b_f[l], w_br_pool[l], w_br_ml[l], w_br_fox[l], w_out[l])
        x2 = x.reshape(b * s, d)
        if l % 2 == 0:
            x2 = _swiglu(x2, ffn_norm_g[l].reshape(1, d), ff_w_gate[l // 2].astype(BF16),
                         ff_w_up[l // 2].astype(BF16), ff_w_down[l // 2].astype(BF16))
        else:
            fused_final = l == depth - 1
            x2 = _moe(x2, ffn_norm_g[l], moe_w_router[l // 2], moe_b_router[l // 2], moe_w_gate[l // 2],
                      moe_w_up[l // 2], moe_w_down[l // 2], final_norm_g if fused_final else None)
        x = x2.reshape(b, s, d)
    if not fused_final:
        x = _final_norm(x.reshape(b * s, d), final_norm_g.reshape(1, d)).reshape(b, s, d)
    return x
```

```python
import functools
import math

import jax
import jax.numpy as jnp
import numpy as np
from jax import lax
from jax.experimental import pallas as pl
from jax.experimental.pallas import tpu as pltpu

F32 = jnp.float32
BF16 = jnp.bfloat16

NORM_EPS = 1e-6
POOL_WINDOWS = (2, 4, 8, 16)
ML_HEADS = 4
FOX_HEADS = 8
TOP_K = 2
N_BRANCH = 3

LANES = 128
BF16_SUBLANES = 16
VMEM_LIMIT_BYTES = 56 * 1024 * 1024

NEG_BIG = -1e30


def _cparams(*sem):
    return pltpu.CompilerParams(dimension_semantics=sem, vmem_limit_bytes=VMEM_LIMIT_BYTES)


def _resident(shape):
    zeros = (0,) * len(shape)
    return pl.BlockSpec(shape, lambda *_: zeros, pipeline_mode=pl.Buffered(1))


def _rmsnorm(x, g):
    return x * lax.rsqrt(jnp.mean(x * x, axis=-1, keepdims=True) + NORM_EPS) * g


def _sigmoid(x):
    return 1.0 / (1.0 + jnp.exp2(x * (-math.log2(math.e))))


def _log_sigmoid(x):
    return jnp.minimum(x, 0.0) - jnp.log(1.0 + jnp.exp(-jnp.abs(x)))


def _split3(x):
    hi = x.astype(BF16)
    r = x - hi.astype(F32)
    mid = r.astype(BF16)
    lo = (r - mid.astype(F32)).astype(BF16)
    return hi, mid, lo


def _cumsum_rows(tril, x):
    return sum(jnp.dot(tril, part, preferred_element_type=F32) for part in _split3(x))


def _inproj_kernel(x_ref, g_ref, wm_ref, ws_ref, p_ref, s_ref, *, col_chunk):
    h = _rmsnorm(x_ref[...], g_ref[...]).astype(BF16)
    s_ref[...] = jnp.dot(h, ws_ref[...], preferred_element_type=F32)
    for c in range(wm_ref.shape[1] // col_chunk):
        sl = slice(c * col_chunk, (c + 1) * col_chunk)
        p_ref[:, sl] = jnp.dot(h, wm_ref[:, sl], preferred_element_type=F32).astype(BF16)


def _inproj(x2, g, wm, ws, *, tm=512, col_chunk=1024):
    n, d = x2.shape
    wcols = wm.shape[1]
    return pl.pallas_call(
        functools.partial(_inproj_kernel, col_chunk=col_chunk),
        grid=(n // tm,),
        in_specs=[pl.BlockSpec((tm, d), lambda i: (i, 0)),
                  _resident((1, d)), _resident((d, wcols)), _resident((d, LANES))],
        out_specs=[pl.BlockSpec((tm, wcols), lambda i: (i, 0)),
                   pl.BlockSpec((tm, LANES), lambda i: (i, 0))],
        out_shape=[jax.ShapeDtypeStruct((n, wcols), BF16), jax.ShapeDtypeStruct((n, LANES), F32)],
        compiler_params=_cparams("parallel"),
        name="inproj",
    )(x2, g, wm, ws)


AUG_TERMS = 3
AUG_STRIDE = 8


def _aug_placement(dh):
    pairs = FOX_HEADS * dh // LANES
    width = pairs * LANES
    pq = np.zeros((AUG_TERMS * LANES, width), np.float32)
    pk = np.zeros((AUG_TERMS * LANES, width), np.float32)
    cq = np.zeros((1, width), np.float32)
    ck = np.zeros((1, width), np.float32)
    for h in range(FOX_HEADS):
        base = (h // 2) * LANES + (h % 2) * AUG_STRIDE
        for t in range(AUG_TERMS):
            src = t * LANES + 2 * ML_HEADS + h
            pq[src, base + t] = 1.0
            pk[src, base + AUG_TERMS + t] = -1.0
            cq[0, base + AUG_TERMS + t] = 1.0
            ck[0, base + t] = 1.0
    return jnp.asarray(pq, BF16), jnp.asarray(pk, BF16), jnp.asarray(cq), jnp.asarray(ck)


def _gateprep_kernel(s_ref, b_ref, pq_ref, pk_ref, cq_ref, ck_ref, col_ref, row_ref, aq_ref, ak_ref, carry_ref):
    t = s_ref.shape[1]

    @pl.when(pl.program_id(1) == 0)
    def _():
        carry_ref[...] = jnp.zeros_like(carry_ref)

    pre = s_ref[0] + b_ref[...]
    lane = lax.broadcasted_iota(jnp.int32, pre.shape, 1)
    ls = _log_sigmoid(pre)
    forget = (lane >= ML_HEADS) & (lane < 2 * ML_HEADS + FOX_HEADS)
    tril = (lax.broadcasted_iota(jnp.int32, (t, t), 0) >= lax.broadcasted_iota(jnp.int32, (t, t), 1)).astype(BF16)
    fcum = _cumsum_rows(tril, jnp.where(forget, ls, 0.0)) + carry_ref[...]
    carry_ref[...] = fcum[t - 1:t, :]
    col = jnp.where(lane < ML_HEADS, pre, fcum)
    col_ref[0] = col
    row_ref[0] = col.T[0:row_ref.shape[1], :]
    parts = jnp.concatenate(_split3(fcum), axis=-1)
    aq_ref[0] = (jnp.dot(parts, pq_ref[...], preferred_element_type=F32) + cq_ref[...]).astype(BF16)
    ak_ref[0] = (jnp.dot(parts, pk_ref[...], preferred_element_type=F32) + ck_ref[...]).astype(BF16)


def _gateprep(s3, bias, dh, *, tg=512):
    b, s, _ = s3.shape
    n_rows = 2 * ML_HEADS + FOX_HEADS
    pq, pk, cq, ck = _aug_placement(dh)
    width = pq.shape[1]
    return pl.pallas_call(
        _gateprep_kernel,
        grid=(b, s // tg),
        in_specs=[pl.BlockSpec((1, tg, LANES), lambda i, j: (i, j, 0)), _resident((1, LANES)),
                  _resident(pq.shape), _resident(pk.shape), _resident(cq.shape), _resident(ck.shape)],
        out_specs=[pl.BlockSpec((1, tg, LANES), lambda i, j: (i, j, 0)),
                   pl.BlockSpec((1, n_rows, tg), lambda i, j: (i, 0, j)),
                   pl.BlockSpec((1, tg, width), lambda i, j: (i, j, 0)),
                   pl.BlockSpec((1, tg, width), lambda i, j: (i, j, 0))],
        out_shape=[jax.ShapeDtypeStruct((b, s, LANES), F32), jax.ShapeDtypeStruct((b, n_rows, s), F32),
                   jax.ShapeDtypeStruct((b, s, width), BF16), jax.ShapeDtypeStruct((b, s, width), BF16)],
        scratch_shapes=[pltpu.VMEM((1, LANES), F32)],
        compiler_params=_cparams("parallel", "arbitrary"),
        name="gateprep",
    )(s3, bias, pq, pk, cq, ck)


def _fox_kernel(q_ref, aq_ref, k_ref, ak_ref, v_ref, o_ref, m_ref, acc_ref, *, tq, tk, diag_strips, dh, scale):
    qi = pl.program_id(2)
    lane = lax.broadcasted_iota(jnp.int32, (tq, LANES), 1)
    q2 = q_ref[0] * scale
    aq = aq_ref[0]
    in_head = [(lane >= a * dh) & (lane < (a + 1) * dh) for a in range(2)]
    q_ops = []
    for a in range(2):
        in_aug = (lane >= a * AUG_STRIDE) & (lane < a * AUG_STRIDE + 2 * AUG_TERMS)
        q_ops.append(jnp.concatenate([jnp.where(in_head[a], q2, jnp.zeros_like(q2)),
                                      jnp.where(in_aug, aq, jnp.zeros_like(aq))], axis=-1))
    m_ref[...] = jnp.full_like(m_ref, NEG_BIG)
    acc_ref[...] = jnp.zeros_like(acc_ref)
    def step(start, width, row0=0, masked=False):
        rows = slice(row0, tq)
        kb = jnp.concatenate([k_ref[0, pl.ds(start, width), :], ak_ref[0, pl.ds(start, width), :]], axis=-1)
        vb = v_ref[0, pl.ds(start, width), :]
        key_lane = lax.broadcasted_iota(jnp.int32, (width, LANES), 1)
        key_head = [(key_lane >= a * dh) & (key_lane < (a + 1) * dh) for a in range(2)]
        if masked:
            causal = (lax.broadcasted_iota(jnp.int32, (tq - row0, width), 1)
                      <= lax.broadcasted_iota(jnp.int32, (tq - row0, width), 0))
        for a in range(2):
            s = lax.dot_general(q_ops[a][rows], kb, (((1,), (1,)), ((), ())), preferred_element_type=F32)
            if masked:
                s = jnp.where(causal, s, NEG_BIG)
            m_prev = m_ref[a, rows]
            m_new = jnp.maximum(m_prev, jnp.max(s, axis=-1, keepdims=True))
            alpha = jnp.exp(m_prev - m_new)
            pexp = jnp.exp(s - jnp.concatenate([m_new] * (width // LANES), axis=-1))
            v_op = jnp.where(key_head[a], vb, jnp.ones_like(vb))
            acc_ref[a, rows] = alpha * acc_ref[a, rows] + jnp.dot(pexp.astype(BF16), v_op,
                                                                  preferred_element_type=F32)
            m_ref[a, rows] = m_new

    n_wide = (qi * tq) // tk
    n_narrow = qi - n_wide * (tk // tq)

    def wide_body(j, carry):
        step(pl.multiple_of(j * tk, tk), tk)
        return carry

    def narrow_body(j, carry):
        step(pl.multiple_of((n_wide * (tk // tq) + j) * tq, tq), tq)
        return carry

    lax.fori_loop(0, n_wide, wide_body, 0)
    if tk != tq:
        lax.fori_loop(0, n_narrow, narrow_body, 0)
    strip = tq // diag_strips
    for c in range(diag_strips):
        step(pl.multiple_of(qi * tq + c * strip, strip), strip, row0=c * strip, masked=True)
    outs = [acc_ref[a] / pltpu.roll(acc_ref[a], LANES // 2, 1) for a in range(2)]
    o_ref[0] = jnp.where(in_head[0], outs[0], outs[1]).astype(BF16)


def _fox(p3, aq, ak, *, q_col, k_col, v_col, dh, tq=1024, tk=1024, diag_strips=2):
    b, s, _ = p3.shape
    assert 2 * dh == LANES, "two heads share one 128-lane block"
    assert tk % tq == 0 and s % tq == 0
    pairs = FOX_HEADS * dh // LANES
    scale = dh ** -0.5
    assert math.frexp(scale)[0] == 0.5, "score scale is folded into bf16 q; exact only for a power of two"
    qb, kb, vb = q_col // LANES, k_col // LANES, v_col // LANES
    return pl.pallas_call(
        functools.partial(_fox_kernel, tq=tq, tk=tk, diag_strips=diag_strips, dh=dh, scale=scale),
        grid=(b, pairs, s // tq),
        in_specs=[pl.BlockSpec((1, tq, LANES), lambda i, p, q: (i, q, qb + p)),
                  pl.BlockSpec((1, tq, LANES), lambda i, p, q: (i, q, p)),
                  pl.BlockSpec((1, s, LANES), lambda i, p, q: (i, 0, kb + p)),
                  pl.BlockSpec((1, s, LANES), lambda i, p, q: (i, 0, p)),
                  pl.BlockSpec((1, s, LANES), lambda i, p, q: (i, 0, vb + p))],
        out_specs=pl.BlockSpec((1, tq, LANES), lambda i, p, q: (i, q, p)),
        out_shape=jax.ShapeDtypeStruct((b, s, pairs * LANES), BF16),
        scratch_shapes=[pltpu.VMEM((2, tq, LANES), F32), pltpu.VMEM((2, tq, LANES), F32)],
        compiler_params=_cparams("parallel", "parallel", "arbitrary"),
        name="fox_attention",
    )(p3, aq, p3, ak, p3)


def _mlstm_kernel(qk_ref, halo_ref, v_ref, og_ref, gc_ref, gr_ref, cw_ref, ng_ref, out_ref,
                  ubuf, ct_ref, n_ref, m_ref, g0_ref, *, chunk, dh):
    c = pl.program_id(1)
    width = ML_HEADS * dh
    taps = cw_ref.shape[0]
    pad = halo_ref.shape[1]
    assert 3 * ML_HEADS <= BF16_SUBLANES

    @pl.when(c == 0)
    def _():
        ct_ref[...] = jnp.zeros_like(ct_ref)
        n_ref[...] = jnp.zeros_like(n_ref)
        m_ref[...] = jnp.zeros_like(m_ref)
        g0_ref[...] = jnp.zeros_like(g0_ref)

    ubuf[0:pad, :] = jnp.where(c == 0, 0.0, halo_ref[0].astype(F32))
    ubuf[pad:pad + chunk, :] = qk_ref[0].astype(F32)

    def conv_silu(col0):
        y = sum(ubuf[pad - taps + 1 + j:pad - taps + 1 + j + chunk, col0:col0 + dh] * cw_ref[j:j + 1, col0:col0 + dh]
                for j in range(taps))
        return y * _sigmoid(y)

    ri = lax.broadcasted_iota(jnp.int32, (chunk, chunk), 0)
    ci = lax.broadcasted_iota(jnp.int32, (chunk, chunk), 1)
    visible = ri <= ci
    gc = gc_ref[0]
    gr = gr_ref[0]
    g0 = g0_ref[...]
    g0_ref[...] = gc[chunk - 1:chunk, :]
    contract_last = (((1,), (1,)), ((), ()))
    contract_first = (((0,), (0,)), ((), ()))
    part_row = lax.broadcasted_iota(jnp.int32, (BF16_SUBLANES, dh), 0)
    slab_row = lax.broadcasted_iota(jnp.int32, (BF16_SUBLANES, chunk), 0)
    ones_dh = jnp.ones((dh, dh), BF16)

    slab = jnp.zeros((BF16_SUBLANES, chunk), F32)
    heads = []
    for h in range(ML_HEADS):
        hs = slice(h * dh, (h + 1) * dh)
        q = conv_silu(h * dh)
        k = conv_silu(width + h * dh) * dh ** -0.5
        qb, kb = q.astype(BF16), k.astype(BF16)
        vb = v_ref[0, :, hs]
        g0_h = g0[:, ML_HEADS + h:ML_HEADS + h + 1]
        key_col = gc[:, h:h + 1] - (gc[:, ML_HEADS + h:ML_HEADS + h + 1] - g0_h)
        li_r = gr[h:h + 1, :]
        b_r = gr[ML_HEADS + h:ML_HEADS + h + 1, :] - g0_h
        b_last = b_r[:, chunk - 1:chunk]
        m_prev = m_ref[h:h + 1, 0:1]

        d_t = jnp.where(visible, b_r + key_col, NEG_BIG)
        m_inter = b_r + m_prev
        m_t = jnp.maximum(m_inter, jnp.max(d_t, axis=0, keepdims=True))
        w_inter = jnp.exp(m_inter - m_t)
        p_t = jnp.exp(d_t - m_t) * lax.dot_general(kb, qb, contract_last, preferred_element_type=F32)
        n_hi, n_mid, n_lo = (part.astype(F32) for part in _split3(n_ref[h]))
        n_parts = jnp.where(part_row == 0, n_hi, jnp.where(part_row == 1, n_mid,
                                                           jnp.where(part_row == 2, n_lo, 0.0))).astype(BF16)
        nq = jnp.sum(lax.dot_general(n_parts, qb, contract_last, preferred_element_type=F32), axis=0, keepdims=True)
        den = w_inter * nq + jnp.sum(p_t, axis=0, keepdims=True)
        inv = 1.0 / jnp.maximum(jnp.abs(den), jnp.exp(-m_t))

        g_r = b_last - b_r + li_r
        m_new = jnp.maximum(b_last + m_prev, jnp.max(g_r, axis=-1, keepdims=True))
        decay = jnp.exp(b_last + m_prev - m_new)
        wk_r = jnp.exp(g_r - m_new)
        for slot, vec in enumerate((w_inter, inv, wk_r)):
            slab = jnp.where(slab_row == 3 * h + slot, vec, slab)
        heads.append((hs, k, qb, vb, p_t.astype(BF16), decay, m_new))

    cols = jnp.concatenate([slab, jnp.zeros((LANES - BF16_SUBLANES, chunk), F32)], axis=0).T

    for h, (hs, k, qb, vb, p_tb, decay, m_new) in enumerate(heads):
        w_col, inv_col, wk_col = (cols[:, 3 * h + slot:3 * h + slot + 1] for slot in range(3))
        num = (w_col * jnp.dot(qb, ct_ref[h].astype(BF16), preferred_element_type=F32)
               + lax.dot_general(p_tb, vb, contract_first, preferred_element_type=F32))
        hh = num * inv_col
        kw = k * wk_col
        ct_ref[h] = decay * ct_ref[h] + lax.dot_general(kw.astype(BF16), vb, contract_first,
                                                       preferred_element_type=F32)
        n_ref[h] = decay * n_ref[h] + jnp.sum(kw, axis=0, keepdims=True)
        m_ref[h:h + 1, :] = jnp.broadcast_to(m_new, (1, LANES))

        sq = hh * hh
        sq_hi = sq.astype(BF16)
        sq_lo = (sq - sq_hi.astype(F32)).astype(BF16)
        mean_sq = (jnp.dot(sq_hi, ones_dh, preferred_element_type=F32)
                   + jnp.dot(sq_lo, ones_dh, preferred_element_type=F32)) * (1.0 / dh)
        hn = hh * lax.rsqrt(mean_sq + NORM_EPS) * ng_ref[:, hs]
        out_ref[0, :, hs] = (hn * _sigmoid(og_ref[0, :, hs].astype(F32))).astype(BF16)


def _mlstm(p3, gcol, grow, conv_w, norm_g, *, qk_col, v_col, o_col, dh, chunk=256):
    b, s, _ = p3.shape
    width = ML_HEADS * dh
    pad = BF16_SUBLANES
    assert conv_w.shape[0] - 1 <= pad
    qkb, vb, ob = qk_col // (2 * width), v_col // width, o_col // width
    halo_per_chunk = chunk // pad
    return pl.pallas_call(
        functools.partial(_mlstm_kernel, chunk=chunk, dh=dh),
        grid=(b, s // chunk),
        in_specs=[pl.BlockSpec((1, chunk, 2 * width), lambda i, c: (i, c, qkb)),
                  pl.BlockSpec((1, pad, 2 * width),
                               lambda i, c: (i, jnp.maximum(c * halo_per_chunk - 1, 0), qkb)),
                  pl.BlockSpec((1, chunk, width), lambda i, c: (i, c, vb)),
                  pl.BlockSpec((1, chunk, width), lambda i, c: (i, c, ob)),
                  pl.BlockSpec((1, chunk, LANES), lambda i, c: (i, c, 0)),
                  pl.BlockSpec((1, grow.shape[1], chunk), lambda i, c: (i, 0, c)),
                  _resident(conv_w.shape), _resident((1, width))],
        out_specs=pl.BlockSpec((1, chunk, width), lambda i, c: (i, c, 0)),
        out_shape=jax.ShapeDtypeStruct((b, s, width), BF16),
        scratch_shapes=[pltpu.VMEM((pad + chunk, 2 * width), F32),
                        pltpu.VMEM((ML_HEADS, dh, dh), F32),
                        pltpu.VMEM((ML_HEADS, 1, dh), F32),
                        pltpu.VMEM((ML_HEADS, LANES), F32),
                        pltpu.VMEM((1, LANES), F32)],
        compiler_params=_cparams("parallel", "arbitrary"),
        name="mlstm",
    )(p3, p3, p3, p3, gcol, grow, conv_w, norm_g)


def _merge_kernel(u_ref, halo_ref, hm_ref, fo_ref, gp_ref, x_ref, wg_ref, ps_ref, wbp_ref, wbm_ref, wbf_ref,
                  wo_ref, o_ref, ubuf, sbuf, *, tm):
    j = pl.program_id(1)
    pad = halo_ref.shape[1]
    gw = wg_ref.shape[1]
    pw = ubuf.shape[1]
    d = x_ref.shape[2]
    margin = ubuf.shape[0] - pad - tm
    base = margin + pad
    ext = pad + tm
    levels = len(POOL_WINDOWS)
    assert all(w == 2 ** (g + 1) for g, w in enumerate(POOL_WINDOWS)) and 2 ** (levels - 1) <= margin

    ubuf[0:margin, :] = jnp.zeros((margin, pw), F32)
    ubuf[margin:base, :] = jnp.where(j == 0, 0.0, halo_ref[0].astype(F32))
    ubuf[base:base + tm, :] = u_ref[0].astype(F32)
    wsums = []
    prev = ubuf
    for lvl in range(1, levels + 1):
        shift = 2 ** (lvl - 1)
        lanes = slice((lvl - 1) * gw, pw)
        if lvl < levels:
            cur = sbuf.at[lvl - 1]
            cur[0:margin, lanes] = jnp.zeros((margin, pw - (lvl - 1) * gw), F32)
            cur[margin:margin + ext, lanes] = (prev[margin:margin + ext, lanes]
                                               + prev[margin - shift:margin - shift + ext, lanes])
            wsums.append(cur[base:base + tm, (lvl - 1) * gw:lvl * gw])
            prev = cur
        else:
            wsums.append(prev[base:base + tm, lanes] + prev[base - shift:base - shift + tm, lanes])
    pos = j * tm + lax.broadcasted_iota(jnp.int32, (tm, 1), 0)
    ys = []
    for g, w in enumerate(POOL_WINDOWS):
        gs = slice(g * gw, (g + 1) * gw)
        cnt = jnp.minimum(pos + 1, w).astype(F32)
        dlt = wsums[g] / cnt - ubuf[base:base + tm, gs]
        ys.append(jnp.dot(dlt.astype(BF16), wg_ref[g], preferred_element_type=F32))
    y_pool = (jnp.concatenate(ys, axis=-1) * ps_ref[...]).astype(BF16)
    merged = _sigmoid(gp_ref[0, :, 0:d].astype(F32)) * jnp.dot(y_pool, wbp_ref[...], preferred_element_type=F32)
    merged += _sigmoid(gp_ref[0, :, d:2 * d].astype(F32)) * jnp.dot(hm_ref[0], wbm_ref[...],
                                                                   preferred_element_type=F32)
    merged += _sigmoid(gp_ref[0, :, 2 * d:3 * d].astype(F32)) * jnp.dot(fo_ref[0], wbf_ref[...],
                                                                       preferred_element_type=F32)
    o_ref[0] = x_ref[0] + jnp.dot(merged.astype(BF16), wo_ref[...], preferred_element_type=F32)


def _merge(p3, hm, fo, x3, wgrp, pscale, wbp, wbm, wbf, wo, *, pool_col, gate_col, tm=512):
    b, s, d = x3.shape
    pw = wbp.shape[0]
    pad = BF16_SUBLANES
    margin = 8
    assert max(POOL_WINDOWS) - 1 <= pad and gate_col == 0
    pcb = pool_col // pw
    halo_per_blk = tm // pad
    return pl.pallas_call(
        functools.partial(_merge_kernel, tm=tm),
        grid=(b, s // tm),
        in_specs=[pl.BlockSpec((1, tm, pw), lambda i, j: (i, j, pcb)),
                  pl.BlockSpec((1, pad, pw), lambda i, j: (i, jnp.maximum(j * halo_per_blk - 1, 0), pcb)),
                  pl.BlockSpec((1, tm, hm.shape[2]), lambda i, j: (i, j, 0)),
                  pl.BlockSpec((1, tm, fo.shape[2]), lambda i, j: (i, j, 0)),
                  pl.BlockSpec((1, tm, N_BRANCH * d), lambda i, j: (i, j, 0)),
                  pl.BlockSpec((1, tm, d), lambda i, j: (i, j, 0)),
                  _resident(wgrp.shape), _resident(pscale.shape), _resident(wbp.shape),
                  _resident(wbm.shape), _resident(wbf.shape), _resident(wo.shape)],
        out_specs=pl.BlockSpec((1, tm, d), lambda i, j: (i, j, 0)),
        out_shape=jax.ShapeDtypeStruct((b, s, d), F32),
        scratch_shapes=[pltpu.VMEM((margin + pad + tm, pw), F32),
                        pltpu.VMEM((len(POOL_WINDOWS) - 1, margin + pad + tm, pw), F32)],
        compiler_params=_cparams("parallel", "parallel"),
        name="merge",
    )(p3, p3, hm, fo, p3, x3, wgrp, pscale, wbp, wbm, wbf, wo)


def _swiglu_kernel(x_ref, g_ref, wg_ref, wu_ref, wd_ref, o_ref, a_ref, *, ff_chunk):
    x = x_ref[...]
    h = _rmsnorm(x, g_ref[...]).astype(BF16)
    for c in range(wg_ref.shape[1] // ff_chunk):
        sl = slice(c * ff_chunk, (c + 1) * ff_chunk)
        gate = jnp.dot(h, wg_ref[:, sl], preferred_element_type=F32)
        up = jnp.dot(h, wu_ref[:, sl], preferred_element_type=F32)
        a_ref[:, sl] = (gate * _sigmoid(gate) * up).astype(BF16)
    o_ref[...] = x + jnp.dot(a_ref[...], wd_ref[...], preferred_element_type=F32)


def _swiglu(x2, g, wg, wu, wd, *, tm=512, ff_chunk=256):
    n, d = x2.shape
    ff = wg.shape[1]
    assert ff % ff_chunk == 0
    return pl.pallas_call(
        functools.partial(_swiglu_kernel, ff_chunk=ff_chunk),
        grid=(n // tm,),
        in_specs=[pl.BlockSpec((tm, d), lambda i: (i, 0)), _resident((1, d)),
                  _resident(wg.shape), _resident(wu.shape), _resident(wd.shape)],
        out_specs=pl.BlockSpec((tm, d), lambda i: (i, 0)),
        out_shape=jax.ShapeDtypeStruct((n, d), F32),
        scratch_shapes=[pltpu.VMEM((tm, ff), BF16)],
        compiler_params=_cparams("parallel"),
        name="dense_swiglu",
    )(x2, g, wg, wu, wd)


def _router_kernel(x_ref, g_ref, wr_ref, br_ref, e_ref, gt_ref, cnt_ref, carry_ref):
    h = _rmsnorm(x_ref[...], g_ref[...])
    h_hi = h.astype(BF16)
    h_lo = (h - h_hi.astype(F32)).astype(BF16)
    logits = jnp.dot(jnp.concatenate([h_hi, h_hi, h_lo], axis=-1), wr_ref[...],
                     preferred_element_type=F32) + br_ref[...]
    lane = lax.broadcasted_iota(jnp.int32, logits.shape, 1)
    m1 = jnp.max(logits, axis=-1, keepdims=True)
    i1 = jnp.min(jnp.where(logits == m1, lane, LANES), axis=-1, keepdims=True)
    rest = jnp.where(lane == i1, NEG_BIG, logits)
    m2 = jnp.max(rest, axis=-1, keepdims=True)
    i2 = jnp.min(jnp.where(rest == m2, lane, LANES), axis=-1, keepdims=True)
    e2 = jnp.exp(m2 - m1)
    g1 = 1.0 / (1.0 + e2)
    gt_ref[...] = jnp.where(lane == 0, g1, jnp.where(lane == 1, e2 * g1, 0.0))

    @pl.when(pl.program_id(0) == 0)
    def _():
        carry_ref[...] = jnp.zeros_like(carry_ref)

    tm = logits.shape[0]
    pick0, pick1 = lane == i1, lane == i2
    picks = jnp.where(pick0 | pick1, 1.0, 0.0)
    earlier = (lax.broadcasted_iota(jnp.int32, (tm, tm), 0) > lax.broadcasted_iota(jnp.int32, (tm, tm), 1))
    prior = jnp.dot(earlier.astype(BF16), picks.astype(BF16), preferred_element_type=F32) + carry_ref[...]
    r0 = jnp.sum(jnp.where(pick0, prior, 0.0), axis=-1, keepdims=True).astype(jnp.int32)
    r1 = jnp.sum(jnp.where(pick1, prior, 0.0), axis=-1, keepdims=True).astype(jnp.int32)
    carry_ref[...] = prior[tm - 1:tm, :] + picks[tm - 1:tm, :]
    cnt_ref[...] = jnp.broadcast_to(carry_ref[...], cnt_ref.shape).astype(jnp.int32)
    e_ref[...] = jnp.where(lane == 0, i1, jnp.where(lane == 1, i2, jnp.where(lane == 2, r0,
                                                                          jnp.where(lane == 3, r1, 0))))


def _router(x2, g, wr, br, *, tm=512):
    n, d = x2.shape
    return pl.pallas_call(
        _router_kernel,
        grid=(n // tm,),
        in_specs=[pl.BlockSpec((tm, d), lambda i: (i, 0)), _resident((1, d)),
                  _resident(wr.shape), _resident((1, LANES))],
        out_specs=[pl.BlockSpec((tm, LANES), lambda i: (i, 0)), pl.BlockSpec((tm, LANES), lambda i: (i, 0)),
                   pl.BlockSpec((8, LANES), lambda i: (0, 0))],
        out_shape=[jax.ShapeDtypeStruct((n, LANES), jnp.int32), jax.ShapeDtypeStruct((n, LANES), F32),
                   jax.ShapeDtypeStruct((8, LANES), jnp.int32)],
        scratch_shapes=[pltpu.VMEM((1, LANES), F32)],
        compiler_params=_cparams("arbitrary"),
        name="router",
    )(x2, g, wr, br)


def _dispatch_kernel(pend_ref, dest_ref, x_ref, xs_ref, zero_ref, sem, zsem):
    tm = dest_ref.shape[2] // TOP_K
    rows = zero_ref.shape[0]
    step = pl.program_id(0)

    @pl.when(step == 0)
    def _():
        zero_ref[...] = jnp.zeros_like(zero_ref)

        def zero_copy(e):
            start = pl.multiple_of(pend_ref[e] - rows, rows)
            return pltpu.make_async_copy(zero_ref, xs_ref.at[pl.ds(start, rows)], zsem)

        def has_rows(e):
            return pend_ref[e] > (pend_ref[e - 1] if e else 0)

        n_exp = pend_ref.shape[0]
        n_rows = xs_ref.shape[0]
        tail_starts = [n_rows - (j + 1) * rows for j in range(min(n_exp, n_rows // rows))]

        def tail_copy(start):
            return pltpu.make_async_copy(zero_ref, xs_ref.at[pl.ds(start, rows)], zsem)

        for e in range(n_exp):
            @pl.when(has_rows(e))
            def _():
                zero_copy(e).start()
        for start in tail_starts:
            @pl.when(start >= pend_ref[n_exp - 1])
            def _():
                tail_copy(start).start()
        for e in range(n_exp):
            @pl.when(has_rows(e))
            def _():
                zero_copy(e).wait()
        for start in tail_starts:
            @pl.when(start >= pend_ref[n_exp - 1])
            def _():
                tail_copy(start).wait()

    def row_copy(t, k):
        return pltpu.make_async_copy(x_ref.at[pl.ds(t, 1)], xs_ref.at[pl.ds(dest_ref[0, 0, TOP_K * t + k], 1)], sem)

    def start(t, carry):
        for k in range(TOP_K):
            row_copy(t, k).start()
        return carry

    def wait(t, carry):
        for k in range(TOP_K):
            row_copy(t, k).wait()
        return carry

    lax.fori_loop(0, tm, start, 0, unroll=True)
    lax.fori_loop(0, tm, wait, 0, unroll=8)


def _dispatch(pend, dest2, x2, *, n_rows, rows, tm=256):
    n, d = x2.shape
    grid_spec = pltpu.PrefetchScalarGridSpec(
        num_scalar_prefetch=1,
        grid=(n // tm,),
        in_specs=[pl.BlockSpec((1, 1, TOP_K * tm), lambda i, pe: (i, 0, 0), memory_space=pltpu.SMEM),
                  pl.BlockSpec((tm, d), lambda i, pe: (i, 0))],
        out_specs=pl.BlockSpec(memory_space=pl.ANY),
        scratch_shapes=[pltpu.VMEM((rows, d), F32), pltpu.SemaphoreType.DMA(()), pltpu.SemaphoreType.DMA(())],
    )
    return pl.pallas_call(
        _dispatch_kernel,
        grid_spec=grid_spec,
        out_shape=jax.ShapeDtypeStruct((n_rows, d), F32),
        compiler_params=_cparams("arbitrary"),
        name="moe_dispatch",
    )(pend, dest2, x2)


def _experts_kernel(blk_e_ref, nact_ref, xs_ref, g_ref, wg_ref, wu_ref, wd_ref, ys_ref, h_ref, a_ref, acc_ref,
                    *, ff_chunk):
    del blk_e_ref
    i = pl.program_id(0)
    f = pl.program_id(1)

    @pl.when(i < nact_ref[0])
    def _():
        @pl.when(f == 0)
        def _():
            h_ref[...] = _rmsnorm(xs_ref[...], g_ref[...]).astype(BF16)
            acc_ref[...] = jnp.zeros_like(acc_ref)

        h = h_ref[...]
        for c in range(wg_ref.shape[2] // ff_chunk):
            sl = slice(c * ff_chunk, (c + 1) * ff_chunk)
            gate = jnp.dot(h, wg_ref[0, :, sl], preferred_element_type=F32)
            up = jnp.dot(h, wu_ref[0, :, sl], preferred_element_type=F32)
            a_ref[:, sl] = (gate * _sigmoid(gate) * up).astype(BF16)
        acc_ref[...] += jnp.dot(a_ref[...], wd_ref[0], preferred_element_type=F32)

        @pl.when(f == pl.num_programs(1) - 1)
        def _():
            ys_ref[...] = acc_ref[...]

    @pl.when((i >= nact_ref[0]) & (f == 0))
    def _():
        ys_ref[...] = jnp.zeros_like(ys_ref)


def _experts(blk_e, nact, xs, g, wg, wu, wd, *, rows, tf=1792, ff_chunk=256):
    n_rows, d = xs.shape
    ff = wg.shape[2]
    nf = ff // tf
    assert ff % tf == 0 and tf % ff_chunk == 0 and n_rows % rows == 0

    def blk(i, nact):
        return jnp.minimum(i, nact[0] - 1)

    def ff_tile(i, f, nact):
        return jnp.where(i < nact[0], f, nf - 1)

    grid_spec = pltpu.PrefetchScalarGridSpec(
        num_scalar_prefetch=2,
        grid=(n_rows // rows, nf),
        in_specs=[pl.BlockSpec((rows, d), lambda i, f, be, na: (blk(i, na), 0)),
                  pl.BlockSpec((1, d), lambda i, f, be, na: (0, 0)),
                  pl.BlockSpec((1, d, tf), lambda i, f, be, na: (be[blk(i, na)], 0, ff_tile(i, f, na))),
                  pl.BlockSpec((1, d, tf), lambda i, f, be, na: (be[blk(i, na)], 0, ff_tile(i, f, na))),
                  pl.BlockSpec((1, tf, d), lambda i, f, be, na: (be[blk(i, na)], ff_tile(i, f, na), 0))],
        out_specs=pl.BlockSpec((rows, d), lambda i, f, be, na: (i, 0)),
        scratch_shapes=[pltpu.VMEM((rows, d), BF16), pltpu.VMEM((rows, tf), BF16), pltpu.VMEM((rows, d), F32)],
    )
    return pl.pallas_call(
        functools.partial(_experts_kernel, ff_chunk=ff_chunk),
        grid_spec=grid_spec,
        out_shape=jax.ShapeDtypeStruct((n_rows, d), F32),
        compiler_params=_cparams("arbitrary", "arbitrary"),
        name="moe_experts",
    )(blk_e, nact, xs, g, wg, wu, wd)


def _combine_kernel(dest_ref, next_dest_ref, x_ref, gt_ref, fg_ref, ys_ref, o_ref, buf, sems, *, final_norm):
    tm = x_ref.shape[0]
    step = pl.program_id(0)
    slot = step % 2

    def row_copy(dref, s, t, k):
        return pltpu.make_async_copy(ys_ref.at[pl.ds(dref[0, 0, TOP_K * t + k], 1)], buf.at[s, k, pl.ds(t, 1)],
                                     sems.at[s])

    def gather(dref, s):
        def start(t, carry):
            for k in range(TOP_K):
                row_copy(dref, s, t, k).start()
            return carry
        lax.fori_loop(0, tm, start, 0, unroll=True)

    @pl.when(step == 0)
    def _():
        gather(dest_ref, 0)

    @pl.when(step + 1 < pl.num_programs(0))
    def _():
        gather(next_dest_ref, 1 - slot)

    def wait(t, carry):
        for k in range(TOP_K):
            row_copy(dest_ref, slot, t, k).wait()
        return carry

    lax.fori_loop(0, tm, wait, 0, unroll=8)
    gt = gt_ref[...]
    y = x_ref[...]
    for k in range(TOP_K):
        y = y + buf[slot, k] * gt[:, k:k + 1]
    o_ref[...] = _rmsnorm(y, fg_ref[...]) if final_norm else y


def _combine(dest2, x2, gates, fg, ys, *, final_norm, tm=256):
    n, d = x2.shape
    steps = n // tm
    return pl.pallas_call(
        functools.partial(_combine_kernel, final_norm=final_norm),
        grid=(steps,),
        in_specs=[pl.BlockSpec((1, 1, TOP_K * tm), lambda i: (i, 0, 0), memory_space=pltpu.SMEM),
                  pl.BlockSpec((1, 1, TOP_K * tm), lambda i: (jnp.minimum(i + 1, steps - 1), 0, 0),
                               memory_space=pltpu.SMEM),
                  pl.BlockSpec((tm, d), lambda i: (i, 0)),
                  pl.BlockSpec((tm, LANES), lambda i: (i, 0)),
                  _resident((1, d)),
                  pl.BlockSpec(memory_space=pl.ANY)],
        out_specs=pl.BlockSpec((tm, d), lambda i: (i, 0)),
        out_shape=jax.ShapeDtypeStruct((n, d), F32),
        scratch_shapes=[pltpu.VMEM((2, TOP_K, tm, d), F32), pltpu.SemaphoreType.DMA((2,))],
        compiler_params=_cparams("arbitrary"),
        name="moe_combine",
    )(dest2, dest2, x2, gates, fg, ys)


def _final_norm_kernel(x_ref, g_ref, o_ref):
    o_ref[...] = _rmsnorm(x_ref[...], g_ref[...])


def _final_norm(x2, g, *, tm=512):
    n, d = x2.shape
    return pl.pallas_call(
        _final_norm_kernel,
        grid=(n // tm,),
        in_specs=[pl.BlockSpec((tm, d), lambda i: (i, 0)), _resident((1, d))],
        out_specs=pl.BlockSpec((tm, d), lambda i: (i, 0)),
        out_shape=jax.ShapeDtypeStruct((n, d), F32),
        compiler_params=_cparams("parallel"),
        name="final_norm",
    )(x2, g)


def _token_mixing(x3, norm_g, w_in, pool_w_grp, pool_scale, conv_w, b_i, b_f, ml_norm_g, fox_b_f,
                  w_br_pool, w_br_ml, w_br_fox, w_out):
    b, s, d = x3.shape
    pool_w = w_br_pool.shape[0]
    ml_w = w_br_ml.shape[0]
    fox_w = w_br_fox.shape[0]
    ml_dh = ml_w // ML_HEADS
    fox_dh = fox_w // FOX_HEADS
    n_small = 2 * ML_HEADS + FOX_HEADS

    sizes = (pool_w, ml_w, ml_w, ml_w, ml_w, ML_HEADS, ML_HEADS, fox_w, fox_w, fox_w, FOX_HEADS, N_BRANCH * d)
    offs = [0]
    for sz in sizes:
        offs.append(offs[-1] + sz)
    assert offs[-1] == w_in.shape[1]
    (o_pool, o_q, o_k, o_v, o_o, o_i, o_f, o_fq, o_fk, o_fv, o_ff, o_g) = offs[:-1]

    def cols(o, sz):
        return w_in[:, o:o + sz]

    wm = jnp.concatenate([cols(o_g, N_BRANCH * d), cols(o_q, ml_w), cols(o_k, ml_w), cols(o_v, ml_w),
                          cols(o_o, ml_w), cols(o_pool, pool_w), cols(o_fq, fox_w), cols(o_fk, fox_w),
                          cols(o_fv, fox_w)], axis=1).astype(BF16)
    c_gate = 0
    c_qk = c_gate + N_BRANCH * d
    c_v = c_qk + 2 * ml_w
    c_o = c_v + ml_w
    c_pool = c_o + ml_w
    c_fq = c_pool + pool_w
    c_fk = c_fq + fox_w
    c_fv = c_fk + fox_w
    ws = jnp.concatenate([cols(o_i, ML_HEADS), cols(o_f, ML_HEADS), cols(o_ff, FOX_HEADS),
                          jnp.zeros((d, LANES - n_small), w_in.dtype)], axis=1).astype(BF16)
    bias = jnp.concatenate([b_i, b_f, fox_b_f, jnp.zeros((LANES - n_small,), F32)]).reshape(1, LANES)

    proj, small = _inproj(x3.reshape(b * s, d), norm_g.reshape(1, d), wm, ws)
    p3 = proj.reshape(b, s, -1)
    gcol, grow, aq, ak = _gateprep(small.reshape(b, s, LANES), bias, fox_dh)
    fo = _fox(p3, aq, ak, q_col=c_fq, k_col=c_fk, v_col=c_fv, dh=fox_dh)
    hm = _mlstm(p3, gcol, grow, conv_w, ml_norm_g.reshape(1, ml_w), qk_col=c_qk, v_col=c_v, o_col=c_o, dh=ml_dh)
    return _merge(p3, hm, fo, x3, pool_w_grp.astype(BF16), pool_scale.reshape(1, pool_w),
                  w_br_pool.astype(BF16), w_br_ml.astype(BF16), w_br_fox.astype(BF16), w_out.astype(BF16),
                  pool_col=c_pool, gate_col=c_gate)


def _moe(x2, norm_g, w_router, b_router, w_gate, w_up, w_down, final_g, *, rows=512):
    n, d = x2.shape
    n_exp = w_router.shape[1]
    wr = jnp.concatenate([w_router.astype(F32), jnp.zeros((d, LANES - n_exp), F32)], axis=1)
    wr_hi = wr.astype(BF16)
    wr_lo = (wr - wr_hi.astype(F32)).astype(BF16)
    br = jnp.concatenate([b_router.astype(F32), jnp.full((LANES - n_exp,), NEG_BIG, F32)]).reshape(1, LANES)
    routed, gates, cnt = _router(x2, norm_g.reshape(1, d), jnp.concatenate([wr_hi, wr_lo, wr_hi], axis=0), br)

    e_flat = routed[:, :TOP_K].reshape(n * TOP_K)
    rank = routed[:, TOP_K:2 * TOP_K].reshape(n * TOP_K)
    counts = cnt[0, :n_exp]
    padded = ((counts + rows - 1) // rows) * rows
    pend = jnp.cumsum(padded)
    onehot = e_flat[:, None] == jnp.arange(n_exp, dtype=jnp.int32)[None, :]
    dest = (jnp.sum(jnp.where(onehot, (pend - padded)[None, :], 0), axis=1) + rank).astype(jnp.int32)
    n_blk = (n * TOP_K + n_exp * (rows - 1) + rows - 1) // rows
    blk_start = jnp.arange(n_blk, dtype=jnp.int32) * rows
    blk_e = jnp.minimum(jnp.sum(blk_start[:, None] >= pend[None, :], axis=1), n_exp - 1).astype(jnp.int32)
    nact = (pend[-1:] // rows).astype(jnp.int32)

    tm = 512
    dest2 = dest.reshape(n // tm, 1, TOP_K * tm)
    xs = _dispatch(pend.astype(jnp.int32), dest2, x2, n_rows=n_blk * rows, rows=rows, tm=tm)
    ys = _experts(blk_e, nact, xs, norm_g.reshape(1, d), w_gate.astype(BF16), w_up.astype(BF16),
                  w_down.astype(BF16), rows=rows)
    fg = jnp.ones((1, d), F32) if final_g is None else final_g.reshape(1, d)
    return _combine(dest2, x2, gates, fg, ys, final_norm=final_g is not None, tm=tm)


def kernel(x, mix_norm_g, w_in, pool_w_grp, pool_scale, ml_conv_w, ml_b_i, ml_b_f, ml_norm_g, fox_b_f,
           w_br_pool, w_br_ml, w_br_fox, w_out, ffn_norm_g, ff_w_gate, ff_w_up, ff_w_down,
           moe_w_router, moe_b_router, moe_w_gate, moe_w_up, moe_w_down, final_norm_g):
    b, s, d = x.shape
    depth = mix_norm_g.shape[0]
    fused_final = False
    for l in range(depth):
        x = _token_mixing(x, mix_norm_g[l], w_in[l], pool_w_grp[l], pool_scale[l], ml_conv_w[l], ml_b_i[l],
                          ml_b_f[l], ml_norm_g[l], fox_b_f[l], w_br_pool[l], w_br_ml[l], w_br_fox[l], w_out[l])
        x2 = x.reshape(b * s, d)
        if l % 2 == 0:
            x2 = _swiglu(x2, ffn_norm_g[l].reshape(1, d), ff_w_gate[l // 2].astype(BF16),
                         ff_w_up[l // 2].astype(BF16), ff_w_down[l // 2].astype(BF16))
        else:
            fused_final = l == depth - 1
            x2 = _moe(x2, ffn_norm_g[l], moe_w_router[l // 2], moe_b_router[l // 2], moe_w_gate[l // 2],
                      moe_w_up[l // 2], moe_w_down[l // 2], final_norm_g if fused_final else None)
        x = x2.reshape(b, s, d)
    if not fused_final:
        x = _final_norm(x.reshape(b * s, d), final_norm_g.reshape(1, d)).reshape(b, s, d)
    return x
```

```python
import functools
import math

import jax
import jax.numpy as jnp
import numpy as np
from jax import lax
from jax.experimental import pallas as pl
from jax.experimental.pallas import tpu as pltpu

F32 = jnp.float32
BF16 = jnp.bfloat16

NORM_EPS = 1e-6
POOL_WINDOWS = (2, 4, 8, 16)
ML_HEADS = 4
FOX_HEADS = 8
TOP_K = 2
N_BRANCH = 3

LANES = 128
BF16_SUBLANES = 16
VMEM_LIMIT_BYTES = 56 * 1024 * 1024

NEG_BIG = -1e30


def _cparams(*sem):
    return pltpu.CompilerParams(dimension_semantics=sem, vmem_limit_bytes=VMEM_LIMIT_BYTES)


def _resident(shape):
    zeros = (0,) * len(shape)
    return pl.BlockSpec(shape, lambda *_: zeros, pipeline_mode=pl.Buffered(1))


def _rmsnorm(x, g):
    return x * lax.rsqrt(jnp.mean(x * x, axis=-1, keepdims=True) + NORM_EPS) * g


def _sigmoid(x):
    return 1.0 / (1.0 + jnp.exp2(x * (-math.log2(math.e))))


def _log_sigmoid(x):
    return jnp.minimum(x, 0.0) - jnp.log(1.0 + jnp.exp(-jnp.abs(x)))


def _split3(x):
    hi = x.astype(BF16)
    r = x - hi.astype(F32)
    mid = r.astype(BF16)
    lo = (r - mid.astype(F32)).astype(BF16)
    return hi, mid, lo


def _cumsum_rows(tril, x):
    return sum(jnp.dot(tril, part, preferred_element_type=F32) for part in _split3(x))


def _inproj_kernel(x_ref, g_ref, wm_ref, ws_ref, p_ref, s_ref, *, col_chunk):
    h = _rmsnorm(x_ref[...], g_ref[...]).astype(BF16)
    s_ref[...] = jnp.dot(h, ws_ref[...], preferred_element_type=F32)
    for c in range(wm_ref.shape[1] // col_chunk):
        sl = slice(c * col_chunk, (c + 1) * col_chunk)
        p_ref[:, sl] = jnp.dot(h, wm_ref[:, sl], preferred_element_type=F32).astype(BF16)


def _inproj(x2, g, wm, ws, *, tm=512, col_chunk=1024):
    n, d = x2.shape
    wcols = wm.shape[1]
    return pl.pallas_call(
        functools.partial(_inproj_kernel, col_chunk=col_chunk),
        grid=(n // tm,),
        in_specs=[pl.BlockSpec((tm, d), lambda i: (i, 0)),
                  _resident((1, d)), _resident((d, wcols)), _resident((d, LANES))],
        out_specs=[pl.BlockSpec((tm, wcols), lambda i: (i, 0)),
                   pl.BlockSpec((tm, LANES), lambda i: (i, 0))],
        out_shape=[jax.ShapeDtypeStruct((n, wcols), BF16), jax.ShapeDtypeStruct((n, LANES), F32)],
        compiler_params=_cparams("parallel"),
        name="inproj",
    )(x2, g, wm, ws)


AUG_TERMS = 3
AUG_STRIDE = 8


def _aug_placement(dh):
    pairs = FOX_HEADS * dh // LANES
    width = pairs * LANES
    pq = np.zeros((AUG_TERMS * LANES, width), np.float32)
    pk = np.zeros((AUG_TERMS * LANES, width), np.float32)
    cq = np.zeros((1, width), np.float32)
    ck = np.zeros((1, width), np.float32)
    for h in range(FOX_HEADS):
        base = (h // 2) * LANES + (h % 2) * AUG_STRIDE
        for t in range(AUG_TERMS):
            src = t * LANES + 2 * ML_HEADS + h
            pq[src, base + t] = 1.0
            pk[src, base + AUG_TERMS + t] = -1.0
            cq[0, base + AUG_TERMS + t] = 1.0
            ck[0, base + t] = 1.0
    return jnp.asarray(pq, BF16), jnp.asarray(pk, BF16), jnp.asarray(cq), jnp.asarray(ck)


def _gateprep_kernel(s_ref, b_ref, pq_ref, pk_ref, cq_ref, ck_ref, col_ref, row_ref, aq_ref, ak_ref, carry_ref):
    t = s_ref.shape[1]

    @pl.when(pl.program_id(1) == 0)
    def _():
        carry_ref[...] = jnp.zeros_like(carry_ref)

    pre = s_ref[0] + b_ref[...]
    lane = lax.broadcasted_iota(jnp.int32, pre.shape, 1)
    ls = _log_sigmoid(pre)
    forget = (lane >= ML_HEADS) & (lane < 2 * ML_HEADS + FOX_HEADS)
    tril = (lax.broadcasted_iota(jnp.int32, (t, t), 0) >= lax.broadcasted_iota(jnp.int32, (t, t), 1)).astype(BF16)
    fcum = _cumsum_rows(tril, jnp.where(forget, ls, 0.0)) + carry_ref[...]
    carry_ref[...] = fcum[t - 1:t, :]
    col = jnp.where(lane < ML_HEADS, pre, fcum)
    col_ref[0] = col
    row_ref[0] = col.T[0:row_ref.shape[1], :]
    parts = jnp.concatenate(_split3(fcum), axis=-1)
    aq_ref[0] = (jnp.dot(parts, pq_ref[...], preferred_element_type=F32) + cq_ref[...]).astype(BF16)
    ak_ref[0] = (jnp.dot(parts, pk_ref[...], preferred_element_type=F32) + ck_ref[...]).astype(BF16)


def _gateprep(s3, bias, dh, *, tg=512):
    b, s, _ = s3.shape
    n_rows = 2 * ML_HEADS + FOX_HEADS
    pq, pk, cq, ck = _aug_placement(dh)
    width = pq.shape[1]
    return pl.pallas_call(
        _gateprep_kernel,
        grid=(b, s // tg),
        in_specs=[pl.BlockSpec((1, tg, LANES), lambda i, j: (i, j, 0)), _resident((1, LANES)),
                  _resident(pq.shape), _resident(pk.shape), _resident(cq.shape), _resident(ck.shape)],
        out_specs=[pl.BlockSpec((1, tg, LANES), lambda i, j: (i, j, 0)),
                   pl.BlockSpec((1, n_rows, tg), lambda i, j: (i, 0, j)),
                   pl.BlockSpec((1, tg, width), lambda i, j: (i, j, 0)),
                   pl.BlockSpec((1, tg, width), lambda i, j: (i, j, 0))],
        out_shape=[jax.ShapeDtypeStruct((b, s, LANES), F32), jax.ShapeDtypeStruct((b, n_rows, s), F32),
                   jax.ShapeDtypeStruct((b, s, width), BF16), jax.ShapeDtypeStruct((b, s, width), BF16)],
        scratch_shapes=[pltpu.VMEM((1, LANES), F32)],
        compiler_params=_cparams("parallel", "arbitrary"),
        name="gateprep",
    )(s3, bias, pq, pk, cq, ck)


def _fox_kernel(q_ref, aq_ref, k_ref, ak_ref, v_ref, o_ref, m_ref, acc_ref, *, tq, tk, diag_strips, dh, scale):
    qi = pl.program_id(2)
    lane = lax.broadcasted_iota(jnp.int32, (tq, LANES), 1)
    q2 = q_ref[0] * scale
    aq = aq_ref[0]
    in_head = [(lane >= a * dh) & (lane < (a + 1) * dh) for a in range(2)]
    q_ops = []
    for a in range(2):
        in_aug = (lane >= a * AUG_STRIDE) & (lane < a * AUG_STRIDE + 2 * AUG_TERMS)
        q_ops.append(jnp.concatenate([jnp.where(in_head[a], q2, jnp.zeros_like(q2)),
                                      jnp.where(in_aug, aq, jnp.zeros_like(aq))], axis=-1))
    m_ref[...] = jnp.full_like(m_ref, NEG_BIG)
    acc_ref[...] = jnp.zeros_like(acc_ref)
    def step(start, width, row0=0, masked=False):
        rows = slice(row0, tq)
        kb = jnp.concatenate([k_ref[0, pl.ds(start, width), :], ak_ref[0, pl.ds(start, width), :]], axis=-1)
        vb = v_ref[0, pl.ds(start, width), :]
        key_lane = lax.broadcasted_iota(jnp.int32, (width, LANES), 1)
        key_head = [(key_lane >= a * dh) & (key_lane < (a + 1) * dh) for a in range(2)]
        if masked:
            causal = (lax.broadcasted_iota(jnp.int32, (tq - row0, width), 1)
                      <= lax.broadcasted_iota(jnp.int32, (tq - row0, width), 0))
        for a in range(2):
            s = lax.dot_general(q_ops[a][rows], kb, (((1,), (1,)), ((), ())), preferred_element_type=F32)
            if masked:
                s = jnp.where(causal, s, NEG_BIG)
            m_prev = m_ref[a, rows]
            m_new = jnp.maximum(m_prev, jnp.max(s, axis=-1, keepdims=True))
            alpha = jnp.exp(m_prev - m_new)
            pexp = jnp.exp(s - jnp.concatenate([m_new] * (width // LANES), axis=-1))
            v_op = jnp.where(key_head[a], vb, jnp.ones_like(vb))
            acc_ref[a, rows] = alpha * acc_ref[a, rows] + jnp.dot(pexp.astype(BF16), v_op,
                                                                  preferred_element_type=F32)
            m_ref[a, rows] = m_new

    n_wide = (qi * tq) // tk
    n_narrow = qi - n_wide * (tk // tq)

    def wide_body(j, carry):
        step(pl.multiple_of(j * tk, tk), tk)
        return carry

    def narrow_body(j, carry):
        step(pl.multiple_of((n_wide * (tk // tq) + j) * tq, tq), tq)
        return carry

    lax.fori_loop(0, n_wide, wide_body, 0)
    if tk != tq:
        lax.fori_loop(0, n_narrow, narrow_body, 0)
    strip = tq // diag_strips
    for c in range(diag_strips):
        step(pl.multiple_of(qi * tq + c * strip, strip), strip, row0=c * strip, masked=True)
    outs = [acc_ref[a] / pltpu.roll(acc_ref[a], LANES // 2, 1) for a in range(2)]
    o_ref[0] = jnp.where(in_head[0], outs[0], outs[1]).astype(BF16)


def _fox(p3, aq, ak, *, q_col, k_col, v_col, dh, tq=1024, tk=2048, diag_strips=2):
    b, s, _ = p3.shape
    assert 2 * dh == LANES, "two heads share one 128-lane block"
    assert tk % tq == 0 and s % tq == 0
    pairs = FOX_HEADS * dh // LANES
    scale = dh ** -0.5
    assert math.frexp(scale)[0] == 0.5, "score scale is folded into bf16 q; exact only for a power of two"
    qb, kb, vb = q_col // LANES, k_col // LANES, v_col // LANES
    return pl.pallas_call(
        functools.partial(_fox_kernel, tq=tq, tk=tk, diag_strips=diag_strips, dh=dh, scale=scale),
        grid=(b, pairs, s // tq),
        in_specs=[pl.BlockSpec((1, tq, LANES), lambda i, p, q: (i, q, qb + p)),
                  pl.BlockSpec((1, tq, LANES), lambda i, p, q: (i, q, p)),
                  pl.BlockSpec((1, s, LANES), lambda i, p, q: (i, 0, kb + p)),
                  pl.BlockSpec((1, s, LANES), lambda i, p, q: (i, 0, p)),
                  pl.BlockSpec((1, s, LANES), lambda i, p, q: (i, 0, vb + p))],
        out_specs=pl.BlockSpec((1, tq, LANES), lambda i, p, q: (i, q, p)),
        out_shape=jax.ShapeDtypeStruct((b, s, pairs * LANES), BF16),
        scratch_shapes=[pltpu.VMEM((2, tq, LANES), F32), pltpu.VMEM((2, tq, LANES), F32)],
        compiler_params=_cparams("parallel", "parallel", "arbitrary"),
        name="fox_attention",
    )(p3, aq, p3, ak, p3)


def _mlstm_kernel(qk_ref, halo_ref, v_ref, og_ref, gc_ref, gr_ref, cw_ref, ng_ref, out_ref,
                  ubuf, ct_ref, n_ref, m_ref, g0_ref, *, chunk, dh):
    c = pl.program_id(1)
    width = ML_HEADS * dh
    taps = cw_ref.shape[0]
    pad = halo_ref.shape[1]
    assert 3 * ML_HEADS <= BF16_SUBLANES

    @pl.when(c == 0)
    def _():
        ct_ref[...] = jnp.zeros_like(ct_ref)
        n_ref[...] = jnp.zeros_like(n_ref)
        m_ref[...] = jnp.zeros_like(m_ref)
        g0_ref[...] = jnp.zeros_like(g0_ref)

    ubuf[0:pad, :] = jnp.where(c == 0, 0.0, halo_ref[0].astype(F32))
    ubuf[pad:pad + chunk, :] = qk_ref[0].astype(F32)

    def conv_silu(col0):
        y = sum(ubuf[pad - taps + 1 + j:pad - taps + 1 + j + chunk, col0:col0 + dh] * cw_ref[j:j + 1, col0:col0 + dh]
                for j in range(taps))
        return y * _sigmoid(y)

    ri = lax.broadcasted_iota(jnp.int32, (chunk, chunk), 0)
    ci = lax.broadcasted_iota(jnp.int32, (chunk, chunk), 1)
    visible = ri <= ci
    gc = gc_ref[0]
    gr = gr_ref[0]
    g0 = g0_ref[...]
    g0_ref[...] = gc[chunk - 1:chunk, :]
    contract_last = (((1,), (1,)), ((), ()))
    contract_first = (((0,), (0,)), ((), ()))
    part_row = lax.broadcasted_iota(jnp.int32, (BF16_SUBLANES, dh), 0)
    slab_row = lax.broadcasted_iota(jnp.int32, (BF16_SUBLANES, chunk), 0)
    ones_dh = jnp.ones((dh, dh), BF16)

    slab = jnp.zeros((BF16_SUBLANES, chunk), F32)
    heads = []
    for h in range(ML_HEADS):
        hs = slice(h * dh, (h + 1) * dh)
        q = conv_silu(h * dh)
        k = conv_silu(width + h * dh) * dh ** -0.5
        qb, kb = q.astype(BF16), k.astype(BF16)
        vb = v_ref[0, :, hs]
        g0_h = g0[:, ML_HEADS + h:ML_HEADS + h + 1]
        key_col = gc[:, h:h + 1] - (gc[:, ML_HEADS + h:ML_HEADS + h + 1] - g0_h)
        li_r = gr[h:h + 1, :]
        b_r = gr[ML_HEADS + h:ML_HEADS + h + 1, :] - g0_h
        b_last = b_r[:, chunk - 1:chunk]
        m_prev = m_ref[h:h + 1, 0:1]

        d_t = jnp.where(visible, b_r + key_col, NEG_BIG)
        m_inter = b_r + m_prev
        m_t = jnp.maximum(m_inter, jnp.max(d_t, axis=0, keepdims=True))
        w_inter = jnp.exp(m_inter - m_t)
        p_t = jnp.exp(d_t - m_t) * lax.dot_general(kb, qb, contract_last, preferred_element_type=F32)
        n_hi, n_mid, n_lo = (part.astype(F32) for part in _split3(n_ref[h]))
        n_parts = jnp.where(part_row == 0, n_hi, jnp.where(part_row == 1, n_mid,
                                                           jnp.where(part_row == 2, n_lo, 0.0))).astype(BF16)
        nq = jnp.sum(lax.dot_general(n_parts, qb, contract_last, preferred_element_type=F32), axis=0, keepdims=True)
        den = w_inter * nq + jnp.sum(p_t, axis=0, keepdims=True)
        inv = 1.0 / jnp.maximum(jnp.abs(den), jnp.exp(-m_t))

        g_r = b_last - b_r + li_r
        m_new = jnp.maximum(b_last + m_prev, jnp.max(g_r, axis=-1, keepdims=True))
        decay = jnp.exp(b_last + m_prev - m_new)
        wk_r = jnp.exp(g_r - m_new)
        for slot, vec in enumerate((w_inter, inv, wk_r)):
            slab = jnp.where(slab_row == 3 * h + slot, vec, slab)
        heads.append((hs, k, qb, vb, p_t.astype(BF16), decay, m_new))

    cols = jnp.concatenate([slab, jnp.zeros((LANES - BF16_SUBLANES, chunk), F32)], axis=0).T

    for h, (hs, k, qb, vb, p_tb, decay, m_new) in enumerate(heads):
        w_col, inv_col, wk_col = (cols[:, 3 * h + slot:3 * h + slot + 1] for slot in range(3))
        num = (w_col * jnp.dot(qb, ct_ref[h].astype(BF16), preferred_element_type=F32)
               + lax.dot_general(p_tb, vb, contract_first, preferred_element_type=F32))
        hh = num * inv_col
        kw = k * wk_col
        ct_ref[h] = decay * ct_ref[h] + lax.dot_general(kw.astype(BF16), vb, contract_first,
                                                       preferred_element_type=F32)
        n_ref[h] = decay * n_ref[h] + jnp.sum(kw, axis=0, keepdims=True)
        m_ref[h:h + 1, :] = jnp.broadcast_to(m_new, (1, LANES))

        sq = hh * hh
        sq_hi = sq.astype(BF16)
        sq_lo = (sq - sq_hi.astype(F32)).astype(BF16)
        mean_sq = (jnp.dot(sq_hi, ones_dh, preferred_element_type=F32)
                   + jnp.dot(sq_lo, ones_dh, preferred_element_type=F32)) * (1.0 / dh)
        hn = hh * lax.rsqrt(mean_sq + NORM_EPS) * ng_ref[:, hs]
        out_ref[0, :, hs] = (hn * _sigmoid(og_ref[0, :, hs].astype(F32))).astype(BF16)


def _mlstm(p3, gcol, grow, conv_w, norm_g, *, qk_col, v_col, o_col, dh, chunk=256):
    b, s, _ = p3.shape
    width = ML_HEADS * dh
    pad = BF16_SUBLANES
    assert conv_w.shape[0] - 1 <= pad
    qkb, vb, ob = qk_col // (2 * width), v_col // width, o_col // width
    halo_per_chunk = chunk // pad
    return pl.pallas_call(
        functools.partial(_mlstm_kernel, chunk=chunk, dh=dh),
        grid=(b, s // chunk),
        in_specs=[pl.BlockSpec((1, chunk, 2 * width), lambda i, c: (i, c, qkb)),
                  pl.BlockSpec((1, pad, 2 * width),
                               lambda i, c: (i, jnp.maximum(c * halo_per_chunk - 1, 0), qkb)),
                  pl.BlockSpec((1, chunk, width), lambda i, c: (i, c, vb)),
                  pl.BlockSpec((1, chunk, width), lambda i, c: (i, c, ob)),
                  pl.BlockSpec((1, chunk, LANES), lambda i, c: (i, c, 0)),
                  pl.BlockSpec((1, grow.shape[1], chunk), lambda i, c: (i, 0, c)),
                  _resident(conv_w.shape), _resident((1, width))],
        out_specs=pl.BlockSpec((1, chunk, width), lambda i, c: (i, c, 0)),
        out_shape=jax.ShapeDtypeStruct((b, s, width), BF16),
        scratch_shapes=[pltpu.VMEM((pad + chunk, 2 * width), F32),
                        pltpu.VMEM((ML_HEADS, dh, dh), F32),
                        pltpu.VMEM((ML_HEADS, 1, dh), F32),
                        pltpu.VMEM((ML_HEADS, LANES), F32),
                        pltpu.VMEM((1, LANES), F32)],
        compiler_params=_cparams("parallel", "arbitrary"),
        name="mlstm",
    )(p3, p3, p3, p3, gcol, grow, conv_w, norm_g)


def _merge_kernel(u_ref, halo_ref, hm_ref, fo_ref, gp_ref, x_ref, wg_ref, ps_ref, wbp_ref, wbm_ref, wbf_ref,
                  wo_ref, o_ref, ubuf, sbuf, *, tm):
    j = pl.program_id(1)
    pad = halo_ref.shape[1]
    gw = wg_ref.shape[1]
    pw = ubuf.shape[1]
    d = x_ref.shape[2]
    margin = ubuf.shape[0] - pad - tm
    base = margin + pad
    ext = pad + tm
    levels = len(POOL_WINDOWS)
    assert all(w == 2 ** (g + 1) for g, w in enumerate(POOL_WINDOWS)) and 2 ** (levels - 1) <= margin

    ubuf[0:margin, :] = jnp.zeros((margin, pw), F32)
    ubuf[margin:base, :] = jnp.where(j == 0, 0.0, halo_ref[0].astype(F32))
    ubuf[base:base + tm, :] = u_ref[0].astype(F32)
    wsums = []
    prev = ubuf
    for lvl in range(1, levels + 1):
        shift = 2 ** (lvl - 1)
        lanes = slice((lvl - 1) * gw, pw)
        if lvl < levels:
            cur = sbuf.at[lvl - 1]
            cur[0:margin, lanes] = jnp.zeros((margin, pw - (lvl - 1) * gw), F32)
            cur[margin:margin + ext, lanes] = (prev[margin:margin + ext, lanes]
                                               + prev[margin - shift:margin - shift + ext, lanes])
            wsums.append(cur[base:base + tm, (lvl - 1) * gw:lvl * gw])
            prev = cur
        else:
            wsums.append(prev[base:base + tm, lanes] + prev[base - shift:base - shift + tm, lanes])
    pos = j * tm + lax.broadcasted_iota(jnp.int32, (tm, 1), 0)
    ys = []
    for g, w in enumerate(POOL_WINDOWS):
        gs = slice(g * gw, (g + 1) * gw)
        cnt = jnp.minimum(pos + 1, w).astype(F32)
        dlt = wsums[g] / cnt - ubuf[base:base + tm, gs]
        ys.append(jnp.dot(dlt.astype(BF16), wg_ref[g], preferred_element_type=F32))
    y_pool = (jnp.concatenate(ys, axis=-1) * ps_ref[...]).astype(BF16)
    merged = _sigmoid(gp_ref[0, :, 0:d].astype(F32)) * jnp.dot(y_pool, wbp_ref[...], preferred_element_type=F32)
    merged += _sigmoid(gp_ref[0, :, d:2 * d].astype(F32)) * jnp.dot(hm_ref[0], wbm_ref[...],
                                                                   preferred_element_type=F32)
    merged += _sigmoid(gp_ref[0, :, 2 * d:3 * d].astype(F32)) * jnp.dot(fo_ref[0], wbf_ref[...],
                                                                       preferred_element_type=F32)
    o_ref[0] = x_ref[0] + jnp.dot(merged.astype(BF16), wo_ref[...], preferred_element_type=F32)


def _merge(p3, hm, fo, x3, wgrp, pscale, wbp, wbm, wbf, wo, *, pool_col, gate_col, tm=512):
    b, s, d = x3.shape
    pw = wbp.shape[0]
    pad = BF16_SUBLANES
    margin = 8
    assert max(POOL_WINDOWS) - 1 <= pad and gate_col == 0
    pcb = pool_col // pw
    halo_per_blk = tm // pad
    return pl.pallas_call(
        functools.partial(_merge_kernel, tm=tm),
        grid=(b, s // tm),
        in_specs=[pl.BlockSpec((1, tm, pw), lambda i, j: (i, j, pcb)),
                  pl.BlockSpec((1, pad, pw), lambda i, j: (i, jnp.maximum(j * halo_per_blk - 1, 0), pcb)),
                  pl.BlockSpec((1, tm, hm.shape[2]), lambda i, j: (i, j, 0)),
                  pl.BlockSpec((1, tm, fo.shape[2]), lambda i, j: (i, j, 0)),
                  pl.BlockSpec((1, tm, N_BRANCH * d), lambda i, j: (i, j, 0)),
                  pl.BlockSpec((1, tm, d), lambda i, j: (i, j, 0)),
                  _resident(wgrp.shape), _resident(pscale.shape), _resident(wbp.shape),
                  _resident(wbm.shape), _resident(wbf.shape), _resident(wo.shape)],
        out_specs=pl.BlockSpec((1, tm, d), lambda i, j: (i, j, 0)),
        out_shape=jax.ShapeDtypeStruct((b, s, d), F32),
        scratch_shapes=[pltpu.VMEM((margin + pad + tm, pw), F32),
                        pltpu.VMEM((len(POOL_WINDOWS) - 1, margin + pad + tm, pw), F32)],
        compiler_params=_cparams("parallel", "parallel"),
        name="merge",
    )(p3, p3, hm, fo, p3, x3, wgrp, pscale, wbp, wbm, wbf, wo)


def _swiglu_kernel(x_ref, g_ref, wg_ref, wu_ref, wd_ref, o_ref, a_ref, *, ff_chunk):
    x = x_ref[...]
    h = _rmsnorm(x, g_ref[...]).astype(BF16)
    for c in range(wg_ref.shape[1] // ff_chunk):
        sl = slice(c * ff_chunk, (c + 1) * ff_chunk)
        gate = jnp.dot(h, wg_ref[:, sl], preferred_element_type=F32)
        up = jnp.dot(h, wu_ref[:, sl], preferred_element_type=F32)
        a_ref[:, sl] = (gate * _sigmoid(gate) * up).astype(BF16)
    o_ref[...] = x + jnp.dot(a_ref[...], wd_ref[...], preferred_element_type=F32)


def _swiglu(x2, g, wg, wu, wd, *, tm=512, ff_chunk=256):
    n, d = x2.shape
    ff = wg.shape[1]
    assert ff % ff_chunk == 0
    return pl.pallas_call(
        functools.partial(_swiglu_kernel, ff_chunk=ff_chunk),
        grid=(n // tm,),
        in_specs=[pl.BlockSpec((tm, d), lambda i: (i, 0)), _resident((1, d)),
                  _resident(wg.shape), _resident(wu.shape), _resident(wd.shape)],
        out_specs=pl.BlockSpec((tm, d), lambda i: (i, 0)),
        out_shape=jax.ShapeDtypeStruct((n, d), F32),
        scratch_shapes=[pltpu.VMEM((tm, ff), BF16)],
        compiler_params=_cparams("parallel"),
        name="dense_swiglu",
    )(x2, g, wg, wu, wd)


def _router_kernel(x_ref, g_ref, wr_ref, br_ref, e_ref, gt_ref, cnt_ref, carry_ref):
    h = _rmsnorm(x_ref[...], g_ref[...])
    h_hi = h.astype(BF16)
    h_lo = (h - h_hi.astype(F32)).astype(BF16)
    logits = jnp.dot(jnp.concatenate([h_hi, h_hi, h_lo], axis=-1), wr_ref[...],
                     preferred_element_type=F32) + br_ref[...]
    lane = lax.broadcasted_iota(jnp.int32, logits.shape, 1)
    m1 = jnp.max(logits, axis=-1, keepdims=True)
    i1 = jnp.min(jnp.where(logits == m1, lane, LANES), axis=-1, keepdims=True)
    rest = jnp.where(lane == i1, NEG_BIG, logits)
    m2 = jnp.max(rest, axis=-1, keepdims=True)
    i2 = jnp.min(jnp.where(rest == m2, lane, LANES), axis=-1, keepdims=True)
    e2 = jnp.exp(m2 - m1)
    g1 = 1.0 / (1.0 + e2)
    gt_ref[...] = jnp.where(lane == 0, g1, jnp.where(lane == 1, e2 * g1, 0.0))

    @pl.when(pl.program_id(0) == 0)
    def _():
        carry_ref[...] = jnp.zeros_like(carry_ref)

    tm = logits.shape[0]
    pick0, pick1 = lane == i1, lane == i2
    picks = jnp.where(pick0 | pick1, 1.0, 0.0)
    earlier = (lax.broadcasted_iota(jnp.int32, (tm, tm), 0) > lax.broadcasted_iota(jnp.int32, (tm, tm), 1))
    prior = jnp.dot(earlier.astype(BF16), picks.astype(BF16), preferred_element_type=F32) + carry_ref[...]
    r0 = jnp.sum(jnp.where(pick0, prior, 0.0), axis=-1, keepdims=True).astype(jnp.int32)
    r1 = jnp.sum(jnp.where(pick1, prior, 0.0), axis=-1, keepdims=True).astype(jnp.int32)
    carry_ref[...] = prior[tm - 1:tm, :] + picks[tm - 1:tm, :]
    cnt_ref[...] = jnp.broadcast_to(carry_ref[...], cnt_ref.shape).astype(jnp.int32)
    e_ref[...] = jnp.where(lane == 0, i1, jnp.where(lane == 1, i2, jnp.where(lane == 2, r0,
                                                                          jnp.where(lane == 3, r1, 0))))


def _router(x2, g, wr, br, *, tm=512):
    n, d = x2.shape
    return pl.pallas_call(
        _router_kernel,
        grid=(n // tm,),
        in_specs=[pl.BlockSpec((tm, d), lambda i: (i, 0)), _resident((1, d)),
                  _resident(wr.shape), _resident((1, LANES))],
        out_specs=[pl.BlockSpec((tm, LANES), lambda i: (i, 0)), pl.BlockSpec((tm, LANES), lambda i: (i, 0)),
                   pl.BlockSpec((8, LANES), lambda i: (0, 0))],
        out_shape=[jax.ShapeDtypeStruct((n, LANES), jnp.int32), jax.ShapeDtypeStruct((n, LANES), F32),
                   jax.ShapeDtypeStruct((8, LANES), jnp.int32)],
        scratch_shapes=[pltpu.VMEM((1, LANES), F32)],
        compiler_params=_cparams("arbitrary"),
        name="router",
    )(x2, g, wr, br)


def _dispatch_kernel(pend_ref, dest_ref, x_ref, xs_ref, zero_ref, sem, zsem):
    tm = dest_ref.shape[2] // TOP_K
    rows = zero_ref.shape[0]
    step = pl.program_id(0)

    @pl.when(step == 0)
    def _():
        zero_ref[...] = jnp.zeros_like(zero_ref)

        def zero_copy(e):
            start = pl.multiple_of(pend_ref[e] - rows, rows)
            return pltpu.make_async_copy(zero_ref, xs_ref.at[pl.ds(start, rows)], zsem)

        def has_rows(e):
            return pend_ref[e] > (pend_ref[e - 1] if e else 0)

        n_exp = pend_ref.shape[0]
        n_rows = xs_ref.shape[0]
        tail_starts = [n_rows - (j + 1) * rows for j in range(min(n_exp, n_rows // rows))]

        def tail_copy(start):
            return pltpu.make_async_copy(zero_ref, xs_ref.at[pl.ds(start, rows)], zsem)

        for e in range(n_exp):
            @pl.when(has_rows(e))
            def _():
                zero_copy(e).start()
        for start in tail_starts:
            @pl.when(start >= pend_ref[n_exp - 1])
            def _():
                tail_copy(start).start()
        for e in range(n_exp):
            @pl.when(has_rows(e))
            def _():
                zero_copy(e).wait()
        for start in tail_starts:
            @pl.when(start >= pend_ref[n_exp - 1])
            def _():
                tail_copy(start).wait()

    def row_copy(t, k):
        return pltpu.make_async_copy(x_ref.at[pl.ds(t, 1)], xs_ref.at[pl.ds(dest_ref[0, 0, TOP_K * t + k], 1)], sem)

    def start(t, carry):
        for k in range(TOP_K):
            row_copy(t, k).start()
        return carry

    def wait(t, carry):
        for k in range(TOP_K):
            row_copy(t, k).wait()
        return carry

    lax.fori_loop(0, tm, start, 0, unroll=True)
    lax.fori_loop(0, tm, wait, 0, unroll=8)


def _dispatch(pend, dest2, x2, *, n_rows, rows, tm=256):
    n, d = x2.shape
    grid_spec = pltpu.PrefetchScalarGridSpec(
        num_scalar_prefetch=1,
        grid=(n // tm,),
        in_specs=[pl.BlockSpec((1, 1, TOP_K * tm), lambda i, pe: (i, 0, 0), memory_space=pltpu.SMEM),
                  pl.BlockSpec((tm, d), lambda i, pe: (i, 0))],
        out_specs=pl.BlockSpec(memory_space=pl.ANY),
        scratch_shapes=[pltpu.VMEM((rows, d), F32), pltpu.SemaphoreType.DMA(()), pltpu.SemaphoreType.DMA(())],
    )
    return pl.pallas_call(
        _dispatch_kernel,
        grid_spec=grid_spec,
        out_shape=jax.ShapeDtypeStruct((n_rows, d), F32),
        compiler_params=_cparams("arbitrary"),
        name="moe_dispatch",
    )(pend, dest2, x2)


def _experts_kernel(blk_e_ref, nact_ref, xs_ref, g_ref, wg_ref, wu_ref, wd_ref, ys_ref, a_ref, *, ff_chunk):
    del blk_e_ref
    i = pl.program_id(0)

    @pl.when(i < nact_ref[0])
    def _():
        h = _rmsnorm(xs_ref[...], g_ref[...]).astype(BF16)
        for c in range(wg_ref.shape[2] // ff_chunk):
            sl = slice(c * ff_chunk, (c + 1) * ff_chunk)
            gate = jnp.dot(h, wg_ref[0, :, sl], preferred_element_type=F32)
            up = jnp.dot(h, wu_ref[0, :, sl], preferred_element_type=F32)
            a_ref[:, sl] = (gate * _sigmoid(gate) * up).astype(BF16)
        ys_ref[...] = jnp.dot(a_ref[...], wd_ref[0], preferred_element_type=F32)

    @pl.when(i >= nact_ref[0])
    def _():
        ys_ref[...] = jnp.zeros_like(ys_ref)


def _experts(blk_e, nact, xs, g, wg, wu, wd, *, rows, ff_chunk=256):
    n_rows, d = xs.shape
    ff = wg.shape[2]
    assert ff % ff_chunk == 0 and n_rows % rows == 0

    def blk(i, nact):
        return jnp.minimum(i, nact[0] - 1)

    def expert_weights(shape):
        return pl.BlockSpec(shape, lambda i, be, na: (be[blk(i, na)], 0, 0), pipeline_mode=pl.Buffered(1))

    grid_spec = pltpu.PrefetchScalarGridSpec(
        num_scalar_prefetch=2,
        grid=(n_rows // rows,),
        in_specs=[pl.BlockSpec((rows, d), lambda i, be, na: (blk(i, na), 0)),
                  pl.BlockSpec((1, d), lambda i, be, na: (0, 0)),
                  expert_weights((1, d, ff)), expert_weights((1, d, ff)), expert_weights((1, ff, d))],
        out_specs=pl.BlockSpec((rows, d), lambda i, be, na: (i, 0)),
        scratch_shapes=[pltpu.VMEM((rows, ff), BF16)],
    )
    return pl.pallas_call(
        functools.partial(_experts_kernel, ff_chunk=ff_chunk),
        grid_spec=grid_spec,
        out_shape=jax.ShapeDtypeStruct((n_rows, d), F32),
        compiler_params=_cparams("arbitrary"),
        name="moe_experts",
    )(blk_e, nact, xs, g, wg, wu, wd)


def _combine_kernel(dest_ref, next_dest_ref, x_ref, gt_ref, fg_ref, ys_ref, o_ref, buf, sems, *, final_norm):
    tm = x_ref.shape[0]
    step = pl.program_id(0)
    slot = step % 2

    def row_copy(dref, s, t, k):
        return pltpu.make_async_copy(ys_ref.at[pl.ds(dref[0, 0, TOP_K * t + k], 1)], buf.at[s, k, pl.ds(t, 1)],
                                     sems.at[s])

    def gather(dref, s):
        def start(t, carry):
            for k in range(TOP_K):
                row_copy(dref, s, t, k).start()
            return carry
        lax.fori_loop(0, tm, start, 0, unroll=True)

    @pl.when(step == 0)
    def _():
        gather(dest_ref, 0)

    @pl.when(step + 1 < pl.num_programs(0))
    def _():
        gather(next_dest_ref, 1 - slot)

    def wait(t, carry):
        for k in range(TOP_K):
            row_copy(dest_ref, slot, t, k).wait()
        return carry

    lax.fori_loop(0, tm, wait, 0, unroll=8)
    gt = gt_ref[...]
    y = x_ref[...]
    for k in range(TOP_K):
        y = y + buf[slot, k] * gt[:, k:k + 1]
    o_ref[...] = _rmsnorm(y, fg_ref[...]) if final_norm else y


def _combine(dest2, x2, gates, fg, ys, *, final_norm, tm=256):
    n, d = x2.shape
    steps = n // tm
    return pl.pallas_call(
        functools.partial(_combine_kernel, final_norm=final_norm),
        grid=(steps,),
        in_specs=[pl.BlockSpec((1, 1, TOP_K * tm), lambda i: (i, 0, 0), memory_space=pltpu.SMEM),
                  pl.BlockSpec((1, 1, TOP_K * tm), lambda i: (jnp.minimum(i + 1, steps - 1), 0, 0),
                               memory_space=pltpu.SMEM),
                  pl.BlockSpec((tm, d), lambda i: (i, 0)),
                  pl.BlockSpec((tm, LANES), lambda i: (i, 0)),
                  _resident((1, d)),
                  pl.BlockSpec(memory_space=pl.ANY)],
        out_specs=pl.BlockSpec((tm, d), lambda i: (i, 0)),
        out_shape=jax.ShapeDtypeStruct((n, d), F32),
        scratch_shapes=[pltpu.VMEM((2, TOP_K, tm, d), F32), pltpu.SemaphoreType.DMA((2,))],
        compiler_params=_cparams("arbitrary"),
        name="moe_combine",
    )(dest2, dest2, x2, gates, fg, ys)


def _final_norm_kernel(x_ref, g_ref, o_ref):
    o_ref[...] = _rmsnorm(x_ref[...], g_ref[...])


def _final_norm(x2, g, *, tm=512):
    n, d = x2.shape
    return pl.pallas_call(
        _final_norm_kernel,
        grid=(n // tm,),
        in_specs=[pl.BlockSpec((tm, d), lambda i: (i, 0)), _resident((1, d))],
        out_specs=pl.BlockSpec((tm, d), lambda i: (i, 0)),
        out_shape=jax.ShapeDtypeStruct((n, d), F32),
        compiler_params=_cparams("parallel"),
        name="final_norm",
    )(x2, g)


def _token_mixing(x3, norm_g, w_in, pool_w_grp, pool_scale, conv_w, b_i, b_f, ml_norm_g, fox_b_f,
                  w_br_pool, w_br_ml, w_br_fox, w_out):
    b, s, d = x3.shape
    pool_w = w_br_pool.shape[0]
    ml_w = w_br_ml.shape[0]
    fox_w = w_br_fox.shape[0]
    ml_dh = ml_w // ML_HEADS
    fox_dh = fox_w // FOX_HEADS
    n_small = 2 * ML_HEADS + FOX_HEADS

    sizes = (pool_w, ml_w, ml_w, ml_w, ml_w, ML_HEADS, ML_HEADS, fox_w, fox_w, fox_w, FOX_HEADS, N_BRANCH * d)
    offs = [0]
    for sz in sizes:
        offs.append(offs[-1] + sz)
    assert offs[-1] == w_in.shape[1]
    (o_pool, o_q, o_k, o_v, o_o, o_i, o_f, o_fq, o_fk, o_fv, o_ff, o_g) = offs[:-1]

    def cols(o, sz):
        return w_in[:, o:o + sz]

    wm = jnp.concatenate([cols(o_g, N_BRANCH * d), cols(o_q, ml_w), cols(o_k, ml_w), cols(o_v, ml_w),
                          cols(o_o, ml_w), cols(o_pool, pool_w), cols(o_fq, fox_w), cols(o_fk, fox_w),
                          cols(o_fv, fox_w)], axis=1).astype(BF16)
    c_gate = 0
    c_qk = c_gate + N_BRANCH * d
    c_v = c_qk + 2 * ml_w
    c_o = c_v + ml_w
    c_pool = c_o + ml_w
    c_fq = c_pool + pool_w
    c_fk = c_fq + fox_w
    c_fv = c_fk + fox_w
    ws = jnp.concatenate([cols(o_i, ML_HEADS), cols(o_f, ML_HEADS), cols(o_ff, FOX_HEADS),
                          jnp.zeros((d, LANES - n_small), w_in.dtype)], axis=1).astype(BF16)
    bias = jnp.concatenate([b_i, b_f, fox_b_f, jnp.zeros((LANES - n_small,), F32)]).reshape(1, LANES)

    proj, small = _inproj(x3.reshape(b * s, d), norm_g.reshape(1, d), wm, ws)
    p3 = proj.reshape(b, s, -1)
    gcol, grow, aq, ak = _gateprep(small.reshape(b, s, LANES), bias, fox_dh)
    fo = _fox(p3, aq, ak, q_col=c_fq, k_col=c_fk, v_col=c_fv, dh=fox_dh)
    hm = _mlstm(p3, gcol, grow, conv_w, ml_norm_g.reshape(1, ml_w), qk_col=c_qk, v_col=c_v, o_col=c_o, dh=ml_dh)
    return _merge(p3, hm, fo, x3, pool_w_grp.astype(BF16), pool_scale.reshape(1, pool_w),
                  w_br_pool.astype(BF16), w_br_ml.astype(BF16), w_br_fox.astype(BF16), w_out.astype(BF16),
                  pool_col=c_pool, gate_col=c_gate)


def _moe(x2, norm_g, w_router, b_router, w_gate, w_up, w_down, final_g, *, rows=512):
    n, d = x2.shape
    n_exp = w_router.shape[1]
    wr = jnp.concatenate([w_router.astype(F32), jnp.zeros((d, LANES - n_exp), F32)], axis=1)
    wr_hi = wr.astype(BF16)
    wr_lo = (wr - wr_hi.astype(F32)).astype(BF16)
    br = jnp.concatenate([b_router.astype(F32), jnp.full((LANES - n_exp,), NEG_BIG, F32)]).reshape(1, LANES)
    routed, gates, cnt = _router(x2, norm_g.reshape(1, d), jnp.concatenate([wr_hi, wr_lo, wr_hi], axis=0), br)

    e_flat = routed[:, :TOP_K].reshape(n * TOP_K)
    rank = routed[:, TOP_K:2 * TOP_K].reshape(n * TOP_K)
    counts = cnt[0, :n_exp]
    padded = ((counts + rows - 1) // rows) * rows
    pend = jnp.cumsum(padded)
    onehot = e_flat[:, None] == jnp.arange(n_exp, dtype=jnp.int32)[None, :]
    dest = (jnp.sum(jnp.where(onehot, (pend - padded)[None, :], 0), axis=1) + rank).astype(jnp.int32)
    n_blk = (n * TOP_K + n_exp * (rows - 1) + rows - 1) // rows
    blk_start = jnp.arange(n_blk, dtype=jnp.int32) * rows
    blk_e = jnp.minimum(jnp.sum(blk_start[:, None] >= pend[None, :], axis=1), n_exp - 1).astype(jnp.int32)
    nact = (pend[-1:] // rows).astype(jnp.int32)

    tm = 512
    dest2 = dest.reshape(n // tm, 1, TOP_K * tm)
    xs = _dispatch(pend.astype(jnp.int32), dest2, x2, n_rows=n_blk * rows, rows=rows, tm=tm)
    ys = _experts(blk_e, nact, xs, norm_g.reshape(1, d), w_gate.astype(BF16), w_up.astype(BF16),
                  w_down.astype(BF16), rows=rows)
    fg = jnp.ones((1, d), F32) if final_g is None else final_g.reshape(1, d)
    return _combine(dest2, x2, gates, fg, ys, final_norm=final_g is not None, tm=tm)


def kernel(x, mix_norm_g, w_in, pool_w_grp, pool_scale, ml_conv_w, ml_b_i, ml_b_f, ml_norm_g, fox_b_f,
           w_br_pool, w_br_ml, w_br_fox, w_out, ffn_norm_g, ff_w_gate, ff_w_up, ff_w_down,
           moe_w_router, moe_b_router, moe_w_gate, moe_w_up, moe_w_down, final_norm_g):
    b, s, d = x.shape
    depth = mix_norm_g.shape[0]
    fused_final = False
    for l in range(depth):
        x = _token_mixing(x, mix_norm_g[l], w_in[l], pool_w_grp[l], pool_scale[l], ml_conv_w[l], ml_b_i[l],
                          ml_b_f[l], ml_norm_g[l], fox_b_f[l], w_br_pool[l], w_br_ml[l], w_br_fox[l], w_out[l])
        x2 = x.reshape(b * s, d)
        if l % 2 == 0:
            x2 = _swiglu(x2, ffn_norm_g[l].reshape(1, d), ff_w_gate[l // 2].astype(BF16),
                         ff_w_up[l // 2].astype(BF16), ff_w_down[l // 2].astype(BF16))
        else:
            fused_final = l == depth - 1
            x2 = _moe(x2, ffn_norm_g[l], moe_w_router[l // 2], moe_b_router[l // 2], moe_w_gate[l // 2],
                      moe_w_up[l // 2], moe_w_down[l // 2], final_norm_g if fused_final else None)
        x = x2.reshape(b, s, d)
    if not fused_final:
        x = _final_norm(x.reshape(b * s, d), final_norm_g.reshape(1, d)).reshape(b, s, d)
    return x
```

```python
import functools
import math

import jax
import jax.numpy as jnp
import numpy as np
from jax import lax
from jax.experimental import pallas as pl
from jax.experimental.pallas import tpu as pltpu

F32 = jnp.float32
BF16 = jnp.bfloat16

NORM_EPS = 1e-6
POOL_WINDOWS = (2, 4, 8, 16)
ML_HEADS = 4
FOX_HEADS = 8
TOP_K = 2
N_BRANCH = 3

LANES = 128
BF16_SUBLANES = 16
VMEM_LIMIT_BYTES = 56 * 1024 * 1024

NEG_BIG = -1e30


def _cparams(*sem):
    return pltpu.CompilerParams(dimension_semantics=sem, vmem_limit_bytes=VMEM_LIMIT_BYTES)


def _resident(shape):
    zeros = (0,) * len(shape)
    return pl.BlockSpec(shape, lambda *_: zeros, pipeline_mode=pl.Buffered(1))


def _rmsnorm(x, g):
    return x * lax.rsqrt(jnp.mean(x * x, axis=-1, keepdims=True) + NORM_EPS) * g


def _sigmoid(x):
    return 1.0 / (1.0 + jnp.exp2(x * (-math.log2(math.e))))


def _log_sigmoid(x):
    return jnp.minimum(x, 0.0) - jnp.log(1.0 + jnp.exp(-jnp.abs(x)))


def _split3(x):
    hi = x.astype(BF16)
    r = x - hi.astype(F32)
    mid = r.astype(BF16)
    lo = (r - mid.astype(F32)).astype(BF16)
    return hi, mid, lo


def _cumsum_rows(tril, x):
    return sum(jnp.dot(tril, part, preferred_element_type=F32) for part in _split3(x))


def _inproj_kernel(x_ref, g_ref, wm_ref, ws_ref, p_ref, s_ref, *, col_chunk):
    h = _rmsnorm(x_ref[...], g_ref[...]).astype(BF16)
    s_ref[...] = jnp.dot(h, ws_ref[...], preferred_element_type=F32)
    for c in range(wm_ref.shape[1] // col_chunk):
        sl = slice(c * col_chunk, (c + 1) * col_chunk)
        p_ref[:, sl] = jnp.dot(h, wm_ref[:, sl], preferred_element_type=F32).astype(BF16)


def _inproj(x2, g, wm, ws, *, tm=512, col_chunk=1024):
    n, d = x2.shape
    wcols = wm.shape[1]
    return pl.pallas_call(
        functools.partial(_inproj_kernel, col_chunk=col_chunk),
        grid=(n // tm,),
        in_specs=[pl.BlockSpec((tm, d), lambda i: (i, 0)),
                  _resident((1, d)), _resident((d, wcols)), _resident((d, LANES))],
        out_specs=[pl.BlockSpec((tm, wcols), lambda i: (i, 0)),
                   pl.BlockSpec((tm, LANES), lambda i: (i, 0))],
        out_shape=[jax.ShapeDtypeStruct((n, wcols), BF16), jax.ShapeDtypeStruct((n, LANES), F32)],
        compiler_params=_cparams("parallel"),
        name="inproj",
    )(x2, g, wm, ws)


AUG_TERMS = 3
AUG_STRIDE = 8


def _aug_placement(dh):
    pairs = FOX_HEADS * dh // LANES
    width = pairs * LANES
    pq = np.zeros((AUG_TERMS * LANES, width), np.float32)
    pk = np.zeros((AUG_TERMS * LANES, width), np.float32)
    cq = np.zeros((1, width), np.float32)
    ck = np.zeros((1, width), np.float32)
    for h in range(FOX_HEADS):
        base = (h // 2) * LANES + (h % 2) * AUG_STRIDE
        for t in range(AUG_TERMS):
            src = t * LANES + 2 * ML_HEADS + h
            pq[src, base + t] = 1.0
            pk[src, base + AUG_TERMS + t] = -1.0
            cq[0, base + AUG_TERMS + t] = 1.0
            ck[0, base + t] = 1.0
    return jnp.asarray(pq, BF16), jnp.asarray(pk, BF16), jnp.asarray(cq), jnp.asarray(ck)


def _gateprep_kernel(s_ref, b_ref, pq_ref, pk_ref, cq_ref, ck_ref, col_ref, row_ref, aq_ref, ak_ref, carry_ref):
    t = s_ref.shape[1]

    @pl.when(pl.program_id(1) == 0)
    def _():
        carry_ref[...] = jnp.zeros_like(carry_ref)

    pre = s_ref[0] + b_ref[...]
    lane = lax.broadcasted_iota(jnp.int32, pre.shape, 1)
    ls = _log_sigmoid(pre)
    forget = (lane >= ML_HEADS) & (lane < 2 * ML_HEADS + FOX_HEADS)
    tril = (lax.broadcasted_iota(jnp.int32, (t, t), 0) >= lax.broadcasted_iota(jnp.int32, (t, t), 1)).astype(BF16)
    fcum = _cumsum_rows(tril, jnp.where(forget, ls, 0.0)) + carry_ref[...]
    carry_ref[...] = fcum[t - 1:t, :]
    col = jnp.where(lane < ML_HEADS, pre, fcum)
    col_ref[0] = col
    row_ref[0] = col.T[0:row_ref.shape[1], :]
    parts = jnp.concatenate(_split3(fcum), axis=-1)
    aq_ref[0] = (jnp.dot(parts, pq_ref[...], preferred_element_type=F32) + cq_ref[...]).astype(BF16)
    ak_ref[0] = (jnp.dot(parts, pk_ref[...], preferred_element_type=F32) + ck_ref[...]).astype(BF16)


def _gateprep(s3, bias, dh, *, tg=512):
    b, s, _ = s3.shape
    n_rows = 2 * ML_HEADS + FOX_HEADS
    pq, pk, cq, ck = _aug_placement(dh)
    width = pq.shape[1]
    return pl.pallas_call(
        _gateprep_kernel,
        grid=(b, s // tg),
        in_specs=[pl.BlockSpec((1, tg, LANES), lambda i, j: (i, j, 0)), _resident((1, LANES)),
                  _resident(pq.shape), _resident(pk.shape), _resident(cq.shape), _resident(ck.shape)],
        out_specs=[pl.BlockSpec((1, tg, LANES), lambda i, j: (i, j, 0)),
                   pl.BlockSpec((1, n_rows, tg), lambda i, j: (i, 0, j)),
                   pl.BlockSpec((1, tg, width), lambda i, j: (i, j, 0)),
                   pl.BlockSpec((1, tg, width), lambda i, j: (i, j, 0))],
        out_shape=[jax.ShapeDtypeStruct((b, s, LANES), F32), jax.ShapeDtypeStruct((b, n_rows, s), F32),
                   jax.ShapeDtypeStruct((b, s, width), BF16), jax.ShapeDtypeStruct((b, s, width), BF16)],
        scratch_shapes=[pltpu.VMEM((1, LANES), F32)],
        compiler_params=_cparams("parallel", "arbitrary"),
        name="gateprep",
    )(s3, bias, pq, pk, cq, ck)


def _fox_kernel(q_ref, aq_ref, k_ref, ak_ref, v_ref, o_ref, m_ref, acc_ref, *, tq, tk, diag_strips, dh, scale):
    qi = pl.program_id(2)
    lane = lax.broadcasted_iota(jnp.int32, (tq, LANES), 1)
    q2 = q_ref[0] * scale
    aq = aq_ref[0]
    in_head = [(lane >= a * dh) & (lane < (a + 1) * dh) for a in range(2)]
    q_ops = []
    for a in range(2):
        in_aug = (lane >= a * AUG_STRIDE) & (lane < a * AUG_STRIDE + 2 * AUG_TERMS)
        q_ops.append(jnp.concatenate([jnp.where(in_head[a], q2, jnp.zeros_like(q2)),
                                      jnp.where(in_aug, aq, jnp.zeros_like(aq))], axis=-1))
    m_ref[...] = jnp.full_like(m_ref, NEG_BIG)
    acc_ref[...] = jnp.zeros_like(acc_ref)
    def step(start, width, row0=0, masked=False):
        rows = slice(row0, tq)
        kb = jnp.concatenate([k_ref[0, pl.ds(start, width), :], ak_ref[0, pl.ds(start, width), :]], axis=-1)
        vb = v_ref[0, pl.ds(start, width), :]
        key_lane = lax.broadcasted_iota(jnp.int32, (width, LANES), 1)
        key_head = [(key_lane >= a * dh) & (key_lane < (a + 1) * dh) for a in range(2)]
        if masked:
            causal = (lax.broadcasted_iota(jnp.int32, (tq - row0, width), 1)
                      <= lax.broadcasted_iota(jnp.int32, (tq - row0, width), 0))
        for a in range(2):
            s = lax.dot_general(q_ops[a][rows], kb, (((1,), (1,)), ((), ())), preferred_element_type=F32)
            if masked:
                s = jnp.where(causal, s, NEG_BIG)
            m_prev = m_ref[a, rows]
            m_new = jnp.maximum(m_prev, jnp.max(s, axis=-1, keepdims=True))
            alpha = jnp.exp(m_prev - m_new)
            pexp = jnp.exp(s - jnp.concatenate([m_new] * (width // LANES), axis=-1))
            v_op = jnp.where(key_head[a], vb, jnp.ones_like(vb))
            acc_ref[a, rows] = alpha * acc_ref[a, rows] + jnp.dot(pexp.astype(BF16), v_op,
                                                                  preferred_element_type=F32)
            m_ref[a, rows] = m_new

    n_wide = (qi * tq) // tk
    n_narrow = qi - n_wide * (tk // tq)

    def wide_body(j, carry):
        step(pl.multiple_of(j * tk, tk), tk)
        return carry

    def narrow_body(j, carry):
        step(pl.multiple_of((n_wide * (tk // tq) + j) * tq, tq), tq)
        return carry

    lax.fori_loop(0, n_wide, wide_body, 0)
    if tk != tq:
        lax.fori_loop(0, n_narrow, narrow_body, 0)
    strip = tq // diag_strips
    for c in range(diag_strips):
        step(pl.multiple_of(qi * tq + c * strip, strip), strip, row0=c * strip, masked=True)
    outs = [acc_ref[a] / pltpu.roll(acc_ref[a], LANES // 2, 1) for a in range(2)]
    o_ref[0] = jnp.where(in_head[0], outs[0], outs[1]).astype(BF16)


def _fox(p3, aq, ak, *, q_col, k_col, v_col, dh, tq=1024, tk=2048, diag_strips=2):
    b, s, _ = p3.shape
    assert 2 * dh == LANES, "two heads share one 128-lane block"
    assert tk % tq == 0 and s % tq == 0
    pairs = FOX_HEADS * dh // LANES
    scale = dh ** -0.5
    assert math.frexp(scale)[0] == 0.5, "score scale is folded into bf16 q; exact only for a power of two"
    qb, kb, vb = q_col // LANES, k_col // LANES, v_col // LANES
    return pl.pallas_call(
        functools.partial(_fox_kernel, tq=tq, tk=tk, diag_strips=diag_strips, dh=dh, scale=scale),
        grid=(b, pairs, s // tq),
        in_specs=[pl.BlockSpec((1, tq, LANES), lambda i, p, q: (i, q, qb + p)),
                  pl.BlockSpec((1, tq, LANES), lambda i, p, q: (i, q, p)),
                  pl.BlockSpec((1, s, LANES), lambda i, p, q: (i, 0, kb + p)),
                  pl.BlockSpec((1, s, LANES), lambda i, p, q: (i, 0, p)),
                  pl.BlockSpec((1, s, LANES), lambda i, p, q: (i, 0, vb + p))],
        out_specs=pl.BlockSpec((1, tq, LANES), lambda i, p, q: (i, q, p)),
        out_shape=jax.ShapeDtypeStruct((b, s, pairs * LANES), BF16),
        scratch_shapes=[pltpu.VMEM((2, tq, LANES), F32), pltpu.VMEM((2, tq, LANES), F32)],
        compiler_params=_cparams("parallel", "parallel", "arbitrary"),
        name="fox_attention",
    )(p3, aq, p3, ak, p3)


def _mlstm_kernel(qk_ref, halo_ref, v_ref, og_ref, gc_ref, gr_ref, cw_ref, ng_ref, out_ref,
                  ubuf, ct_ref, n_ref, m_ref, g0_ref, *, chunk, dh):
    c = pl.program_id(1)
    width = ML_HEADS * dh
    taps = cw_ref.shape[0]
    pad = halo_ref.shape[1]
    assert 3 * ML_HEADS <= BF16_SUBLANES

    @pl.when(c == 0)
    def _():
        ct_ref[...] = jnp.zeros_like(ct_ref)
        n_ref[...] = jnp.zeros_like(n_ref)
        m_ref[...] = jnp.zeros_like(m_ref)
        g0_ref[...] = jnp.zeros_like(g0_ref)

    ubuf[0:pad, :] = jnp.where(c == 0, 0.0, halo_ref[0].astype(F32))
    ubuf[pad:pad + chunk, :] = qk_ref[0].astype(F32)

    def conv_silu(col0):
        y = sum(ubuf[pad - taps + 1 + j:pad - taps + 1 + j + chunk, col0:col0 + dh] * cw_ref[j:j + 1, col0:col0 + dh]
                for j in range(taps))
        return y * _sigmoid(y)

    ri = lax.broadcasted_iota(jnp.int32, (chunk, chunk), 0)
    ci = lax.broadcasted_iota(jnp.int32, (chunk, chunk), 1)
    visible = ri <= ci
    gc = gc_ref[0]
    gr = gr_ref[0]
    g0 = g0_ref[...]
    g0_ref[...] = gc[chunk - 1:chunk, :]
    contract_last = (((1,), (1,)), ((), ()))
    contract_first = (((0,), (0,)), ((), ()))
    part_row = lax.broadcasted_iota(jnp.int32, (BF16_SUBLANES, dh), 0)
    slab_row = lax.broadcasted_iota(jnp.int32, (BF16_SUBLANES, chunk), 0)
    ones_dh = jnp.ones((dh, dh), BF16)

    slab = jnp.zeros((BF16_SUBLANES, chunk), F32)
    heads = []
    for h in range(ML_HEADS):
        hs = slice(h * dh, (h + 1) * dh)
        q = conv_silu(h * dh)
        k = conv_silu(width + h * dh) * dh ** -0.5
        qb, kb = q.astype(BF16), k.astype(BF16)
        vb = v_ref[0, :, hs]
        g0_h = g0[:, ML_HEADS + h:ML_HEADS + h + 1]
        key_col = gc[:, h:h + 1] - (gc[:, ML_HEADS + h:ML_HEADS + h + 1] - g0_h)
        li_r = gr[h:h + 1, :]
        b_r = gr[ML_HEADS + h:ML_HEADS + h + 1, :] - g0_h
        b_last = b_r[:, chunk - 1:chunk]
        m_prev = m_ref[h:h + 1, 0:1]

        d_t = jnp.where(visible, b_r + key_col, NEG_BIG)
        m_inter = b_r + m_prev
        m_t = jnp.maximum(m_inter, jnp.max(d_t, axis=0, keepdims=True))
        w_inter = jnp.exp(m_inter - m_t)
        p_t = jnp.exp(d_t - m_t) * lax.dot_general(kb, qb, contract_last, preferred_element_type=F32)
        n_hi, n_mid, n_lo = (part.astype(F32) for part in _split3(n_ref[h]))
        n_parts = jnp.where(part_row == 0, n_hi, jnp.where(part_row == 1, n_mid,
                                                           jnp.where(part_row == 2, n_lo, 0.0))).astype(BF16)
        nq = jnp.sum(lax.dot_general(n_parts, qb, contract_last, preferred_element_type=F32), axis=0, keepdims=True)
        den = w_inter * nq + jnp.sum(p_t, axis=0, keepdims=True)
        inv = 1.0 / jnp.maximum(jnp.abs(den), jnp.exp(-m_t))

        g_r = b_last - b_r + li_r
        m_new = jnp.maximum(b_last + m_prev, jnp.max(g_r, axis=-1, keepdims=True))
        decay = jnp.exp(b_last + m_prev - m_new)
        wk_r = jnp.exp(g_r - m_new)
        for slot, vec in enumerate((w_inter, inv, wk_r)):
            slab = jnp.where(slab_row == 3 * h + slot, vec, slab)
        heads.append((hs, k, qb, vb, p_t.astype(BF16), decay, m_new))

    cols = jnp.concatenate([slab, jnp.zeros((LANES - BF16_SUBLANES, chunk), F32)], axis=0).T

    for h, (hs, k, qb, vb, p_tb, decay, m_new) in enumerate(heads):
        w_col, inv_col, wk_col = (cols[:, 3 * h + slot:3 * h + slot + 1] for slot in range(3))
        num = (w_col * jnp.dot(qb, ct_ref[h].astype(BF16), preferred_element_type=F32)
               + lax.dot_general(p_tb, vb, contract_first, preferred_element_type=F32))
        hh = num * inv_col
        kw = k * wk_col
        ct_ref[h] = decay * ct_ref[h] + lax.dot_general(kw.astype(BF16), vb, contract_first,
                                                       preferred_element_type=F32)
        n_ref[h] = decay * n_ref[h] + jnp.sum(kw, axis=0, keepdims=True)
        m_ref[h:h + 1, :] = jnp.broadcast_to(m_new, (1, LANES))

        sq = hh * hh
        sq_hi = sq.astype(BF16)
        sq_lo = (sq - sq_hi.astype(F32)).astype(BF16)
        mean_sq = (jnp.dot(sq_hi, ones_dh, preferred_element_type=F32)
                   + jnp.dot(sq_lo, ones_dh, preferred_element_type=F32)) * (1.0 / dh)
        hn = hh * lax.rsqrt(mean_sq + NORM_EPS) * ng_ref[:, hs]
        out_ref[0, :, hs] = (hn * _sigmoid(og_ref[0, :, hs].astype(F32))).astype(BF16)


def _mlstm(p3, gcol, grow, conv_w, norm_g, *, qk_col, v_col, o_col, dh, chunk=256):
    b, s, _ = p3.shape
    width = ML_HEADS * dh
    pad = BF16_SUBLANES
    assert conv_w.shape[0] - 1 <= pad
    qkb, vb, ob = qk_col // (2 * width), v_col // width, o_col // width
    halo_per_chunk = chunk // pad
    return pl.pallas_call(
        functools.partial(_mlstm_kernel, chunk=chunk, dh=dh),
        grid=(b, s // chunk),
        in_specs=[pl.BlockSpec((1, chunk, 2 * width), lambda i, c: (i, c, qkb)),
                  pl.BlockSpec((1, pad, 2 * width),
                               lambda i, c: (i, jnp.maximum(c * halo_per_chunk - 1, 0), qkb)),
                  pl.BlockSpec((1, chunk, width), lambda i, c: (i, c, vb)),
                  pl.BlockSpec((1, chunk, width), lambda i, c: (i, c, ob)),
                  pl.BlockSpec((1, chunk, LANES), lambda i, c: (i, c, 0)),
                  pl.BlockSpec((1, grow.shape[1], chunk), lambda i, c: (i, 0, c)),
                  _resident(conv_w.shape), _resident((1, width))],
        out_specs=pl.BlockSpec((1, chunk, width), lambda i, c: (i, c, 0)),
        out_shape=jax.ShapeDtypeStruct((b, s, width), BF16),
        scratch_shapes=[pltpu.VMEM((pad + chunk, 2 * width), F32),
                        pltpu.VMEM((ML_HEADS, dh, dh), F32),
                        pltpu.VMEM((ML_HEADS, 1, dh), F32),
                        pltpu.VMEM((ML_HEADS, LANES), F32),
                        pltpu.VMEM((1, LANES), F32)],
        compiler_params=_cparams("parallel", "arbitrary"),
        name="mlstm",
    )(p3, p3, p3, p3, gcol, grow, conv_w, norm_g)


def _merge_kernel(u_ref, halo_ref, hm_ref, fo_ref, gp_ref, x_ref, wg_ref, ps_ref, wbp_ref, wbm_ref, wbf_ref,
                  wo_ref, o_ref, ubuf, sbuf, *, tm):
    j = pl.program_id(1)
    pad = halo_ref.shape[1]
    gw = wg_ref.shape[1]
    pw = ubuf.shape[1]
    d = x_ref.shape[2]
    margin = ubuf.shape[0] - pad - tm
    base = margin + pad
    ext = pad + tm
    levels = len(POOL_WINDOWS)
    assert all(w == 2 ** (g + 1) for g, w in enumerate(POOL_WINDOWS)) and 2 ** (levels - 1) <= margin

    ubuf[0:margin, :] = jnp.zeros((margin, pw), F32)
    ubuf[margin:base, :] = jnp.where(j == 0, 0.0, halo_ref[0].astype(F32))
    ubuf[base:base + tm, :] = u_ref[0].astype(F32)
    wsums = []
    prev = ubuf
    for lvl in range(1, levels + 1):
        shift = 2 ** (lvl - 1)
        lanes = slice((lvl - 1) * gw, pw)
        if lvl < levels:
            cur = sbuf.at[lvl - 1]
            cur[0:margin, lanes] = jnp.zeros((margin, pw - (lvl - 1) * gw), F32)
            cur[margin:margin + ext, lanes] = (prev[margin:margin + ext, lanes]
                                               + prev[margin - shift:margin - shift + ext, lanes])
            wsums.append(cur[base:base + tm, (lvl - 1) * gw:lvl * gw])
            prev = cur
        else:
            wsums.append(prev[base:base + tm, lanes] + prev[base - shift:base - shift + tm, lanes])
    pos = j * tm + lax.broadcasted_iota(jnp.int32, (tm, 1), 0)
    ys = []
    for g, w in enumerate(POOL_WINDOWS):
        gs = slice(g * gw, (g + 1) * gw)
        cnt = jnp.minimum(pos + 1, w).astype(F32)
        dlt = wsums[g] / cnt - ubuf[base:base + tm, gs]
        ys.append(jnp.dot(dlt.astype(BF16), wg_ref[g], preferred_element_type=F32))
    y_pool = (jnp.concatenate(ys, axis=-1) * ps_ref[...]).astype(BF16)
    merged = _sigmoid(gp_ref[0, :, 0:d].astype(F32)) * jnp.dot(y_pool, wbp_ref[...], preferred_element_type=F32)
    merged += _sigmoid(gp_ref[0, :, d:2 * d].astype(F32)) * jnp.dot(hm_ref[0], wbm_ref[...],
                                                                   preferred_element_type=F32)
    merged += _sigmoid(gp_ref[0, :, 2 * d:3 * d].astype(F32)) * jnp.dot(fo_ref[0], wbf_ref[...],
                                                                       preferred_element_type=F32)
    o_ref[0] = x_ref[0] + jnp.dot(merged.astype(BF16), wo_ref[...], preferred_element_type=F32)


def _merge(p3, hm, fo, x3, wgrp, pscale, wbp, wbm, wbf, wo, *, pool_col, gate_col, tm=512):
    b, s, d = x3.shape
    pw = wbp.shape[0]
    pad = BF16_SUBLANES
    margin = 8
    assert max(POOL_WINDOWS) - 1 <= pad and gate_col == 0
    pcb = pool_col // pw
    halo_per_blk = tm // pad
    return pl.pallas_call(
        functools.partial(_merge_kernel, tm=tm),
        grid=(b, s // tm),
        in_specs=[pl.BlockSpec((1, tm, pw), lambda i, j: (i, j, pcb)),
                  pl.BlockSpec((1, pad, pw), lambda i, j: (i, jnp.maximum(j * halo_per_blk - 1, 0), pcb)),
                  pl.BlockSpec((1, tm, hm.shape[2]), lambda i, j: (i, j, 0)),
                  pl.BlockSpec((1, tm, fo.shape[2]), lambda i, j: (i, j, 0)),
                  pl.BlockSpec((1, tm, N_BRANCH * d), lambda i, j: (i, j, 0)),
                  pl.BlockSpec((1, tm, d), lambda i, j: (i, j, 0)),
                  _resident(wgrp.shape), _resident(pscale.shape), _resident(wbp.shape),
                  _resident(wbm.shape), _resident(wbf.shape), _resident(wo.shape)],
        out_specs=pl.BlockSpec((1, tm, d), lambda i, j: (i, j, 0)),
        out_shape=jax.ShapeDtypeStruct((b, s, d), F32),
        scratch_shapes=[pltpu.VMEM((margin + pad + tm, pw), F32),
                        pltpu.VMEM((len(POOL_WINDOWS) - 1, margin + pad + tm, pw), F32)],
        compiler_params=_cparams("parallel", "parallel"),
        name="merge",
    )(p3, p3, hm, fo, p3, x3, wgrp, pscale, wbp, wbm, wbf, wo)


def _swiglu_kernel(x_ref, g_ref, wg_ref, wu_ref, wd_ref, o_ref, a_ref, *, ff_chunk):
    x = x_ref[...]
    h = _rmsnorm(x, g_ref[...]).astype(BF16)
    for c in range(wg_ref.shape[1] // ff_chunk):
        sl = slice(c * ff_chunk, (c + 1) * ff_chunk)
        gate = jnp.dot(h, wg_ref[:, sl], preferred_element_type=F32)
        up = jnp.dot(h, wu_ref[:, sl], preferred_element_type=F32)
        a_ref[:, sl] = (gate * _sigmoid(gate) * up).astype(BF16)
    o_ref[...] = x + jnp.dot(a_ref[...], wd_ref[...], preferred_element_type=F32)


def _swiglu(x2, g, wg, wu, wd, *, tm=512, ff_chunk=256):
    n, d = x2.shape
    ff = wg.shape[1]
    assert ff % ff_chunk == 0
    return pl.pallas_call(
        functools.partial(_swiglu_kernel, ff_chunk=ff_chunk),
        grid=(n // tm,),
        in_specs=[pl.BlockSpec((tm, d), lambda i: (i, 0)), _resident((1, d)),
                  _resident(wg.shape), _resident(wu.shape), _resident(wd.shape)],
        out_specs=pl.BlockSpec((tm, d), lambda i: (i, 0)),
        out_shape=jax.ShapeDtypeStruct((n, d), F32),
        scratch_shapes=[pltpu.VMEM((tm, ff), BF16)],
        compiler_params=_cparams("parallel"),
        name="dense_swiglu",
    )(x2, g, wg, wu, wd)


def _router_kernel(x_ref, g_ref, wr_ref, br_ref, e_ref, gt_ref, cnt_ref, carry_ref):
    h = _rmsnorm(x_ref[...], g_ref[...])
    h_hi = h.astype(BF16)
    h_lo = (h - h_hi.astype(F32)).astype(BF16)
    logits = jnp.dot(jnp.concatenate([h_hi, h_hi, h_lo], axis=-1), wr_ref[...],
                     preferred_element_type=F32) + br_ref[...]
    lane = lax.broadcasted_iota(jnp.int32, logits.shape, 1)
    m1 = jnp.max(logits, axis=-1, keepdims=True)
    i1 = jnp.min(jnp.where(logits == m1, lane, LANES), axis=-1, keepdims=True)
    rest = jnp.where(lane == i1, NEG_BIG, logits)
    m2 = jnp.max(rest, axis=-1, keepdims=True)
    i2 = jnp.min(jnp.where(rest == m2, lane, LANES), axis=-1, keepdims=True)
    e2 = jnp.exp(m2 - m1)
    g1 = 1.0 / (1.0 + e2)
    gt_ref[...] = jnp.where(lane == 0, g1, jnp.where(lane == 1, e2 * g1, 0.0))

    @pl.when(pl.program_id(0) == 0)
    def _():
        carry_ref[...] = jnp.zeros_like(carry_ref)

    tm = logits.shape[0]
    pick0, pick1 = lane == i1, lane == i2
    picks = jnp.where(pick0 | pick1, 1.0, 0.0)
    earlier = (lax.broadcasted_iota(jnp.int32, (tm, tm), 0) > lax.broadcasted_iota(jnp.int32, (tm, tm), 1))
    prior = jnp.dot(earlier.astype(BF16), picks.astype(BF16), preferred_element_type=F32) + carry_ref[...]
    r0 = jnp.sum(jnp.where(pick0, prior, 0.0), axis=-1, keepdims=True).astype(jnp.int32)
    r1 = jnp.sum(jnp.where(pick1, prior, 0.0), axis=-1, keepdims=True).astype(jnp.int32)
    carry_ref[...] = prior[tm - 1:tm, :] + picks[tm - 1:tm, :]
    cnt_ref[...] = jnp.broadcast_to(carry_ref[...], cnt_ref.shape).astype(jnp.int32)
    e_ref[...] = jnp.where(lane == 0, i1, jnp.where(lane == 1, i2, jnp.where(lane == 2, r0,
                                                                          jnp.where(lane == 3, r1, 0))))


def _router(x2, g, wr, br, *, tm=512):
    n, d = x2.shape
    return pl.pallas_call(
        _router_kernel,
        grid=(n // tm,),
        in_specs=[pl.BlockSpec((tm, d), lambda i: (i, 0)), _resident((1, d)),
                  _resident(wr.shape), _resident((1, LANES))],
        out_specs=[pl.BlockSpec((tm, LANES), lambda i: (i, 0)), pl.BlockSpec((tm, LANES), lambda i: (i, 0)),
                   pl.BlockSpec((8, LANES), lambda i: (0, 0))],
        out_shape=[jax.ShapeDtypeStruct((n, LANES), jnp.int32), jax.ShapeDtypeStruct((n, LANES), F32),
                   jax.ShapeDtypeStruct((8, LANES), jnp.int32)],
        scratch_shapes=[pltpu.VMEM((1, LANES), F32)],
        compiler_params=_cparams("arbitrary"),
        name="router",
    )(x2, g, wr, br)


def _dispatch_kernel(pend_ref, dest_ref, x_ref, xs_ref, zero_ref, xbuf, sems, zsem):
    tm = dest_ref.shape[2] // TOP_K
    rows = zero_ref.shape[0]
    step = pl.program_id(0)

    @pl.when(step == 0)
    def _():
        zero_ref[...] = jnp.zeros_like(zero_ref)

        def zero_copy(e):
            start = pl.multiple_of(pend_ref[e] - rows, rows)
            return pltpu.make_async_copy(zero_ref, xs_ref.at[pl.ds(start, rows)], zsem)

        def has_rows(e):
            return pend_ref[e] > (pend_ref[e - 1] if e else 0)

        n_exp = pend_ref.shape[0]
        n_rows = xs_ref.shape[0]
        tail_starts = [n_rows - (j + 1) * rows for j in range(min(n_exp, n_rows // rows))]

        def tail_copy(start):
            return pltpu.make_async_copy(zero_ref, xs_ref.at[pl.ds(start, rows)], zsem)

        for e in range(n_exp):
            @pl.when(has_rows(e))
            def _():
                zero_copy(e).start()
        for start in tail_starts:
            @pl.when(start >= pend_ref[n_exp - 1])
            def _():
                tail_copy(start).start()
        for e in range(n_exp):
            @pl.when(has_rows(e))
            def _():
                zero_copy(e).wait()
        for start in tail_starts:
            @pl.when(start >= pend_ref[n_exp - 1])
            def _():
                tail_copy(start).wait()

    def issue(s):
        xbuf[s] = x_ref[...]

        def start(t, carry):
            for k in range(TOP_K):
                pltpu.make_async_copy(xbuf.at[s, pl.ds(t, 1)],
                                      xs_ref.at[pl.ds(dest_ref[0, 0, TOP_K * t + k], 1)], sems.at[s]).start()
            return carry

        lax.fori_loop(0, tm, start, 0, unroll=True)

    def drain(s):
        def wait(t, carry):
            pltpu.make_async_copy(xbuf.at[s, pl.ds(0, 1)], xs_ref.at[pl.ds(0, 1)], sems.at[s]).wait()
            return carry

        lax.fori_loop(0, TOP_K * tm, wait, 0, unroll=8)

    last = pl.num_programs(0) - 1
    for s in range(2):
        @pl.when(step % 2 == s)
        def _():
            issue(s)

            @pl.when(step > 0)
            def _():
                drain(1 - s)

            @pl.when(step == last)
            def _():
                drain(s)


def _dispatch(pend, dest2, x2, *, n_rows, rows, tm=256):
    n, d = x2.shape
    grid_spec = pltpu.PrefetchScalarGridSpec(
        num_scalar_prefetch=1,
        grid=(n // tm,),
        in_specs=[pl.BlockSpec((1, 1, TOP_K * tm), lambda i, pe: (i, 0, 0), memory_space=pltpu.SMEM),
                  pl.BlockSpec((tm, d), lambda i, pe: (i, 0))],
        out_specs=pl.BlockSpec(memory_space=pl.ANY),
        scratch_shapes=[pltpu.VMEM((rows, d), F32), pltpu.VMEM((2, tm, d), F32),
                        pltpu.SemaphoreType.DMA((2,)), pltpu.SemaphoreType.DMA(())],
    )
    return pl.pallas_call(
        _dispatch_kernel,
        grid_spec=grid_spec,
        out_shape=jax.ShapeDtypeStruct((n_rows, d), F32),
        compiler_params=_cparams("arbitrary"),
        name="moe_dispatch",
    )(pend, dest2, x2)


def _experts_kernel(blk_e_ref, nact_ref, xs_ref, g_ref, wg_ref, wu_ref, wd_ref, ys_ref, a_ref, *, ff_chunk):
    del blk_e_ref
    i = pl.program_id(0)

    @pl.when(i < nact_ref[0])
    def _():
        h = _rmsnorm(xs_ref[...], g_ref[...]).astype(BF16)
        for c in range(wg_ref.shape[2] // ff_chunk):
            sl = slice(c * ff_chunk, (c + 1) * ff_chunk)
            gate = jnp.dot(h, wg_ref[0, :, sl], preferred_element_type=F32)
            up = jnp.dot(h, wu_ref[0, :, sl], preferred_element_type=F32)
            a_ref[:, sl] = (gate * _sigmoid(gate) * up).astype(BF16)
        ys_ref[...] = jnp.dot(a_ref[...], wd_ref[0], preferred_element_type=F32)

    @pl.when(i >= nact_ref[0])
    def _():
        ys_ref[...] = jnp.zeros_like(ys_ref)


def _experts(blk_e, nact, xs, g, wg, wu, wd, *, rows, ff_chunk=256):
    n_rows, d = xs.shape
    ff = wg.shape[2]
    assert ff % ff_chunk == 0 and n_rows % rows == 0

    def blk(i, nact):
        return jnp.minimum(i, nact[0] - 1)

    def expert_weights(shape, buffers):
        return pl.BlockSpec(shape, lambda i, be, na: (be[blk(i, na)], 0, 0), pipeline_mode=pl.Buffered(buffers))

    grid_spec = pltpu.PrefetchScalarGridSpec(
        num_scalar_prefetch=2,
        grid=(n_rows // rows,),
        in_specs=[pl.BlockSpec((rows, d), lambda i, be, na: (blk(i, na), 0)),
                  pl.BlockSpec((1, d), lambda i, be, na: (0, 0)),
                  expert_weights((1, d, ff), 2), expert_weights((1, d, ff), 2), expert_weights((1, ff, d), 1)],
        out_specs=pl.BlockSpec((rows, d), lambda i, be, na: (i, 0)),
        scratch_shapes=[pltpu.VMEM((rows, ff), BF16)],
    )
    return pl.pallas_call(
        functools.partial(_experts_kernel, ff_chunk=ff_chunk),
        grid_spec=grid_spec,
        out_shape=jax.ShapeDtypeStruct((n_rows, d), F32),
        compiler_params=_cparams("arbitrary"),
        name="moe_experts",
    )(blk_e, nact, xs, g, wg, wu, wd)


def _combine_kernel(dest_ref, next_dest_ref, x_ref, gt_ref, fg_ref, ys_ref, o_ref, buf, sems, *, final_norm):
    tm = x_ref.shape[0]
    step = pl.program_id(0)
    slot = step % 2

    def row_copy(dref, s, t, k):
        return pltpu.make_async_copy(ys_ref.at[pl.ds(dref[0, 0, TOP_K * t + k], 1)], buf.at[s, k, pl.ds(t, 1)],
                                     sems.at[s])

    def gather(dref, s):
        def start(t, carry):
            for k in range(TOP_K):
                row_copy(dref, s, t, k).start()
            return carry
        lax.fori_loop(0, tm, start, 0, unroll=True)

    @pl.when(step == 0)
    def _():
        gather(dest_ref, 0)

    @pl.when(step + 1 < pl.num_programs(0))
    def _():
        gather(next_dest_ref, 1 - slot)

    def wait(t, carry):
        for k in range(TOP_K):
            row_copy(dest_ref, slot, t, k).wait()
        return carry

    lax.fori_loop(0, tm, wait, 0, unroll=8)
    gt = gt_ref[...]
    y = x_ref[...]
    for k in range(TOP_K):
        y = y + buf[slot, k] * gt[:, k:k + 1]
    o_ref[...] = _rmsnorm(y, fg_ref[...]) if final_norm else y


def _combine(dest2, x2, gates, fg, ys, *, final_norm, tm=256):
    n, d = x2.shape
    steps = n // tm
    return pl.pallas_call(
        functools.partial(_combine_kernel, final_norm=final_norm),
        grid=(steps,),
        in_specs=[pl.BlockSpec((1, 1, TOP_K * tm), lambda i: (i, 0, 0), memory_space=pltpu.SMEM),
                  pl.BlockSpec((1, 1, TOP_K * tm), lambda i: (jnp.minimum(i + 1, steps - 1), 0, 0),
                               memory_space=pltpu.SMEM),
                  pl.BlockSpec((tm, d), lambda i: (i, 0)),
                  pl.BlockSpec((tm, LANES), lambda i: (i, 0)),
                  _resident((1, d)),
                  pl.BlockSpec(memory_space=pl.ANY)],
        out_specs=pl.BlockSpec((tm, d), lambda i: (i, 0)),
        out_shape=jax.ShapeDtypeStruct((n, d), F32),
        scratch_shapes=[pltpu.VMEM((2, TOP_K, tm, d), F32), pltpu.SemaphoreType.DMA((2,))],
        compiler_params=_cparams("arbitrary"),
        name="moe_combine",
    )(dest2, dest2, x2, gates, fg, ys)


def _final_norm_kernel(x_ref, g_ref, o_ref):
    o_ref[...] = _rmsnorm(x_ref[...], g_ref[...])


def _final_norm(x2, g, *, tm=512):
    n, d = x2.shape
    return pl.pallas_call(
        _final_norm_kernel,
        grid=(n // tm,),
        in_specs=[pl.BlockSpec((tm, d), lambda i: (i, 0)), _resident((1, d))],
        out_specs=pl.BlockSpec((tm, d), lambda i: (i, 0)),
        out_shape=jax.ShapeDtypeStruct((n, d), F32),
        compiler_params=_cparams("parallel"),
        name="final_norm",
    )(x2, g)


def _token_mixing(x3, norm_g, w_in, pool_w_grp, pool_scale, conv_w, b_i, b_f, ml_norm_g, fox_b_f,
                  w_br_pool, w_br_ml, w_br_fox, w_out):
    b, s, d = x3.shape
    pool_w = w_br_pool.shape[0]
    ml_w = w_br_ml.shape[0]
    fox_w = w_br_fox.shape[0]
    ml_dh = ml_w // ML_HEADS
    fox_dh = fox_w // FOX_HEADS
    n_small = 2 * ML_HEADS + FOX_HEADS

    sizes = (pool_w, ml_w, ml_w, ml_w, ml_w, ML_HEADS, ML_HEADS, fox_w, fox_w, fox_w, FOX_HEADS, N_BRANCH * d)
    offs = [0]
    for sz in sizes:
        offs.append(offs[-1] + sz)
    assert offs[-1] == w_in.shape[1]
    (o_pool, o_q, o_k, o_v, o_o, o_i, o_f, o_fq, o_fk, o_fv, o_ff, o_g) = offs[:-1]

    def cols(o, sz):
        return w_in[:, o:o + sz]

    wm = jnp.concatenate([cols(o_g, N_BRANCH * d), cols(o_q, ml_w), cols(o_k, ml_w), cols(o_v, ml_w),
                          cols(o_o, ml_w), cols(o_pool, pool_w), cols(o_fq, fox_w), cols(o_fk, fox_w),
                          cols(o_fv, fox_w)], axis=1).astype(BF16)
    c_gate = 0
    c_qk = c_gate + N_BRANCH * d
    c_v = c_qk + 2 * ml_w
    c_o = c_v + ml_w
    c_pool = c_o + ml_w
    c_fq = c_pool + pool_w
    c_fk = c_fq + fox_w
    c_fv = c_fk + fox_w
    ws = jnp.concatenate([cols(o_i, ML_HEADS), cols(o_f, ML_HEADS), cols(o_ff, FOX_HEADS),
                          jnp.zeros((d, LANES - n_small), w_in.dtype)], axis=1).astype(BF16)
    bias = jnp.concatenate([b_i, b_f, fox_b_f, jnp.zeros((LANES - n_small,), F32)]).reshape(1, LANES)

    proj, small = _inproj(x3.reshape(b * s, d), norm_g.reshape(1, d), wm, ws)
    p3 = proj.reshape(b, s, -1)
    gcol, grow, aq, ak = _gateprep(small.reshape(b, s, LANES), bias, fox_dh)
    fo = _fox(p3, aq, ak, q_col=c_fq, k_col=c_fk, v_col=c_fv, dh=fox_dh)
    hm = _mlstm(p3, gcol, grow, conv_w, ml_norm_g.reshape(1, ml_w), qk_col=c_qk, v_col=c_v, o_col=c_o, dh=ml_dh)
    return _merge(p3, hm, fo, x3, pool_w_grp.astype(BF16), pool_scale.reshape(1, pool_w),
                  w_br_pool.astype(BF16), w_br_ml.astype(BF16), w_br_fox.astype(BF16), w_out.astype(BF16),
                  pool_col=c_pool, gate_col=c_gate)


def _moe(x2, norm_g, w_router, b_router, w_gate, w_up, w_down, final_g, *, rows=512):
    n, d = x2.shape
    n_exp = w_router.shape[1]
    wr = jnp.concatenate([w_router.astype(F32), jnp.zeros((d, LANES - n_exp), F32)], axis=1)
    wr_hi = wr.astype(BF16)
    wr_lo = (wr - wr_hi.astype(F32)).astype(BF16)
    br = jnp.concatenate([b_router.astype(F32), jnp.full((LANES - n_exp,), NEG_BIG, F32)]).reshape(1, LANES)
    routed, gates, cnt = _router(x2, norm_g.reshape(1, d), jnp.concatenate([wr_hi, wr_lo, wr_hi], axis=0), br)

    e_flat = routed[:, :TOP_K].reshape(n * TOP_K)
    rank = routed[:, TOP_K:2 * TOP_K].reshape(n * TOP_K)
    counts = cnt[0, :n_exp]
    padded = ((counts + rows - 1) // rows) * rows
    pend = jnp.cumsum(padded)
    onehot = e_flat[:, None] == jnp.arange(n_exp, dtype=jnp.int32)[None, :]
    dest = (jnp.sum(jnp.where(onehot, (pend - padded)[None, :], 0), axis=1) + rank).astype(jnp.int32)
    n_blk = (n * TOP_K + n_exp * (rows - 1) + rows - 1) // rows
    blk_start = jnp.arange(n_blk, dtype=jnp.int32) * rows
    blk_e = jnp.minimum(jnp.sum(blk_start[:, None] >= pend[None, :], axis=1), n_exp - 1).astype(jnp.int32)
    nact = (pend[-1:] // rows).astype(jnp.int32)

    tm = 512
    dest2 = dest.reshape(n // tm, 1, TOP_K * tm)
    xs = _dispatch(pend.astype(jnp.int32), dest2, x2, n_rows=n_blk * rows, rows=rows, tm=tm)
    ys = _experts(blk_e, nact, xs, norm_g.reshape(1, d), w_gate.astype(BF16), w_up.astype(BF16),
                  w_down.astype(BF16), rows=rows)
    fg = jnp.ones((1, d), F32) if final_g is None else final_g.reshape(1, d)
    return _combine(dest2, x2, gates, fg, ys, final_norm=final_g is not None, tm=tm)


def kernel(x, mix_norm_g, w_in, pool_w_grp, pool_scale, ml_conv_w, ml_b_i, ml_b_f, ml_norm_g, fox_b_f,
           w_br_pool, w_br_ml, w_br_fox, w_out, ffn_norm_g, ff_w_gate, ff_w_up, ff_w_down,
           moe_w_router, moe_b_router, moe_w_gate, moe_w_up, moe_w_down, final_norm_g):
    b, s, d = x.shape
    depth = mix_norm_g.shape[0]
    fused_final = False
    for l in range(depth):
        x = _token_mixing(x, mix_norm_g[l], w_in[l], pool_w_grp[l], pool_scale[l], ml_conv_w[l], ml_b_i[l],
                          ml_b_f[l], ml_norm_g[l], fox_b_f[l], w_br_pool[l], w_br_ml[l], w_br_fox[l], w_out[l])
        x2 = x.reshape(b * s, d)
        if l % 2 == 0:
            x2 = _swiglu(x2, ffn_norm_g[l].reshape(1, d), ff_w_gate[l // 2].astype(BF16),
                         ff_w_up[l // 2].astype(BF16), ff_w_down[l // 2].astype(BF16))
        else:
            fused_final = l == depth - 1
            x2 = _moe(x2, ffn_norm_g[l], moe_w_router[l // 2], moe_b_router[l // 2], moe_w_gate[l // 2],
                      moe_w_up[l // 2], moe_w_down[l // 2], final_norm_g if fused_final else None)
        x = x2.reshape(b, s, d)
    if not fused_final:
        x = _final_norm(x.reshape(b * s, d), final_norm_g.reshape(1, d)).reshape(b, s, d)
    return x
```

```python
import functools
import math

import jax
import jax.numpy as jnp
import numpy as np
from jax import lax
from jax.experimental import pallas as pl
from jax.experimental.pallas import tpu as pltpu

F32 = jnp.float32
BF16 = jnp.bfloat16

NORM_EPS = 1e-6
POOL_WINDOWS = (2, 4, 8, 16)
ML_HEADS = 4
FOX_HEADS = 8
TOP_K = 2
N_BRANCH = 3

LANES = 128
BF16_SUBLANES = 16
VMEM_LIMIT_BYTES = 56 * 1024 * 1024

NEG_BIG = -1e30


def _cparams(*sem):
    return pltpu.CompilerParams(dimension_semantics=sem, vmem_limit_bytes=VMEM_LIMIT_BYTES)


def _resident(shape):
    zeros = (0,) * len(shape)
    return pl.BlockSpec(shape, lambda *_: zeros, pipeline_mode=pl.Buffered(1))


def _rmsnorm(x, g):
    return x * lax.rsqrt(jnp.mean(x * x, axis=-1, keepdims=True) + NORM_EPS) * g


def _sigmoid(x):
    return 1.0 / (1.0 + jnp.exp2(x * (-math.log2(math.e))))


def _log_sigmoid(x):
    return jnp.minimum(x, 0.0) - jnp.log(1.0 + jnp.exp(-jnp.abs(x)))


def _split3(x):
    hi = x.astype(BF16)
    r = x - hi.astype(F32)
    mid = r.astype(BF16)
    lo = (r - mid.astype(F32)).astype(BF16)
    return hi, mid, lo


def _cumsum_rows(tril, x):
    return sum(jnp.dot(tril, part, preferred_element_type=F32) for part in _split3(x))


def _inproj_kernel(x_ref, g_ref, wm_ref, ws_ref, p_ref, s_ref, *, col_chunk):
    h = _rmsnorm(x_ref[...], g_ref[...]).astype(BF16)
    s_ref[...] = jnp.dot(h, ws_ref[...], preferred_element_type=F32)
    for c in range(wm_ref.shape[1] // col_chunk):
        sl = slice(c * col_chunk, (c + 1) * col_chunk)
        p_ref[:, sl] = jnp.dot(h, wm_ref[:, sl], preferred_element_type=F32).astype(BF16)


def _inproj(x2, g, wm, ws, *, tm=512, col_chunk=1024):
    n, d = x2.shape
    wcols = wm.shape[1]
    return pl.pallas_call(
        functools.partial(_inproj_kernel, col_chunk=col_chunk),
        grid=(n // tm,),
        in_specs=[pl.BlockSpec((tm, d), lambda i: (i, 0)),
                  _resident((1, d)), _resident((d, wcols)), _resident((d, LANES))],
        out_specs=[pl.BlockSpec((tm, wcols), lambda i: (i, 0)),
                   pl.BlockSpec((tm, LANES), lambda i: (i, 0))],
        out_shape=[jax.ShapeDtypeStruct((n, wcols), BF16), jax.ShapeDtypeStruct((n, LANES), F32)],
        compiler_params=_cparams("parallel"),
        name="inproj",
    )(x2, g, wm, ws)


AUG_TERMS = 3
AUG_STRIDE = 8


def _aug_placement(dh):
    pairs = FOX_HEADS * dh // LANES
    width = pairs * LANES
    pq = np.zeros((AUG_TERMS * LANES, width), np.float32)
    pk = np.zeros((AUG_TERMS * LANES, width), np.float32)
    cq = np.zeros((1, width), np.float32)
    ck = np.zeros((1, width), np.float32)
    for h in range(FOX_HEADS):
        base = (h // 2) * LANES + (h % 2) * AUG_STRIDE
        for t in range(AUG_TERMS):
            src = t * LANES + 2 * ML_HEADS + h
            pq[src, base + t] = 1.0
            pk[src, base + AUG_TERMS + t] = -1.0
            cq[0, base + AUG_TERMS + t] = 1.0
            ck[0, base + t] = 1.0
    return jnp.asarray(pq, BF16), jnp.asarray(pk, BF16), jnp.asarray(cq), jnp.asarray(ck)


def _gateprep_kernel(s_ref, b_ref, pq_ref, pk_ref, cq_ref, ck_ref, col_ref, row_ref, aq_ref, ak_ref, carry_ref):
    t = s_ref.shape[1]

    @pl.when(pl.program_id(1) == 0)
    def _():
        carry_ref[...] = jnp.zeros_like(carry_ref)

    pre = s_ref[0] + b_ref[...]
    lane = lax.broadcasted_iota(jnp.int32, pre.shape, 1)
    ls = _log_sigmoid(pre)
    forget = (lane >= ML_HEADS) & (lane < 2 * ML_HEADS + FOX_HEADS)
    half = t // 2
    tril = (lax.broadcasted_iota(jnp.int32, (half, half), 0)
            >= lax.broadcasted_iota(jnp.int32, (half, half), 1)).astype(BF16)
    gate_log = jnp.where(forget, ls, 0.0)
    top = _cumsum_rows(tril, gate_log[0:half]) + carry_ref[...]
    bottom = _cumsum_rows(tril, gate_log[half:t]) + top[half - 1:half, :]
    fcum = jnp.concatenate([top, bottom], axis=0)
    carry_ref[...] = fcum[t - 1:t, :]
    col = jnp.where(lane < ML_HEADS, pre, fcum)
    col_ref[0] = col
    row_ref[0] = col.T[0:row_ref.shape[1], :]
    parts = jnp.concatenate(_split3(fcum), axis=-1)
    aq_ref[0] = (jnp.dot(parts, pq_ref[...], preferred_element_type=F32) + cq_ref[...]).astype(BF16)
    ak_ref[0] = (jnp.dot(parts, pk_ref[...], preferred_element_type=F32) + ck_ref[...]).astype(BF16)


def _gateprep(s3, bias, dh, *, tg=512):
    b, s, _ = s3.shape
    n_rows = 2 * ML_HEADS + FOX_HEADS
    pq, pk, cq, ck = _aug_placement(dh)
    width = pq.shape[1]
    return pl.pallas_call(
        _gateprep_kernel,
        grid=(b, s // tg),
        in_specs=[pl.BlockSpec((1, tg, LANES), lambda i, j: (i, j, 0)), _resident((1, LANES)),
                  _resident(pq.shape), _resident(pk.shape), _resident(cq.shape), _resident(ck.shape)],
        out_specs=[pl.BlockSpec((1, tg, LANES), lambda i, j: (i, j, 0)),
                   pl.BlockSpec((1, n_rows, tg), lambda i, j: (i, 0, j)),
                   pl.BlockSpec((1, tg, width), lambda i, j: (i, j, 0)),
                   pl.BlockSpec((1, tg, width), lambda i, j: (i, j, 0))],
        out_shape=[jax.ShapeDtypeStruct((b, s, LANES), F32), jax.ShapeDtypeStruct((b, n_rows, s), F32),
                   jax.ShapeDtypeStruct((b, s, width), BF16), jax.ShapeDtypeStruct((b, s, width), BF16)],
        scratch_shapes=[pltpu.VMEM((1, LANES), F32)],
        compiler_params=_cparams("parallel", "arbitrary"),
        name="gateprep",
    )(s3, bias, pq, pk, cq, ck)


def _fox_kernel(q_ref, aq_ref, k_ref, ak_ref, v_ref, o_ref, m_ref, acc_ref, *, tq, tk, diag_strips, dh, scale):
    qi = pl.program_id(2)
    lane = lax.broadcasted_iota(jnp.int32, (tq, LANES), 1)
    q2 = q_ref[0] * scale
    aq = aq_ref[0]
    in_head = [(lane >= a * dh) & (lane < (a + 1) * dh) for a in range(2)]
    q_ops = []
    for a in range(2):
        in_aug = (lane >= a * AUG_STRIDE) & (lane < a * AUG_STRIDE + 2 * AUG_TERMS)
        q_ops.append(jnp.concatenate([jnp.where(in_head[a], q2, jnp.zeros_like(q2)),
                                      jnp.where(in_aug, aq, jnp.zeros_like(aq))], axis=-1))
    m_ref[...] = jnp.full_like(m_ref, NEG_BIG)
    acc_ref[...] = jnp.zeros_like(acc_ref)
    def step(start, width, row0=0, masked=False):
        rows = slice(row0, tq)
        kb = jnp.concatenate([k_ref[0, pl.ds(start, width), :], ak_ref[0, pl.ds(start, width), :]], axis=-1)
        vb = v_ref[0, pl.ds(start, width), :]
        key_lane = lax.broadcasted_iota(jnp.int32, (width, LANES), 1)
        key_head = [(key_lane >= a * dh) & (key_lane < (a + 1) * dh) for a in range(2)]
        if masked:
            causal = (lax.broadcasted_iota(jnp.int32, (tq - row0, width), 1)
                      <= lax.broadcasted_iota(jnp.int32, (tq - row0, width), 0))
        for a in range(2):
            s = lax.dot_general(q_ops[a][rows], kb, (((1,), (1,)), ((), ())), preferred_element_type=F32)
            if masked:
                s = jnp.where(causal, s, NEG_BIG)
            m_prev = m_ref[a, rows]
            m_new = jnp.maximum(m_prev, jnp.max(s, axis=-1, keepdims=True))
            alpha = jnp.exp(m_prev - m_new)
            pexp = jnp.exp(s - jnp.concatenate([m_new] * (width // LANES), axis=-1))
            v_op = jnp.where(key_head[a], vb, jnp.ones_like(vb))
            acc_ref[a, rows] = alpha * acc_ref[a, rows] + jnp.dot(pexp.astype(BF16), v_op,
                                                                  preferred_element_type=F32)
            m_ref[a, rows] = m_new

    n_wide = (qi * tq) // tk
    n_narrow = qi - n_wide * (tk // tq)

    def wide_body(j, carry):
        step(pl.multiple_of(j * tk, tk), tk)
        return carry

    def narrow_body(j, carry):
        step(pl.multiple_of((n_wide * (tk // tq) + j) * tq, tq), tq)
        return carry

    lax.fori_loop(0, n_wide, wide_body, 0)
    if tk != tq:
        lax.fori_loop(0, n_narrow, narrow_body, 0)
    strip = tq // diag_strips
    for c in range(diag_strips):
        step(pl.multiple_of(qi * tq + c * strip, strip), strip, row0=c * strip, masked=True)
    outs = [acc_ref[a] / pltpu.roll(acc_ref[a], LANES // 2, 1) for a in range(2)]
    o_ref[0] = jnp.where(in_head[0], outs[0], outs[1]).astype(BF16)


def _fox(p3, aq, ak, *, q_col, k_col, v_col, dh, tq=1024, tk=2048, diag_strips=2):
    b, s, _ = p3.shape
    assert 2 * dh == LANES, "two heads share one 128-lane block"
    assert tk % tq == 0 and s % tq == 0
    pairs = FOX_HEADS * dh // LANES
    scale = dh ** -0.5
    assert math.frexp(scale)[0] == 0.5, "score scale is folded into bf16 q; exact only for a power of two"
    qb, kb, vb = q_col // LANES, k_col // LANES, v_col // LANES
    return pl.pallas_call(
        functools.partial(_fox_kernel, tq=tq, tk=tk, diag_strips=diag_strips, dh=dh, scale=scale),
        grid=(b, pairs, s // tq),
        in_specs=[pl.BlockSpec((1, tq, LANES), lambda i, p, q: (i, q, qb + p)),
                  pl.BlockSpec((1, tq, LANES), lambda i, p, q: (i, q, p)),
                  pl.BlockSpec((1, s, LANES), lambda i, p, q: (i, 0, kb + p)),
                  pl.BlockSpec((1, s, LANES), lambda i, p, q: (i, 0, p)),
                  pl.BlockSpec((1, s, LANES), lambda i, p, q: (i, 0, vb + p))],
        out_specs=pl.BlockSpec((1, tq, LANES), lambda i, p, q: (i, q, p)),
        out_shape=jax.ShapeDtypeStruct((b, s, pairs * LANES), BF16),
        scratch_shapes=[pltpu.VMEM((2, tq, LANES), F32), pltpu.VMEM((2, tq, LANES), F32)],
        compiler_params=_cparams("parallel", "parallel", "arbitrary"),
        name="fox_attention",
    )(p3, aq, p3, ak, p3)


def _mlstm_kernel(qk_ref, halo_ref, v_ref, og_ref, gc_ref, gr_ref, cw_ref, ng_ref, out_ref,
                  ubuf, ct_ref, n_ref, m_ref, g0_ref, *, chunk, dh):
    c = pl.program_id(1)
    width = ML_HEADS * dh
    taps = cw_ref.shape[0]
    pad = halo_ref.shape[1]
    assert 3 * ML_HEADS <= BF16_SUBLANES

    @pl.when(c == 0)
    def _():
        ct_ref[...] = jnp.zeros_like(ct_ref)
        n_ref[...] = jnp.zeros_like(n_ref)
        m_ref[...] = jnp.zeros_like(m_ref)
        g0_ref[...] = jnp.zeros_like(g0_ref)

    ubuf[0:pad, :] = jnp.where(c == 0, 0.0, halo_ref[0].astype(F32))
    ubuf[pad:pad + chunk, :] = qk_ref[0].astype(F32)

    def conv_silu(col0):
        y = sum(ubuf[pad - taps + 1 + j:pad - taps + 1 + j + chunk, col0:col0 + dh] * cw_ref[j:j + 1, col0:col0 + dh]
                for j in range(taps))
        return y * _sigmoid(y)

    ri = lax.broadcasted_iota(jnp.int32, (chunk, chunk), 0)
    ci = lax.broadcasted_iota(jnp.int32, (chunk, chunk), 1)
    visible = ri <= ci
    gc = gc_ref[0]
    gr = gr_ref[0]
    g0 = g0_ref[...]
    g0_ref[...] = gc[chunk - 1:chunk, :]
    contract_last = (((1,), (1,)), ((), ()))
    contract_first = (((0,), (0,)), ((), ()))
    part_row = lax.broadcasted_iota(jnp.int32, (BF16_SUBLANES, dh), 0)
    slab_row = lax.broadcasted_iota(jnp.int32, (BF16_SUBLANES, chunk), 0)
    ones_dh = jnp.ones((dh, dh), BF16)

    slab = jnp.zeros((BF16_SUBLANES, chunk), F32)
    heads = []
    for h in range(ML_HEADS):
        hs = slice(h * dh, (h + 1) * dh)
        q = conv_silu(h * dh)
        k = conv_silu(width + h * dh) * dh ** -0.5
        qb, kb = q.astype(BF16), k.astype(BF16)
        vb = v_ref[0, :, hs]
        g0_h = g0[:, ML_HEADS + h:ML_HEADS + h + 1]
        key_col = gc[:, h:h + 1] - (gc[:, ML_HEADS + h:ML_HEADS + h + 1] - g0_h)
        li_r = gr[h:h + 1, :]
        b_r = gr[ML_HEADS + h:ML_HEADS + h + 1, :] - g0_h
        b_last = b_r[:, chunk - 1:chunk]
        m_prev = m_ref[h:h + 1, 0:1]

        d_t = jnp.where(visible, b_r + key_col, NEG_BIG)
        m_inter = b_r + m_prev
        m_t = jnp.maximum(m_inter, jnp.max(d_t, axis=0, keepdims=True))
        w_inter = jnp.exp(m_inter - m_t)
        p_t = jnp.exp(d_t - m_t) * lax.dot_general(kb, qb, contract_last, preferred_element_type=F32)
        n_hi, n_mid, n_lo = (part.astype(F32) for part in _split3(n_ref[h]))
        n_parts = jnp.where(part_row == 0, n_hi, jnp.where(part_row == 1, n_mid,
                                                           jnp.where(part_row == 2, n_lo, 0.0))).astype(BF16)
        nq = jnp.sum(lax.dot_general(n_parts, qb, contract_last, preferred_element_type=F32), axis=0, keepdims=True)
        den = w_inter * nq + jnp.sum(p_t, axis=0, keepdims=True)
        inv = 1.0 / jnp.maximum(jnp.abs(den), jnp.exp(-m_t))

        g_r = b_last - b_r + li_r
        m_new = jnp.maximum(b_last + m_prev, jnp.max(g_r, axis=-1, keepdims=True))
        decay = jnp.exp(b_last + m_prev - m_new)
        wk_r = jnp.exp(g_r - m_new)
        for slot, vec in enumerate((w_inter, inv, wk_r)):
            slab = jnp.where(slab_row == 3 * h + slot, vec, slab)
        heads.append((hs, k, qb, vb, p_t.astype(BF16), decay, m_new))

    cols = jnp.concatenate([slab, jnp.zeros((LANES - BF16_SUBLANES, chunk), F32)], axis=0).T

    for h, (hs, k, qb, vb, p_tb, decay, m_new) in enumerate(heads):
        w_col, inv_col, wk_col = (cols[:, 3 * h + slot:3 * h + slot + 1] for slot in range(3))
        num = (w_col * jnp.dot(qb, ct_ref[h].astype(BF16), preferred_element_type=F32)
               + lax.dot_general(p_tb, vb, contract_first, preferred_element_type=F32))
        hh = num * inv_col
        kw = k * wk_col
        ct_ref[h] = decay * ct_ref[h] + lax.dot_general(kw.astype(BF16), vb, contract_first,
                                                       preferred_element_type=F32)
        n_ref[h] = decay * n_ref[h] + jnp.sum(kw, axis=0, keepdims=True)
        m_ref[h:h + 1, :] = jnp.broadcast_to(m_new, (1, LANES))

        sq = hh * hh
        sq_hi = sq.astype(BF16)
        sq_lo = (sq - sq_hi.astype(F32)).astype(BF16)
        mean_sq = (jnp.dot(sq_hi, ones_dh, preferred_element_type=F32)
                   + jnp.dot(sq_lo, ones_dh, preferred_element_type=F32)) * (1.0 / dh)
        hn = hh * lax.rsqrt(mean_sq + NORM_EPS) * ng_ref[:, hs]
        out_ref[0, :, hs] = (hn * _sigmoid(og_ref[0, :, hs].astype(F32))).astype(BF16)


def _mlstm(p3, gcol, grow, conv_w, norm_g, *, qk_col, v_col, o_col, dh, chunk=256):
    b, s, _ = p3.shape
    width = ML_HEADS * dh
    pad = BF16_SUBLANES
    assert conv_w.shape[0] - 1 <= pad
    qkb, vb, ob = qk_col // (2 * width), v_col // width, o_col // width
    halo_per_chunk = chunk // pad
    return pl.pallas_call(
        functools.partial(_mlstm_kernel, chunk=chunk, dh=dh),
        grid=(b, s // chunk),
        in_specs=[pl.BlockSpec((1, chunk, 2 * width), lambda i, c: (i, c, qkb)),
                  pl.BlockSpec((1, pad, 2 * width),
                               lambda i, c: (i, jnp.maximum(c * halo_per_chunk - 1, 0), qkb)),
                  pl.BlockSpec((1, chunk, width), lambda i, c: (i, c, vb)),
                  pl.BlockSpec((1, chunk, width), lambda i, c: (i, c, ob)),
                  pl.BlockSpec((1, chunk, LANES), lambda i, c: (i, c, 0)),
                  pl.BlockSpec((1, grow.shape[1], chunk), lambda i, c: (i, 0, c)),
                  _resident(conv_w.shape), _resident((1, width))],
        out_specs=pl.BlockSpec((1, chunk, width), lambda i, c: (i, c, 0)),
        out_shape=jax.ShapeDtypeStruct((b, s, width), BF16),
        scratch_shapes=[pltpu.VMEM((pad + chunk, 2 * width), F32),
                        pltpu.VMEM((ML_HEADS, dh, dh), F32),
                        pltpu.VMEM((ML_HEADS, 1, dh), F32),
                        pltpu.VMEM((ML_HEADS, LANES), F32),
                        pltpu.VMEM((1, LANES), F32)],
        compiler_params=_cparams("parallel", "arbitrary"),
        name="mlstm",
    )(p3, p3, p3, p3, gcol, grow, conv_w, norm_g)


def _merge_kernel(u_ref, halo_ref, hm_ref, fo_ref, gp_ref, x_ref, wg_ref, ps_ref, wbp_ref, wbm_ref, wbf_ref,
                  wo_ref, o_ref, ubuf, sbuf, *, tm):
    j = pl.program_id(1)
    pad = halo_ref.shape[1]
    gw = wg_ref.shape[1]
    pw = ubuf.shape[1]
    d = x_ref.shape[2]
    margin = ubuf.shape[0] - pad - tm
    base = margin + pad
    ext = pad + tm
    levels = len(POOL_WINDOWS)
    assert all(w == 2 ** (g + 1) for g, w in enumerate(POOL_WINDOWS)) and 2 ** (levels - 1) <= margin

    ubuf[0:margin, :] = jnp.zeros((margin, pw), F32)
    ubuf[margin:base, :] = jnp.where(j == 0, 0.0, halo_ref[0].astype(F32))
    ubuf[base:base + tm, :] = u_ref[0].astype(F32)
    wsums = []
    prev = ubuf
    for lvl in range(1, levels + 1):
        shift = 2 ** (lvl - 1)
        lanes = slice((lvl - 1) * gw, pw)
        if lvl < levels:
            cur = sbuf.at[lvl - 1]
            cur[0:margin, lanes] = jnp.zeros((margin, pw - (lvl - 1) * gw), F32)
            cur[margin:margin + ext, lanes] = (prev[margin:margin + ext, lanes]
                                               + prev[margin - shift:margin - shift + ext, lanes])
            wsums.append(cur[base:base + tm, (lvl - 1) * gw:lvl * gw])
            prev = cur
        else:
            wsums.append(prev[base:base + tm, lanes] + prev[base - shift:base - shift + tm, lanes])
    pos = j * tm + lax.broadcasted_iota(jnp.int32, (tm, 1), 0)
    ys = []
    for g, w in enumerate(POOL_WINDOWS):
        gs = slice(g * gw, (g + 1) * gw)
        cnt = jnp.minimum(pos + 1, w).astype(F32)
        dlt = wsums[g] / cnt - ubuf[base:base + tm, gs]
        ys.append(jnp.dot(dlt.astype(BF16), wg_ref[g], preferred_element_type=F32))
    y_pool = (jnp.concatenate(ys, axis=-1) * ps_ref[...]).astype(BF16)
    merged = _sigmoid(gp_ref[0, :, 0:d].astype(F32)) * jnp.dot(y_pool, wbp_ref[...], preferred_element_type=F32)
    merged += _sigmoid(gp_ref[0, :, d:2 * d].astype(F32)) * jnp.dot(hm_ref[0], wbm_ref[...],
                                                                   preferred_element_type=F32)
    merged += _sigmoid(gp_ref[0, :, 2 * d:3 * d].astype(F32)) * jnp.dot(fo_ref[0], wbf_ref[...],
                                                                       preferred_element_type=F32)
    o_ref[0] = x_ref[0] + jnp.dot(merged.astype(BF16), wo_ref[...], preferred_element_type=F32)


def _merge(p3, hm, fo, x3, wgrp, pscale, wbp, wbm, wbf, wo, *, pool_col, gate_col, tm=512):
    b, s, d = x3.shape
    pw = wbp.shape[0]
    pad = BF16_SUBLANES
    margin = 8
    assert max(POOL_WINDOWS) - 1 <= pad and gate_col == 0
    pcb = pool_col // pw
    halo_per_blk = tm // pad
    return pl.pallas_call(
        functools.partial(_merge_kernel, tm=tm),
        grid=(b, s // tm),
        in_specs=[pl.BlockSpec((1, tm, pw), lambda i, j: (i, j, pcb)),
                  pl.BlockSpec((1, pad, pw), lambda i, j: (i, jnp.maximum(j * halo_per_blk - 1, 0), pcb)),
                  pl.BlockSpec((1, tm, hm.shape[2]), lambda i, j: (i, j, 0)),
                  pl.BlockSpec((1, tm, fo.shape[2]), lambda i, j: (i, j, 0)),
                  pl.BlockSpec((1, tm, N_BRANCH * d), lambda i, j: (i, j, 0)),
                  pl.BlockSpec((1, tm, d), lambda i, j: (i, j, 0)),
                  _resident(wgrp.shape), _resident(pscale.shape), _resident(wbp.shape),
                  _resident(wbm.shape), _resident(wbf.shape), _resident(wo.shape)],
        out_specs=pl.BlockSpec((1, tm, d), lambda i, j: (i, j, 0)),
        out_shape=jax.ShapeDtypeStruct((b, s, d), F32),
        scratch_shapes=[pltpu.VMEM((margin + pad + tm, pw), F32),
                        pltpu.VMEM((len(POOL_WINDOWS) - 1, margin + pad + tm, pw), F32)],
        compiler_params=_cparams("parallel", "parallel"),
        name="merge",
    )(p3, p3, hm, fo, p3, x3, wgrp, pscale, wbp, wbm, wbf, wo)


def _swiglu_kernel(x_ref, g_ref, wg_ref, wu_ref, wd_ref, o_ref, a_ref, *, ff_chunk):
    x = x_ref[...]
    h = _rmsnorm(x, g_ref[...]).astype(BF16)
    for c in range(wg_ref.shape[1] // ff_chunk):
        sl = slice(c * ff_chunk, (c + 1) * ff_chunk)
        gate = jnp.dot(h, wg_ref[:, sl], preferred_element_type=F32)
        up = jnp.dot(h, wu_ref[:, sl], preferred_element_type=F32)
        a_ref[:, sl] = (gate * _sigmoid(gate) * up).astype(BF16)
    o_ref[...] = x + jnp.dot(a_ref[...], wd_ref[...], preferred_element_type=F32)


def _swiglu(x2, g, wg, wu, wd, *, tm=512, ff_chunk=256):
    n, d = x2.shape
    ff = wg.shape[1]
    assert ff % ff_chunk == 0
    return pl.pallas_call(
        functools.partial(_swiglu_kernel, ff_chunk=ff_chunk),
        grid=(n // tm,),
        in_specs=[pl.BlockSpec((tm, d), lambda i: (i, 0)), _resident((1, d)),
                  _resident(wg.shape), _resident(wu.shape), _resident(wd.shape)],
        out_specs=pl.BlockSpec((tm, d), lambda i: (i, 0)),
        out_shape=jax.ShapeDtypeStruct((n, d), F32),
        scratch_shapes=[pltpu.VMEM((tm, ff), BF16)],
        compiler_params=_cparams("parallel"),
        name="dense_swiglu",
    )(x2, g, wg, wu, wd)


def _router_kernel(x_ref, g_ref, wr_ref, br_ref, e_ref, gt_ref, cnt_ref, carry_ref):
    h = _rmsnorm(x_ref[...], g_ref[...])
    h_hi = h.astype(BF16)
    h_lo = (h - h_hi.astype(F32)).astype(BF16)
    logits = jnp.dot(jnp.concatenate([h_hi, h_hi, h_lo], axis=-1), wr_ref[...],
                     preferred_element_type=F32) + br_ref[...]
    lane = lax.broadcasted_iota(jnp.int32, logits.shape, 1)
    m1 = jnp.max(logits, axis=-1, keepdims=True)
    i1 = jnp.min(jnp.where(logits == m1, lane, LANES), axis=-1, keepdims=True)
    rest = jnp.where(lane == i1, NEG_BIG, logits)
    m2 = jnp.max(rest, axis=-1, keepdims=True)
    i2 = jnp.min(jnp.where(rest == m2, lane, LANES), axis=-1, keepdims=True)
    e2 = jnp.exp(m2 - m1)
    g1 = 1.0 / (1.0 + e2)
    gt_ref[...] = jnp.where(lane == 0, g1, jnp.where(lane == 1, e2 * g1, 0.0))

    @pl.when(pl.program_id(0) == 0)
    def _():
        carry_ref[...] = jnp.zeros_like(carry_ref)

    tm = logits.shape[0]
    pick0, pick1 = lane == i1, lane == i2
    picks = jnp.where(pick0 | pick1, 1.0, 0.0)
    earlier = (lax.broadcasted_iota(jnp.int32, (tm, tm), 0) > lax.broadcasted_iota(jnp.int32, (tm, tm), 1))
    prior = jnp.dot(earlier.astype(BF16), picks.astype(BF16), preferred_element_type=F32) + carry_ref[...]
    r0 = jnp.sum(jnp.where(pick0, prior, 0.0), axis=-1, keepdims=True).astype(jnp.int32)
    r1 = jnp.sum(jnp.where(pick1, prior, 0.0), axis=-1, keepdims=True).astype(jnp.int32)
    carry_ref[...] = prior[tm - 1:tm, :] + picks[tm - 1:tm, :]
    cnt_ref[...] = jnp.broadcast_to(carry_ref[...], cnt_ref.shape).astype(jnp.int32)
    routed = jnp.where(lane == 0, i1, jnp.where(lane == 1, i2, jnp.where(lane == 2, r0, jnp.where(lane == 3, r1, 0))))
    e_ref[...] = routed[:, 0:e_ref.shape[1]]


def _router(x2, g, wr, br, *, tm=512):
    n, d = x2.shape
    return pl.pallas_call(
        _router_kernel,
        grid=(n // tm,),
        in_specs=[pl.BlockSpec((tm, d), lambda i: (i, 0)), _resident((1, d)),
                  _resident(wr.shape), _resident((1, LANES))],
        out_specs=[pl.BlockSpec((tm, 2 * TOP_K), lambda i: (i, 0)), pl.BlockSpec((tm, LANES), lambda i: (i, 0)),
                   pl.BlockSpec((8, LANES), lambda i: (0, 0))],
        out_shape=[jax.ShapeDtypeStruct((n, 2 * TOP_K), jnp.int32), jax.ShapeDtypeStruct((n, LANES), F32),
                   jax.ShapeDtypeStruct((8, LANES), jnp.int32)],
        scratch_shapes=[pltpu.VMEM((1, LANES), F32)],
        compiler_params=_cparams("arbitrary"),
        name="router",
    )(x2, g, wr, br)


def _dispatch_kernel(pend_ref, dest_ref, x_ref, xs_ref, zero_ref, xbuf, sems, zsem):
    tm = dest_ref.shape[2] // TOP_K
    rows = zero_ref.shape[0]
    step = pl.program_id(0)

    @pl.when(step == 0)
    def _():
        zero_ref[...] = jnp.zeros_like(zero_ref)

        def zero_copy(e):
            start = pl.multiple_of(pend_ref[e] - rows, rows)
            return pltpu.make_async_copy(zero_ref, xs_ref.at[pl.ds(start, rows)], zsem)

        def has_rows(e):
            return pend_ref[e] > (pend_ref[e - 1] if e else 0)

        n_exp = pend_ref.shape[0]
        n_rows = xs_ref.shape[0]
        tail_starts = [n_rows - (j + 1) * rows for j in range(min(n_exp, n_rows // rows))]

        def tail_copy(start):
            return pltpu.make_async_copy(zero_ref, xs_ref.at[pl.ds(start, rows)], zsem)

        for e in range(n_exp):
            @pl.when(has_rows(e))
            def _():
                zero_copy(e).start()
        for start in tail_starts:
            @pl.when(start >= pend_ref[n_exp - 1])
            def _():
                tail_copy(start).start()
        for e in range(n_exp):
            @pl.when(has_rows(e))
            def _():
                zero_copy(e).wait()
        for start in tail_starts:
            @pl.when(start >= pend_ref[n_exp - 1])
            def _():
                tail_copy(start).wait()

    def issue(s):
        xbuf[s] = x_ref[...]

        def start(t, carry):
            for k in range(TOP_K):
                pltpu.make_async_copy(xbuf.at[s, pl.ds(t, 1)],
                                      xs_ref.at[pl.ds(dest_ref[0, 0, TOP_K * t + k], 1)], sems.at[s]).start()
            return carry

        lax.fori_loop(0, tm, start, 0, unroll=True)

    def drain(s):
        def wait(t, carry):
            pltpu.make_async_copy(xbuf.at[s, pl.ds(0, 1)], xs_ref.at[pl.ds(0, 1)], sems.at[s]).wait()
            return carry

        lax.fori_loop(0, TOP_K * tm, wait, 0, unroll=8)

    last = pl.num_programs(0) - 1
    for s in range(2):
        @pl.when(step % 2 == s)
        def _():
            issue(s)

            @pl.when(step > 0)
            def _():
                drain(1 - s)

            @pl.when(step == last)
            def _():
                drain(s)


def _dispatch(pend, dest2, x2, *, n_rows, rows, tm=256):
    n, d = x2.shape
    grid_spec = pltpu.PrefetchScalarGridSpec(
        num_scalar_prefetch=1,
        grid=(n // tm,),
        in_specs=[pl.BlockSpec((1, 1, TOP_K * tm), lambda i, pe: (i, 0, 0), memory_space=pltpu.SMEM),
                  pl.BlockSpec((tm, d), lambda i, pe: (i, 0))],
        out_specs=pl.BlockSpec(memory_space=pl.ANY),
        scratch_shapes=[pltpu.VMEM((rows, d), F32), pltpu.VMEM((2, tm, d), F32),
                        pltpu.SemaphoreType.DMA((2,)), pltpu.SemaphoreType.DMA(())],
    )
    return pl.pallas_call(
        _dispatch_kernel,
        grid_spec=grid_spec,
        out_shape=jax.ShapeDtypeStruct((n_rows, d), F32),
        compiler_params=_cparams("arbitrary"),
        name="moe_dispatch",
    )(pend, dest2, x2)


def _experts_kernel(blk_e_ref, nact_ref, xs_ref, g_ref, wg_ref, wu_ref, wd_ref, ys_ref, a_ref, *, ff_chunk):
    del blk_e_ref
    i = pl.program_id(0)

    @pl.when(i < nact_ref[0])
    def _():
        h = _rmsnorm(xs_ref[...], g_ref[...]).astype(BF16)
        for c in range(wg_ref.shape[2] // ff_chunk):
            sl = slice(c * ff_chunk, (c + 1) * ff_chunk)
            gate = jnp.dot(h, wg_ref[0, :, sl], preferred_element_type=F32)
            up = jnp.dot(h, wu_ref[0, :, sl], preferred_element_type=F32)
            a_ref[:, sl] = (gate * _sigmoid(gate) * up).astype(BF16)
        ys_ref[...] = jnp.dot(a_ref[...], wd_ref[0], preferred_element_type=F32)

    @pl.when(i >= nact_ref[0])
    def _():
        ys_ref[...] = jnp.zeros_like(ys_ref)


def _experts(blk_e, nact, xs, g, wg, wu, wd, *, rows, ff_chunk=256):
    n_rows, d = xs.shape
    ff = wg.shape[2]
    assert ff % ff_chunk == 0 and n_rows % rows == 0

    def blk(i, nact):
        return jnp.minimum(i, nact[0] - 1)

    def expert_weights(shape, buffers):
        return pl.BlockSpec(shape, lambda i, be, na: (be[blk(i, na)], 0, 0), pipeline_mode=pl.Buffered(buffers))

    grid_spec = pltpu.PrefetchScalarGridSpec(
        num_scalar_prefetch=2,
        grid=(n_rows // rows,),
        in_specs=[pl.BlockSpec((rows, d), lambda i, be, na: (blk(i, na), 0)),
                  pl.BlockSpec((1, d), lambda i, be, na: (0, 0)),
                  expert_weights((1, d, ff), 2), expert_weights((1, d, ff), 2), expert_weights((1, ff, d), 1)],
        out_specs=pl.BlockSpec((rows, d), lambda i, be, na: (i, 0)),
        scratch_shapes=[pltpu.VMEM((rows, ff), BF16)],
    )
    return pl.pallas_call(
        functools.partial(_experts_kernel, ff_chunk=ff_chunk),
        grid_spec=grid_spec,
        out_shape=jax.ShapeDtypeStruct((n_rows, d), F32),
        compiler_params=_cparams("arbitrary"),
        name="moe_experts",
    )(blk_e, nact, xs, g, wg, wu, wd)


def _combine_kernel(dest_ref, next_dest_ref, x_ref, gt_ref, fg_ref, ys_ref, o_ref, buf, sems, *, final_norm):
    tm = x_ref.shape[0]
    step = pl.program_id(0)
    slot = step % 2

    def row_copy(dref, s, t, k):
        return pltpu.make_async_copy(ys_ref.at[pl.ds(dref[0, 0, TOP_K * t + k], 1)], buf.at[s, k, pl.ds(t, 1)],
                                     sems.at[s])

    def gather(dref, s):
        def start(t, carry):
            for k in range(TOP_K):
                row_copy(dref, s, t, k).start()
            return carry
        lax.fori_loop(0, tm, start, 0, unroll=True)

    @pl.when(step == 0)
    def _():
        gather(dest_ref, 0)

    @pl.when(step + 1 < pl.num_programs(0))
    def _():
        gather(next_dest_ref, 1 - slot)

    def wait(t, carry):
        for k in range(TOP_K):
            row_copy(dest_ref, slot, t, k).wait()
        return carry

    lax.fori_loop(0, tm, wait, 0, unroll=8)
    gt = gt_ref[...]
    y = x_ref[...]
    for k in range(TOP_K):
        y = y + buf[slot, k] * gt[:, k:k + 1]
    o_ref[...] = _rmsnorm(y, fg_ref[...]) if final_norm else y


def _combine(dest2, x2, gates, fg, ys, *, final_norm, tm=256):
    n, d = x2.shape
    steps = n // tm
    return pl.pallas_call(
        functools.partial(_combine_kernel, final_norm=final_norm),
        grid=(steps,),
        in_specs=[pl.BlockSpec((1, 1, TOP_K * tm), lambda i: (i, 0, 0), memory_space=pltpu.SMEM),
                  pl.BlockSpec((1, 1, TOP_K * tm), lambda i: (jnp.minimum(i + 1, steps - 1), 0, 0),
                               memory_space=pltpu.SMEM),
                  pl.BlockSpec((tm, d), lambda i: (i, 0)),
                  pl.BlockSpec((tm, LANES), lambda i: (i, 0)),
                  _resident((1, d)),
                  pl.BlockSpec(memory_space=pl.ANY)],
        out_specs=pl.BlockSpec((tm, d), lambda i: (i, 0)),
        out_shape=jax.ShapeDtypeStruct((n, d), F32),
        scratch_shapes=[pltpu.VMEM((2, TOP_K, tm, d), F32), pltpu.SemaphoreType.DMA((2,))],
        compiler_params=_cparams("arbitrary"),
        name="moe_combine",
    )(dest2, dest2, x2, gates, fg, ys)


def _final_norm_kernel(x_ref, g_ref, o_ref):
    o_ref[...] = _rmsnorm(x_ref[...], g_ref[...])


def _final_norm(x2, g, *, tm=512):
    n, d = x2.shape
    return pl.pallas_call(
        _final_norm_kernel,
        grid=(n // tm,),
        in_specs=[pl.BlockSpec((tm, d), lambda i: (i, 0)), _resident((1, d))],
        out_specs=pl.BlockSpec((tm, d), lambda i: (i, 0)),
        out_shape=jax.ShapeDtypeStruct((n, d), F32),
        compiler_params=_cparams("parallel"),
        name="final_norm",
    )(x2, g)


def _token_mixing(x3, norm_g, w_in, pool_w_grp, pool_scale, conv_w, b_i, b_f, ml_norm_g, fox_b_f,
                  w_br_pool, w_br_ml, w_br_fox, w_out):
    b, s, d = x3.shape
    pool_w = w_br_pool.shape[0]
    ml_w = w_br_ml.shape[0]
    fox_w = w_br_fox.shape[0]
    ml_dh = ml_w // ML_HEADS
    fox_dh = fox_w // FOX_HEADS
    n_small = 2 * ML_HEADS + FOX_HEADS

    sizes = (pool_w, ml_w, ml_w, ml_w, ml_w, ML_HEADS, ML_HEADS, fox_w, fox_w, fox_w, FOX_HEADS, N_BRANCH * d)
    offs = [0]
    for sz in sizes:
        offs.append(offs[-1] + sz)
    assert offs[-1] == w_in.shape[1]
    (o_pool, o_q, o_k, o_v, o_o, o_i, o_f, o_fq, o_fk, o_fv, o_ff, o_g) = offs[:-1]

    def cols(o, sz):
        return w_in[:, o:o + sz]

    wm = jnp.concatenate([cols(o_g, N_BRANCH * d), cols(o_q, ml_w), cols(o_k, ml_w), cols(o_v, ml_w),
                          cols(o_o, ml_w), cols(o_pool, pool_w), cols(o_fq, fox_w), cols(o_fk, fox_w),
                          cols(o_fv, fox_w)], axis=1).astype(BF16)
    c_gate = 0
    c_qk = c_gate + N_BRANCH * d
    c_v = c_qk + 2 * ml_w
    c_o = c_v + ml_w
    c_pool = c_o + ml_w
    c_fq = c_pool + pool_w
    c_fk = c_fq + fox_w
    c_fv = c_fk + fox_w
    ws = jnp.concatenate([cols(o_i, ML_HEADS), cols(o_f, ML_HEADS), cols(o_ff, FOX_HEADS),
                          jnp.zeros((d, LANES - n_small), w_in.dtype)], axis=1).astype(BF16)
    bias = jnp.concatenate([b_i, b_f, fox_b_f, jnp.zeros((LANES - n_small,), F32)]).reshape(1, LANES)

    proj, small = _inproj(x3.reshape(b * s, d), norm_g.reshape(1, d), wm, ws)
    p3 = proj.reshape(b, s, -1)
    gcol, grow, aq, ak = _gateprep(small.reshape(b, s, LANES), bias, fox_dh)
    fo = _fox(p3, aq, ak, q_col=c_fq, k_col=c_fk, v_col=c_fv, dh=fox_dh)
    hm = _mlstm(p3, gcol, grow, conv_w, ml_norm_g.reshape(1, ml_w), qk_col=c_qk, v_col=c_v, o_col=c_o, dh=ml_dh)
    return _merge(p3, hm, fo, x3, pool_w_grp.astype(BF16), pool_scale.reshape(1, pool_w),
                  w_br_pool.astype(BF16), w_br_ml.astype(BF16), w_br_fox.astype(BF16), w_out.astype(BF16),
                  pool_col=c_pool, gate_col=c_gate)


def _moe(x2, norm_g, w_router, b_router, w_gate, w_up, w_down, final_g, *, rows=512):
    n, d = x2.shape
    n_exp = w_router.shape[1]
    wr = jnp.concatenate([w_router.astype(F32), jnp.zeros((d, LANES - n_exp), F32)], axis=1)
    wr_hi = wr.astype(BF16)
    wr_lo = (wr - wr_hi.astype(F32)).astype(BF16)
    br = jnp.concatenate([b_router.astype(F32), jnp.full((LANES - n_exp,), NEG_BIG, F32)]).reshape(1, LANES)
    routed, gates, cnt = _router(x2, norm_g.reshape(1, d), jnp.concatenate([wr_hi, wr_lo, wr_hi], axis=0), br)

    e_flat = routed[:, :TOP_K].reshape(n * TOP_K)
    rank = routed[:, TOP_K:2 * TOP_K].reshape(n * TOP_K)
    counts = cnt[0, :n_exp]
    padded = ((counts + rows - 1) // rows) * rows
    pend = jnp.cumsum(padded)
    onehot = e_flat[:, None] == jnp.arange(n_exp, dtype=jnp.int32)[None, :]
    dest = (jnp.sum(jnp.where(onehot, (pend - padded)[None, :], 0), axis=1) + rank).astype(jnp.int32)
    n_blk = (n * TOP_K + n_exp * (rows - 1) + rows - 1) // rows
    blk_start = jnp.arange(n_blk, dtype=jnp.int32) * rows
    blk_e = jnp.minimum(jnp.sum(blk_start[:, None] >= pend[None, :], axis=1), n_exp - 1).astype(jnp.int32)
    nact = (pend[-1:] // rows).astype(jnp.int32)

    tm = 512
    dest2 = dest.reshape(n // tm, 1, TOP_K * tm)
    xs = _dispatch(pend.astype(jnp.int32), dest2, x2, n_rows=n_blk * rows, rows=rows, tm=tm)
    ys = _experts(blk_e, nact, xs, norm_g.reshape(1, d), w_gate.astype(BF16), w_up.astype(BF16),
                  w_down.astype(BF16), rows=rows)
    fg = jnp.ones((1, d), F32) if final_g is None else final_g.reshape(1, d)
    return _combine(dest2, x2, gates, fg, ys, final_norm=final_g is not None, tm=tm)


def kernel(x, mix_norm_g, w_in, pool_w_grp, pool_scale, ml_conv_w, ml_b_i, ml_b_f, ml_norm_g, fox_b_f,
           w_br_pool, w_br_ml, w_br_fox, w_out, ffn_norm_g, ff_w_gate, ff_w_up, ff_w_down,
           moe_w_router, moe_b_router, moe_w_gate, moe_w_up, moe_w_down, final_norm_g):
    b, s, d = x.shape
    depth = mix_norm_g.shape[0]
    fused_final = False
    for l in range(depth):
        x = _token_mixing(x, mix_norm_g[l], w_in[l], pool_w_grp[l], pool_scale[l], ml_conv_w[l], ml_b_i[l],
                          ml_b_f[l], ml_norm_g[l], fox_b_f[l], w_br_pool[l], w_br_ml[l], w_br_fox[l], w_out[l])
        x2 = x.reshape(b * s, d)
        if l % 2 == 0:
            x2 = _swiglu(x2, ffn_norm_g[l].reshape(1, d), ff_w_gate[l // 2].astype(BF16),
                         ff_w_up[l // 2].astype(BF16), ff_w_down[l // 2].astype(BF16))
        else:
            fused_final = l == depth - 1
            x2 = _moe(x2, ffn_norm_g[l], moe_w_router[l // 2], moe_b_router[l // 2], moe_w_gate[l // 2],
                      moe_w_up[l // 2], moe_w_down[l // 2], final_norm_g if fused_final else None)
        x = x2.reshape(b, s, d)
    if not fused_final:
        x = _final_norm(x.reshape(b * s, d), final_norm_g.reshape(1, d)).reshape(b, s, d)
    return x
```

```python
import functools
import math

import jax
import jax.numpy as jnp
import numpy as np
from jax import lax
from jax.experimental import pallas as pl
from jax.experimental.pallas import tpu as pltpu

F32 = jnp.float32
BF16 = jnp.bfloat16

NORM_EPS = 1e-6
POOL_WINDOWS = (2, 4, 8, 16)
ML_HEADS = 4
FOX_HEADS = 8
TOP_K = 2
N_BRANCH = 3

LANES = 128
BF16_SUBLANES = 16
VMEM_LIMIT_BYTES = 56 * 1024 * 1024

NEG_BIG = -1e30


def _cparams(*sem):
    return pltpu.CompilerParams(dimension_semantics=sem, vmem_limit_bytes=VMEM_LIMIT_BYTES)


def _resident(shape):
    zeros = (0,) * len(shape)
    return pl.BlockSpec(shape, lambda *_: zeros, pipeline_mode=pl.Buffered(1))


def _rmsnorm(x, g):
    return x * lax.rsqrt(jnp.mean(x * x, axis=-1, keepdims=True) + NORM_EPS) * g


def _sigmoid(x):
    return 1.0 / (1.0 + jnp.exp2(x * (-math.log2(math.e))))


def _log_sigmoid(x):
    return jnp.minimum(x, 0.0) - jnp.log(1.0 + jnp.exp(-jnp.abs(x)))


def _split3(x):
    hi = x.astype(BF16)
    r = x - hi.astype(F32)
    mid = r.astype(BF16)
    lo = (r - mid.astype(F32)).astype(BF16)
    return hi, mid, lo


def _cumsum_rows(tril, x):
    return sum(jnp.dot(tril, part, preferred_element_type=F32) for part in _split3(x))


def _inproj_kernel(x_ref, g_ref, wm_ref, ws_ref, p_ref, s_ref, *, col_chunk):
    h = _rmsnorm(x_ref[...], g_ref[...]).astype(BF16)
    s_ref[...] = jnp.dot(h, ws_ref[...], preferred_element_type=F32)
    for c in range(wm_ref.shape[1] // col_chunk):
        sl = slice(c * col_chunk, (c + 1) * col_chunk)
        p_ref[:, sl] = jnp.dot(h, wm_ref[:, sl], preferred_element_type=F32).astype(BF16)


def _inproj(x2, g, wm, ws, *, tm=512, col_chunk=1024):
    n, d = x2.shape
    wcols = wm.shape[1]
    return pl.pallas_call(
        functools.partial(_inproj_kernel, col_chunk=col_chunk),
        grid=(n // tm,),
        in_specs=[pl.BlockSpec((tm, d), lambda i: (i, 0)),
                  _resident((1, d)), _resident((d, wcols)), _resident((d, LANES))],
        out_specs=[pl.BlockSpec((tm, wcols), lambda i: (i, 0)),
                   pl.BlockSpec((tm, LANES), lambda i: (i, 0))],
        out_shape=[jax.ShapeDtypeStruct((n, wcols), BF16), jax.ShapeDtypeStruct((n, LANES), F32)],
        compiler_params=_cparams("parallel"),
        name="inproj",
    )(x2, g, wm, ws)


AUG_TERMS = 3
AUG_STRIDE = 8


def _aug_placement(dh):
    pairs = FOX_HEADS * dh // LANES
    width = pairs * LANES
    pq = np.zeros((AUG_TERMS * LANES, width), np.float32)
    pk = np.zeros((AUG_TERMS * LANES, width), np.float32)
    cq = np.zeros((1, width), np.float32)
    ck = np.zeros((1, width), np.float32)
    for h in range(FOX_HEADS):
        base = (h // 2) * LANES + (h % 2) * AUG_STRIDE
        for t in range(AUG_TERMS):
            src = t * LANES + 2 * ML_HEADS + h
            pq[src, base + t] = 1.0
            pk[src, base + AUG_TERMS + t] = -1.0
            cq[0, base + AUG_TERMS + t] = 1.0
            ck[0, base + t] = 1.0
    return jnp.asarray(pq, BF16), jnp.asarray(pk, BF16), jnp.asarray(cq), jnp.asarray(ck)


def _gateprep_kernel(s_ref, b_ref, pq_ref, pk_ref, cq_ref, ck_ref, col_ref, row_ref, aq_ref, ak_ref, carry_ref):
    t = s_ref.shape[1]

    @pl.when(pl.program_id(1) == 0)
    def _():
        carry_ref[...] = jnp.zeros_like(carry_ref)

    pre = s_ref[0] + b_ref[...]
    lane = lax.broadcasted_iota(jnp.int32, pre.shape, 1)
    ls = _log_sigmoid(pre)
    forget = (lane >= ML_HEADS) & (lane < 2 * ML_HEADS + FOX_HEADS)
    half = t // 2
    tril = (lax.broadcasted_iota(jnp.int32, (half, half), 0)
            >= lax.broadcasted_iota(jnp.int32, (half, half), 1)).astype(BF16)
    gate_log = jnp.where(forget, ls, 0.0)
    top = _cumsum_rows(tril, gate_log[0:half]) + carry_ref[...]
    bottom = _cumsum_rows(tril, gate_log[half:t]) + top[half - 1:half, :]
    fcum = jnp.concatenate([top, bottom], axis=0)
    carry_ref[...] = fcum[t - 1:t, :]
    col = jnp.where(lane < ML_HEADS, pre, fcum)
    col_ref[0] = col
    row_ref[0] = col.T[0:row_ref.shape[1], :]
    parts = jnp.concatenate(_split3(fcum), axis=-1)
    aq_ref[0] = (jnp.dot(parts, pq_ref[...], preferred_element_type=F32) + cq_ref[...]).astype(BF16)
    ak_ref[0] = (jnp.dot(parts, pk_ref[...], preferred_element_type=F32) + ck_ref[...]).astype(BF16)


def _gateprep(s3, bias, dh, *, tg=512):
    b, s, _ = s3.shape
    n_rows = 2 * ML_HEADS + FOX_HEADS
    pq, pk, cq, ck = _aug_placement(dh)
    width = pq.shape[1]
    return pl.pallas_call(
        _gateprep_kernel,
        grid=(b, s // tg),
        in_specs=[pl.BlockSpec((1, tg, LANES), lambda i, j: (i, j, 0)), _resident((1, LANES)),
                  _resident(pq.shape), _resident(pk.shape), _resident(cq.shape), _resident(ck.shape)],
        out_specs=[pl.BlockSpec((1, tg, LANES), lambda i, j: (i, j, 0)),
                   pl.BlockSpec((1, n_rows, tg), lambda i, j: (i, 0, j)),
                   pl.BlockSpec((1, tg, width), lambda i, j: (i, j, 0)),
                   pl.BlockSpec((1, tg, width), lambda i, j: (i, j, 0))],
        out_shape=[jax.ShapeDtypeStruct((b, s, LANES), F32), jax.ShapeDtypeStruct((b, n_rows, s), F32),
                   jax.ShapeDtypeStruct((b, s, width), BF16), jax.ShapeDtypeStruct((b, s, width), BF16)],
        scratch_shapes=[pltpu.VMEM((1, LANES), F32)],
        compiler_params=_cparams("parallel", "arbitrary"),
        name="gateprep",
    )(s3, bias, pq, pk, cq, ck)


def _fox_kernel(q_ref, aq_ref, k_ref, ak_ref, v_ref, o_ref, m_ref, acc_ref, *, tq, tk, diag_strips, dh, scale):
    qi = pl.program_id(2)
    lane = lax.broadcasted_iota(jnp.int32, (tq, LANES), 1)
    q2 = q_ref[0] * scale
    aq = aq_ref[0]
    in_head = [(lane >= a * dh) & (lane < (a + 1) * dh) for a in range(2)]
    q_ops = []
    for a in range(2):
        in_aug = (lane >= a * AUG_STRIDE) & (lane < a * AUG_STRIDE + 2 * AUG_TERMS)
        q_ops.append(jnp.concatenate([jnp.where(in_head[a], q2, jnp.zeros_like(q2)),
                                      jnp.where(in_aug, aq, jnp.zeros_like(aq))], axis=-1))
    m_ref[...] = jnp.full_like(m_ref, NEG_BIG)
    acc_ref[...] = jnp.zeros_like(acc_ref)
    def step(start, width, row0=0, masked=False):
        rows = slice(row0, tq)
        kb = jnp.concatenate([k_ref[0, pl.ds(start, width), :], ak_ref[0, pl.ds(start, width), :]], axis=-1)
        vb = v_ref[0, pl.ds(start, width), :]
        key_lane = lax.broadcasted_iota(jnp.int32, (width, LANES), 1)
        key_head = [(key_lane >= a * dh) & (key_lane < (a + 1) * dh) for a in range(2)]
        if masked:
            causal = (lax.broadcasted_iota(jnp.int32, (tq - row0, width), 1)
                      <= lax.broadcasted_iota(jnp.int32, (tq - row0, width), 0))
        for a in range(2):
            s = lax.dot_general(q_ops[a][rows], kb, (((1,), (1,)), ((), ())), preferred_element_type=F32)
            if masked:
                s = jnp.where(causal, s, NEG_BIG)
            m_prev = m_ref[a, rows]
            m_new = jnp.maximum(m_prev, jnp.max(s, axis=-1, keepdims=True))
            alpha = jnp.exp(m_prev - m_new)
            pexp = jnp.exp(s - jnp.concatenate([m_new] * (width // LANES), axis=-1))
            v_op = jnp.where(key_head[a], vb, jnp.ones_like(vb))
            acc_ref[a, rows] = alpha * acc_ref[a, rows] + jnp.dot(pexp.astype(BF16), v_op,
                                                                  preferred_element_type=F32)
            m_ref[a, rows] = m_new

    n_wide = (qi * tq) // tk
    n_narrow = qi - n_wide * (tk // tq)

    def wide_body(j, carry):
        step(pl.multiple_of(j * tk, tk), tk)
        return carry

    def narrow_body(j, carry):
        step(pl.multiple_of((n_wide * (tk // tq) + j) * tq, tq), tq)
        return carry

    lax.fori_loop(0, n_wide, wide_body, 0)
    if tk != tq:
        lax.fori_loop(0, n_narrow, narrow_body, 0)
    strip = tq // diag_strips
    for c in range(diag_strips):
        step(pl.multiple_of(qi * tq + c * strip, strip), strip, row0=c * strip, masked=True)
    outs = [acc_ref[a] / pltpu.roll(acc_ref[a], LANES // 2, 1) for a in range(2)]
    o_ref[0] = jnp.where(in_head[0], outs[0], outs[1]).astype(BF16)


def _fox(p3, aq, ak, *, q_col, k_col, v_col, dh, tq=1024, tk=2048, diag_strips=2):
    b, s, _ = p3.shape
    assert 2 * dh == LANES, "two heads share one 128-lane block"
    assert tk % tq == 0 and s % tq == 0
    pairs = FOX_HEADS * dh // LANES
    scale = dh ** -0.5
    assert math.frexp(scale)[0] == 0.5, "score scale is folded into bf16 q; exact only for a power of two"
    qb, kb, vb = q_col // LANES, k_col // LANES, v_col // LANES
    return pl.pallas_call(
        functools.partial(_fox_kernel, tq=tq, tk=tk, diag_strips=diag_strips, dh=dh, scale=scale),
        grid=(b, pairs, s // tq),
        in_specs=[pl.BlockSpec((1, tq, LANES), lambda i, p, q: (i, q, qb + p)),
                  pl.BlockSpec((1, tq, LANES), lambda i, p, q: (i, q, p)),
                  pl.BlockSpec((1, s, LANES), lambda i, p, q: (i, 0, kb + p)),
                  pl.BlockSpec((1, s, LANES), lambda i, p, q: (i, 0, p)),
                  pl.BlockSpec((1, s, LANES), lambda i, p, q: (i, 0, vb + p))],
        out_specs=pl.BlockSpec((1, tq, LANES), lambda i, p, q: (i, q, p)),
        out_shape=jax.ShapeDtypeStruct((b, s, pairs * LANES), BF16),
        scratch_shapes=[pltpu.VMEM((2, tq, LANES), F32), pltpu.VMEM((2, tq, LANES), F32)],
        compiler_params=_cparams("parallel", "parallel", "arbitrary"),
        name="fox_attention",
    )(p3, aq, p3, ak, p3)


def _mlstm_kernel(qk_ref, halo_ref, v_ref, og_ref, gc_ref, gr_ref, cw_ref, ng_ref, out_ref,
                  ubuf, ct_ref, n_ref, m_ref, g0_ref, *, chunk, dh):
    c = pl.program_id(1)
    width = ML_HEADS * dh
    taps = cw_ref.shape[0]
    pad = halo_ref.shape[1]
    assert 3 * ML_HEADS <= BF16_SUBLANES

    @pl.when(c == 0)
    def _():
        ct_ref[...] = jnp.zeros_like(ct_ref)
        n_ref[...] = jnp.zeros_like(n_ref)
        m_ref[...] = jnp.zeros_like(m_ref)
        g0_ref[...] = jnp.zeros_like(g0_ref)

    ubuf[0:pad, :] = jnp.where(c == 0, 0.0, halo_ref[0].astype(F32))
    ubuf[pad:pad + chunk, :] = qk_ref[0].astype(F32)

    def conv_silu(col0):
        y = sum(ubuf[pad - taps + 1 + j:pad - taps + 1 + j + chunk, col0:col0 + dh] * cw_ref[j:j + 1, col0:col0 + dh]
                for j in range(taps))
        return y * _sigmoid(y)

    ri = lax.broadcasted_iota(jnp.int32, (chunk, chunk), 0)
    ci = lax.broadcasted_iota(jnp.int32, (chunk, chunk), 1)
    visible = ri <= ci
    gc = gc_ref[0]
    gr = gr_ref[0]
    g0 = g0_ref[...]
    g0_ref[...] = gc[chunk - 1:chunk, :]
    contract_last = (((1,), (1,)), ((), ()))
    contract_first = (((0,), (0,)), ((), ()))
    part_row = lax.broadcasted_iota(jnp.int32, (BF16_SUBLANES, dh), 0)
    slab_row = lax.broadcasted_iota(jnp.int32, (BF16_SUBLANES, chunk), 0)
    ones_dh = jnp.ones((dh, dh), BF16)

    slab = jnp.zeros((BF16_SUBLANES, chunk), F32)
    heads = []
    for h in range(ML_HEADS):
        hs = slice(h * dh, (h + 1) * dh)
        q = conv_silu(h * dh)
        k = conv_silu(width + h * dh) * dh ** -0.5
        qb, kb = q.astype(BF16), k.astype(BF16)
        vb = v_ref[0, :, hs]
        g0_h = g0[:, ML_HEADS + h:ML_HEADS + h + 1]
        key_col = gc[:, h:h + 1] - (gc[:, ML_HEADS + h:ML_HEADS + h + 1] - g0_h)
        li_r = gr[h:h + 1, :]
        b_r = gr[ML_HEADS + h:ML_HEADS + h + 1, :] - g0_h
        b_last = b_r[:, chunk - 1:chunk]
        m_prev = m_ref[h:h + 1, 0:1]

        d_t = jnp.where(visible, b_r + key_col, NEG_BIG)
        m_inter = b_r + m_prev
        m_t = jnp.maximum(m_inter, jnp.max(d_t, axis=0, keepdims=True))
        w_inter = jnp.exp(m_inter - m_t)
        p_t = jnp.exp(d_t - m_t) * lax.dot_general(kb, qb, contract_last, preferred_element_type=F32)
        n_hi, n_mid, n_lo = (part.astype(F32) for part in _split3(n_ref[h]))
        n_parts = jnp.where(part_row == 0, n_hi, jnp.where(part_row == 1, n_mid,
                                                           jnp.where(part_row == 2, n_lo, 0.0))).astype(BF16)
        nq = jnp.sum(lax.dot_general(n_parts, qb, contract_last, preferred_element_type=F32), axis=0, keepdims=True)
        den = w_inter * nq + jnp.sum(p_t, axis=0, keepdims=True)
        inv = 1.0 / jnp.maximum(jnp.abs(den), jnp.exp(-m_t))

        g_r = b_last - b_r + li_r
        m_new = jnp.maximum(b_last + m_prev, jnp.max(g_r, axis=-1, keepdims=True))
        decay = jnp.exp(b_last + m_prev - m_new)
        wk_r = jnp.exp(g_r - m_new)
        for slot, vec in enumerate((w_inter, inv, wk_r)):
            slab = jnp.where(slab_row == 3 * h + slot, vec, slab)
        heads.append((hs, k, qb, vb, p_t.astype(BF16), decay, m_new))

    cols = jnp.concatenate([slab, jnp.zeros((LANES - BF16_SUBLANES, chunk), F32)], axis=0).T

    for h, (hs, k, qb, vb, p_tb, decay, m_new) in enumerate(heads):
        w_col, inv_col, wk_col = (cols[:, 3 * h + slot:3 * h + slot + 1] for slot in range(3))
        num = (w_col * jnp.dot(qb, ct_ref[h].astype(BF16), preferred_element_type=F32)
               + lax.dot_general(p_tb, vb, contract_first, preferred_element_type=F32))
        hh = num * inv_col
        kw = k * wk_col
        ct_ref[h] = decay * ct_ref[h] + lax.dot_general(kw.astype(BF16), vb, contract_first,
                                                       preferred_element_type=F32)
        n_ref[h] = decay * n_ref[h] + jnp.sum(kw, axis=0, keepdims=True)
        m_ref[h:h + 1, :] = jnp.broadcast_to(m_new, (1, LANES))

        sq = hh * hh
        sq_hi = sq.astype(BF16)
        sq_lo = (sq - sq_hi.astype(F32)).astype(BF16)
        mean_sq = (jnp.dot(sq_hi, ones_dh, preferred_element_type=F32)
                   + jnp.dot(sq_lo, ones_dh, preferred_element_type=F32)) * (1.0 / dh)
        hn = hh * lax.rsqrt(mean_sq + NORM_EPS) * ng_ref[:, hs]
        out_ref[0, :, hs] = (hn * _sigmoid(og_ref[0, :, hs].astype(F32))).astype(BF16)


def _mlstm(p3, gcol, grow, conv_w, norm_g, *, qk_col, v_col, o_col, dh, chunk=256):
    b, s, _ = p3.shape
    width = ML_HEADS * dh
    pad = BF16_SUBLANES
    assert conv_w.shape[0] - 1 <= pad
    qkb, vb, ob = qk_col // (2 * width), v_col // width, o_col // width
    halo_per_chunk = chunk // pad
    return pl.pallas_call(
        functools.partial(_mlstm_kernel, chunk=chunk, dh=dh),
        grid=(b, s // chunk),
        in_specs=[pl.BlockSpec((1, chunk, 2 * width), lambda i, c: (i, c, qkb)),
                  pl.BlockSpec((1, pad, 2 * width),
                               lambda i, c: (i, jnp.maximum(c * halo_per_chunk - 1, 0), qkb)),
                  pl.BlockSpec((1, chunk, width), lambda i, c: (i, c, vb)),
                  pl.BlockSpec((1, chunk, width), lambda i, c: (i, c, ob)),
                  pl.BlockSpec((1, chunk, LANES), lambda i, c: (i, c, 0)),
                  pl.BlockSpec((1, grow.shape[1], chunk), lambda i, c: (i, 0, c)),
                  _resident(conv_w.shape), _resident((1, width))],
        out_specs=pl.BlockSpec((1, chunk, width), lambda i, c: (i, c, 0)),
        out_shape=jax.ShapeDtypeStruct((b, s, width), BF16),
        scratch_shapes=[pltpu.VMEM((pad + chunk, 2 * width), F32),
                        pltpu.VMEM((ML_HEADS, dh, dh), F32),
                        pltpu.VMEM((ML_HEADS, 1, dh), F32),
                        pltpu.VMEM((ML_HEADS, LANES), F32),
                        pltpu.VMEM((1, LANES), F32)],
        compiler_params=_cparams("parallel", "arbitrary"),
        name="mlstm",
    )(p3, p3, p3, p3, gcol, grow, conv_w, norm_g)


def _merge_kernel(u_ref, halo_ref, hm_ref, fo_ref, gp_ref, x_ref, wg_ref, ps_ref, wbp_ref, wbm_ref, wbf_ref,
                  wo_ref, o_ref, ubuf, sbuf, *, tm):
    j = pl.program_id(1)
    pad = halo_ref.shape[1]
    gw = wg_ref.shape[1]
    pw = ubuf.shape[1]
    d = x_ref.shape[2]
    margin = ubuf.shape[0] - pad - tm
    base = margin + pad
    ext = pad + tm
    levels = len(POOL_WINDOWS)
    assert all(w == 2 ** (g + 1) for g, w in enumerate(POOL_WINDOWS)) and 2 ** (levels - 1) <= margin

    ubuf[0:margin, :] = jnp.zeros((margin, pw), F32)
    ubuf[margin:base, :] = jnp.where(j == 0, 0.0, halo_ref[0].astype(F32))
    ubuf[base:base + tm, :] = u_ref[0].astype(F32)
    wsums = []
    prev = ubuf
    for lvl in range(1, levels + 1):
        shift = 2 ** (lvl - 1)
        lanes = slice((lvl - 1) * gw, pw)
        if lvl < levels:
            cur = sbuf.at[lvl - 1]
            cur[0:margin, lanes] = jnp.zeros((margin, pw - (lvl - 1) * gw), F32)
            cur[margin:margin + ext, lanes] = (prev[margin:margin + ext, lanes]
                                               + prev[margin - shift:margin - shift + ext, lanes])
            wsums.append(cur[base:base + tm, (lvl - 1) * gw:lvl * gw])
            prev = cur
        else:
            wsums.append(prev[base:base + tm, lanes] + prev[base - shift:base - shift + tm, lanes])
    pos = j * tm + lax.broadcasted_iota(jnp.int32, (tm, 1), 0)
    ys = []
    for g, w in enumerate(POOL_WINDOWS):
        gs = slice(g * gw, (g + 1) * gw)
        cnt = jnp.minimum(pos + 1, w).astype(F32)
        dlt = wsums[g] / cnt - ubuf[base:base + tm, gs]
        ys.append(jnp.dot(dlt.astype(BF16), wg_ref[g], preferred_element_type=F32))
    y_pool = (jnp.concatenate(ys, axis=-1) * ps_ref[...]).astype(BF16)
    merged = _sigmoid(gp_ref[0, :, 0:d].astype(F32)) * jnp.dot(y_pool, wbp_ref[...], preferred_element_type=F32)
    merged += _sigmoid(gp_ref[0, :, d:2 * d].astype(F32)) * jnp.dot(hm_ref[0], wbm_ref[...],
                                                                   preferred_element_type=F32)
    merged += _sigmoid(gp_ref[0, :, 2 * d:3 * d].astype(F32)) * jnp.dot(fo_ref[0], wbf_ref[...],
                                                                       preferred_element_type=F32)
    o_ref[0] = x_ref[0] + jnp.dot(merged.astype(BF16), wo_ref[...], preferred_element_type=F32)


def _merge(p3, hm, fo, x3, wgrp, pscale, wbp, wbm, wbf, wo, *, pool_col, gate_col, tm=1024):
    b, s, d = x3.shape
    pw = wbp.shape[0]
    pad = BF16_SUBLANES
    margin = 8
    assert max(POOL_WINDOWS) - 1 <= pad and gate_col == 0
    pcb = pool_col // pw
    halo_per_blk = tm // pad
    return pl.pallas_call(
        functools.partial(_merge_kernel, tm=tm),
        grid=(b, s // tm),
        in_specs=[pl.BlockSpec((1, tm, pw), lambda i, j: (i, j, pcb)),
                  pl.BlockSpec((1, pad, pw), lambda i, j: (i, jnp.maximum(j * halo_per_blk - 1, 0), pcb)),
                  pl.BlockSpec((1, tm, hm.shape[2]), lambda i, j: (i, j, 0)),
                  pl.BlockSpec((1, tm, fo.shape[2]), lambda i, j: (i, j, 0)),
                  pl.BlockSpec((1, tm, N_BRANCH * d), lambda i, j: (i, j, 0)),
                  pl.BlockSpec((1, tm, d), lambda i, j: (i, j, 0)),
                  _resident(wgrp.shape), _resident(pscale.shape), _resident(wbp.shape),
                  _resident(wbm.shape), _resident(wbf.shape), _resident(wo.shape)],
        out_specs=pl.BlockSpec((1, tm, d), lambda i, j: (i, j, 0)),
        out_shape=jax.ShapeDtypeStruct((b, s, d), F32),
        scratch_shapes=[pltpu.VMEM((margin + pad + tm, pw), F32),
                        pltpu.VMEM((len(POOL_WINDOWS) - 1, margin + pad + tm, pw), F32)],
        compiler_params=_cparams("parallel", "parallel"),
        name="merge",
    )(p3, p3, hm, fo, p3, x3, wgrp, pscale, wbp, wbm, wbf, wo)


def _swiglu_kernel(x_ref, g_ref, wg_ref, wu_ref, wd_ref, o_ref, a_ref, *, ff_chunk):
    x = x_ref[...]
    h = _rmsnorm(x, g_ref[...]).astype(BF16)
    for c in range(wg_ref.shape[1] // ff_chunk):
        sl = slice(c * ff_chunk, (c + 1) * ff_chunk)
        gate = jnp.dot(h, wg_ref[:, sl], preferred_element_type=F32)
        up = jnp.dot(h, wu_ref[:, sl], preferred_element_type=F32)
        a_ref[:, sl] = (gate * _sigmoid(gate) * up).astype(BF16)
    o_ref[...] = x + jnp.dot(a_ref[...], wd_ref[...], preferred_element_type=F32)


def _swiglu(x2, g, wg, wu, wd, *, tm=1024, ff_chunk=256):
    n, d = x2.shape
    ff = wg.shape[1]
    assert ff % ff_chunk == 0
    return pl.pallas_call(
        functools.partial(_swiglu_kernel, ff_chunk=ff_chunk),
        grid=(n // tm,),
        in_specs=[pl.BlockSpec((tm, d), lambda i: (i, 0)), _resident((1, d)),
                  _resident(wg.shape), _resident(wu.shape), _resident(wd.shape)],
        out_specs=pl.BlockSpec((tm, d), lambda i: (i, 0)),
        out_shape=jax.ShapeDtypeStruct((n, d), F32),
        scratch_shapes=[pltpu.VMEM((tm, ff), BF16)],
        compiler_params=_cparams("parallel"),
        name="dense_swiglu",
    )(x2, g, wg, wu, wd)


def _router_kernel(x_ref, g_ref, wr_ref, br_ref, e_ref, gt_ref, cnt_ref, carry_ref):
    h = _rmsnorm(x_ref[...], g_ref[...])
    h_hi = h.astype(BF16)
    h_lo = (h - h_hi.astype(F32)).astype(BF16)
    logits = jnp.dot(jnp.concatenate([h_hi, h_hi, h_lo], axis=-1), wr_ref[...],
                     preferred_element_type=F32) + br_ref[...]
    lane = lax.broadcasted_iota(jnp.int32, logits.shape, 1)
    m1 = jnp.max(logits, axis=-1, keepdims=True)
    i1 = jnp.min(jnp.where(logits == m1, lane, LANES), axis=-1, keepdims=True)
    rest = jnp.where(lane == i1, NEG_BIG, logits)
    m2 = jnp.max(rest, axis=-1, keepdims=True)
    i2 = jnp.min(jnp.where(rest == m2, lane, LANES), axis=-1, keepdims=True)
    e2 = jnp.exp(m2 - m1)
    g1 = 1.0 / (1.0 + e2)
    gt_ref[...] = jnp.where(lane == 0, g1, jnp.where(lane == 1, e2 * g1, 0.0))

    @pl.when(pl.program_id(0) == 0)
    def _():
        carry_ref[...] = jnp.zeros_like(carry_ref)

    tm = logits.shape[0]
    pick0, pick1 = lane == i1, lane == i2
    picks = jnp.where(pick0 | pick1, 1.0, 0.0)
    earlier = (lax.broadcasted_iota(jnp.int32, (tm, tm), 0) > lax.broadcasted_iota(jnp.int32, (tm, tm), 1))
    prior = jnp.dot(earlier.astype(BF16), picks.astype(BF16), preferred_element_type=F32) + carry_ref[...]
    r0 = jnp.sum(jnp.where(pick0, prior, 0.0), axis=-1, keepdims=True).astype(jnp.int32)
    r1 = jnp.sum(jnp.where(pick1, prior, 0.0), axis=-1, keepdims=True).astype(jnp.int32)
    carry_ref[...] = prior[tm - 1:tm, :] + picks[tm - 1:tm, :]
    cnt_ref[...] = jnp.broadcast_to(carry_ref[...], cnt_ref.shape).astype(jnp.int32)
    routed = jnp.where(lane == 0, i1, jnp.where(lane == 1, i2, jnp.where(lane == 2, r0, jnp.where(lane == 3, r1, 0))))
    e_ref[...] = routed[:, 0:e_ref.shape[1]]


def _router(x2, g, wr, br, *, tm=512):
    n, d = x2.shape
    return pl.pallas_call(
        _router_kernel,
        grid=(n // tm,),
        in_specs=[pl.BlockSpec((tm, d), lambda i: (i, 0)), _resident((1, d)),
                  _resident(wr.shape), _resident((1, LANES))],
        out_specs=[pl.BlockSpec((tm, 2 * TOP_K), lambda i: (i, 0)), pl.BlockSpec((tm, LANES), lambda i: (i, 0)),
                   pl.BlockSpec((8, LANES), lambda i: (0, 0))],
        out_shape=[jax.ShapeDtypeStruct((n, 2 * TOP_K), jnp.int32), jax.ShapeDtypeStruct((n, LANES), F32),
                   jax.ShapeDtypeStruct((8, LANES), jnp.int32)],
        scratch_shapes=[pltpu.VMEM((1, LANES), F32)],
        compiler_params=_cparams("arbitrary"),
        name="router",
    )(x2, g, wr, br)


def _dispatch_kernel(pend_ref, dest_ref, x_ref, xs_ref, zero_ref, xbuf, sems, zsem):
    tm = dest_ref.shape[2] // TOP_K
    rows = zero_ref.shape[0]
    step = pl.program_id(0)

    @pl.when(step == 0)
    def _():
        zero_ref[...] = jnp.zeros_like(zero_ref)

        def zero_copy(e):
            start = pl.multiple_of(pend_ref[e] - rows, rows)
            return pltpu.make_async_copy(zero_ref, xs_ref.at[pl.ds(start, rows)], zsem)

        def has_rows(e):
            return pend_ref[e] > (pend_ref[e - 1] if e else 0)

        n_exp = pend_ref.shape[0]
        n_rows = xs_ref.shape[0]
        tail_starts = [n_rows - (j + 1) * rows for j in range(min(n_exp, n_rows // rows))]

        def tail_copy(start):
            return pltpu.make_async_copy(zero_ref, xs_ref.at[pl.ds(start, rows)], zsem)

        for e in range(n_exp):
            @pl.when(has_rows(e))
            def _():
                zero_copy(e).start()
        for start in tail_starts:
            @pl.when(start >= pend_ref[n_exp - 1])
            def _():
                tail_copy(start).start()
        for e in range(n_exp):
            @pl.when(has_rows(e))
            def _():
                zero_copy(e).wait()
        for start in tail_starts:
            @pl.when(start >= pend_ref[n_exp - 1])
            def _():
                tail_copy(start).wait()

    def issue(s):
        xbuf[s] = x_ref[...]

        def start(t, carry):
            for k in range(TOP_K):
                pltpu.make_async_copy(xbuf.at[s, pl.ds(t, 1)],
                                      xs_ref.at[pl.ds(dest_ref[0, 0, TOP_K * t + k], 1)], sems.at[s]).start()
            return carry

        lax.fori_loop(0, tm, start, 0, unroll=True)

    def drain(s):
        def wait(t, carry):
            pltpu.make_async_copy(xbuf.at[s, pl.ds(0, 1)], xs_ref.at[pl.ds(0, 1)], sems.at[s]).wait()
            return carry

        lax.fori_loop(0, TOP_K * tm, wait, 0, unroll=8)

    last = pl.num_programs(0) - 1
    for s in range(2):
        @pl.when(step % 2 == s)
        def _():
            issue(s)

            @pl.when(step > 0)
            def _():
                drain(1 - s)

            @pl.when(step == last)
            def _():
                drain(s)


def _dispatch(pend, dest2, x2, *, n_rows, rows, tm=256):
    n, d = x2.shape
    grid_spec = pltpu.PrefetchScalarGridSpec(
        num_scalar_prefetch=1,
        grid=(n // tm,),
        in_specs=[pl.BlockSpec((1, 1, TOP_K * tm), lambda i, pe: (i, 0, 0), memory_space=pltpu.SMEM),
                  pl.BlockSpec((tm, d), lambda i, pe: (i, 0))],
        out_specs=pl.BlockSpec(memory_space=pl.ANY),
        scratch_shapes=[pltpu.VMEM((rows, d), F32), pltpu.VMEM((2, tm, d), F32),
                        pltpu.SemaphoreType.DMA((2,)), pltpu.SemaphoreType.DMA(())],
    )
    return pl.pallas_call(
        _dispatch_kernel,
        grid_spec=grid_spec,
        out_shape=jax.ShapeDtypeStruct((n_rows, d), F32),
        compiler_params=_cparams("arbitrary"),
        name="moe_dispatch",
    )(pend, dest2, x2)


def _experts_kernel(blk_e_ref, nact_ref, xs_ref, g_ref, wg_ref, wu_ref, wd_ref, ys_ref, a_ref, *, ff_chunk):
    del blk_e_ref
    i = pl.program_id(0)

    @pl.when(i < nact_ref[0])
    def _():
        h = _rmsnorm(xs_ref[...], g_ref[...]).astype(BF16)
        for c in range(wg_ref.shape[2] // ff_chunk):
            sl = slice(c * ff_chunk, (c + 1) * ff_chunk)
            gate = jnp.dot(h, wg_ref[0, :, sl], preferred_element_type=F32)
            up = jnp.dot(h, wu_ref[0, :, sl], preferred_element_type=F32)
            a_ref[:, sl] = (gate * _sigmoid(gate) * up).astype(BF16)
        ys_ref[...] = jnp.dot(a_ref[...], wd_ref[0], preferred_element_type=F32)

    @pl.when(i >= nact_ref[0])
    def _():
        ys_ref[...] = jnp.zeros_like(ys_ref)


def _experts(blk_e, nact, xs, g, wg, wu, wd, *, rows, ff_chunk=256):
    n_rows, d = xs.shape
    ff = wg.shape[2]
    assert ff % ff_chunk == 0 and n_rows % rows == 0

    def blk(i, nact):
        return jnp.minimum(i, nact[0] - 1)

    def expert_weights(shape, buffers):
        return pl.BlockSpec(shape, lambda i, be, na: (be[blk(i, na)], 0, 0), pipeline_mode=pl.Buffered(buffers))

    grid_spec = pltpu.PrefetchScalarGridSpec(
        num_scalar_prefetch=2,
        grid=(n_rows // rows,),
        in_specs=[pl.BlockSpec((rows, d), lambda i, be, na: (blk(i, na), 0)),
                  pl.BlockSpec((1, d), lambda i, be, na: (0, 0)),
                  expert_weights((1, d, ff), 2), expert_weights((1, d, ff), 2), expert_weights((1, ff, d), 1)],
        out_specs=pl.BlockSpec((rows, d), lambda i, be, na: (i, 0)),
        scratch_shapes=[pltpu.VMEM((rows, ff), BF16)],
    )
    return pl.pallas_call(
        functools.partial(_experts_kernel, ff_chunk=ff_chunk),
        grid_spec=grid_spec,
        out_shape=jax.ShapeDtypeStruct((n_rows, d), F32),
        compiler_params=_cparams("arbitrary"),
        name="moe_experts",
    )(blk_e, nact, xs, g, wg, wu, wd)


def _combine_kernel(dest_ref, next_dest_ref, x_ref, gt_ref, fg_ref, ys_ref, o_ref, buf, sems, *, final_norm):
    tm = x_ref.shape[0]
    step = pl.program_id(0)
    slot = step % 2

    def row_copy(dref, s, t, k):
        return pltpu.make_async_copy(ys_ref.at[pl.ds(dref[0, 0, TOP_K * t + k], 1)], buf.at[s, k, pl.ds(t, 1)],
                                     sems.at[s])

    def gather(dref, s):
        def start(t, carry):
            for k in range(TOP_K):
                row_copy(dref, s, t, k).start()
            return carry
        lax.fori_loop(0, tm, start, 0, unroll=True)

    @pl.when(step == 0)
    def _():
        gather(dest_ref, 0)

    @pl.when(step + 1 < pl.num_programs(0))
    def _():
        gather(next_dest_ref, 1 - slot)

    def wait(t, carry):
        for k in range(TOP_K):
            row_copy(dest_ref, slot, t, k).wait()
        return carry

    lax.fori_loop(0, tm, wait, 0, unroll=8)
    gt = gt_ref[...]
    y = x_ref[...]
    for k in range(TOP_K):
        y = y + buf[slot, k] * gt[:, k:k + 1]
    o_ref[...] = _rmsnorm(y, fg_ref[...]) if final_norm else y


def _combine(dest2, x2, gates, fg, ys, *, final_norm, tm=256):
    n, d = x2.shape
    steps = n // tm
    return pl.pallas_call(
        functools.partial(_combine_kernel, final_norm=final_norm),
        grid=(steps,),
        in_specs=[pl.BlockSpec((1, 1, TOP_K * tm), lambda i: (i, 0, 0), memory_space=pltpu.SMEM),
                  pl.BlockSpec((1, 1, TOP_K * tm), lambda i: (jnp.minimum(i + 1, steps - 1), 0, 0),
                               memory_space=pltpu.SMEM),
                  pl.BlockSpec((tm, d), lambda i: (i, 0)),
                  pl.BlockSpec((tm, LANES), lambda i: (i, 0)),
                  _resident((1, d)),
                  pl.BlockSpec(memory_space=pl.ANY)],
        out_specs=pl.BlockSpec((tm, d), lambda i: (i, 0)),
        out_shape=jax.ShapeDtypeStruct((n, d), F32),
        scratch_shapes=[pltpu.VMEM((2, TOP_K, tm, d), F32), pltpu.SemaphoreType.DMA((2,))],
        compiler_params=_cparams("arbitrary"),
        name="moe_combine",
    )(dest2, dest2, x2, gates, fg, ys)


def _final_norm_kernel(x_ref, g_ref, o_ref):
    o_ref[...] = _rmsnorm(x_ref[...], g_ref[...])


def _final_norm(x2, g, *, tm=512):
    n, d = x2.shape
    return pl.pallas_call(
        _final_norm_kernel,
        grid=(n // tm,),
        in_specs=[pl.BlockSpec((tm, d), lambda i: (i, 0)), _resident((1, d))],
        out_specs=pl.BlockSpec((tm, d), lambda i: (i, 0)),
        out_shape=jax.ShapeDtypeStruct((n, d), F32),
        compiler_params=_cparams("parallel"),
        name="final_norm",
    )(x2, g)


def _token_mixing(x3, norm_g, w_in, pool_w_grp, pool_scale, conv_w, b_i, b_f, ml_norm_g, fox_b_f,
                  w_br_pool, w_br_ml, w_br_fox, w_out):
    b, s, d = x3.shape
    pool_w = w_br_pool.shape[0]
    ml_w = w_br_ml.shape[0]
    fox_w = w_br_fox.shape[0]
    ml_dh = ml_w // ML_HEADS
    fox_dh = fox_w // FOX_HEADS
    n_small = 2 * ML_HEADS + FOX_HEADS

    sizes = (pool_w, ml_w, ml_w, ml_w, ml_w, ML_HEADS, ML_HEADS, fox_w, fox_w, fox_w, FOX_HEADS, N_BRANCH * d)
    offs = [0]
    for sz in sizes:
        offs.append(offs[-1] + sz)
    assert offs[-1] == w_in.shape[1]
    (o_pool, o_q, o_k, o_v, o_o, o_i, o_f, o_fq, o_fk, o_fv, o_ff, o_g) = offs[:-1]

    def cols(o, sz):
        return w_in[:, o:o + sz]

    wm = jnp.concatenate([cols(o_g, N_BRANCH * d), cols(o_q, ml_w), cols(o_k, ml_w), cols(o_v, ml_w),
                          cols(o_o, ml_w), cols(o_pool, pool_w), cols(o_fq, fox_w), cols(o_fk, fox_w),
                          cols(o_fv, fox_w)], axis=1).astype(BF16)
    c_gate = 0
    c_qk = c_gate + N_BRANCH * d
    c_v = c_qk + 2 * ml_w
    c_o = c_v + ml_w
    c_pool = c_o + ml_w
    c_fq = c_pool + pool_w
    c_fk = c_fq + fox_w
    c_fv = c_fk + fox_w
    ws = jnp.concatenate([cols(o_i, ML_HEADS), cols(o_f, ML_HEADS), cols(o_ff, FOX_HEADS),
                          jnp.zeros((d, LANES - n_small), w_in.dtype)], axis=1).astype(BF16)
    bias = jnp.concatenate([b_i, b_f, fox_b_f, jnp.zeros((LANES - n_small,), F32)]).reshape(1, LANES)

    proj, small = _inproj(x3.reshape(b * s, d), norm_g.reshape(1, d), wm, ws)
    p3 = proj.reshape(b, s, -1)
    gcol, grow, aq, ak = _gateprep(small.reshape(b, s, LANES), bias, fox_dh)
    fo = _fox(p3, aq, ak, q_col=c_fq, k_col=c_fk, v_col=c_fv, dh=fox_dh)
    hm = _mlstm(p3, gcol, grow, conv_w, ml_norm_g.reshape(1, ml_w), qk_col=c_qk, v_col=c_v, o_col=c_o, dh=ml_dh)
    return _merge(p3, hm, fo, x3, pool_w_grp.astype(BF16), pool_scale.reshape(1, pool_w),
                  w_br_pool.astype(BF16), w_br_ml.astype(BF16), w_br_fox.astype(BF16), w_out.astype(BF16),
                  pool_col=c_pool, gate_col=c_gate)


def _moe(x2, norm_g, w_router, b_router, w_gate, w_up, w_down, final_g, *, rows=512):
    n, d = x2.shape
    n_exp = w_router.shape[1]
    wr = jnp.concatenate([w_router.astype(F32), jnp.zeros((d, LANES - n_exp), F32)], axis=1)
    wr_hi = wr.astype(BF16)
    wr_lo = (wr - wr_hi.astype(F32)).astype(BF16)
    br = jnp.concatenate([b_router.astype(F32), jnp.full((LANES - n_exp,), NEG_BIG, F32)]).reshape(1, LANES)
    routed, gates, cnt = _router(x2, norm_g.reshape(1, d), jnp.concatenate([wr_hi, wr_lo, wr_hi], axis=0), br)

    e_flat = routed[:, :TOP_K].reshape(n * TOP_K)
    rank = routed[:, TOP_K:2 * TOP_K].reshape(n * TOP_K)
    counts = cnt[0, :n_exp]
    padded = ((counts + rows - 1) // rows) * rows
    pend = jnp.cumsum(padded)
    onehot = e_flat[:, None] == jnp.arange(n_exp, dtype=jnp.int32)[None, :]
    dest = (jnp.sum(jnp.where(onehot, (pend - padded)[None, :], 0), axis=1) + rank).astype(jnp.int32)
    n_blk = (n * TOP_K + n_exp * (rows - 1) + rows - 1) // rows
    blk_start = jnp.arange(n_blk, dtype=jnp.int32) * rows
    blk_e = jnp.minimum(jnp.sum(blk_start[:, None] >= pend[None, :], axis=1), n_exp - 1).astype(jnp.int32)
    nact = (pend[-1:] // rows).astype(jnp.int32)

    tm = 512
    dest2 = dest.reshape(n // tm, 1, TOP_K * tm)
    xs = _dispatch(pend.astype(jnp.int32), dest2, x2, n_rows=n_blk * rows, rows=rows, tm=tm)
    ys = _experts(blk_e, nact, xs, norm_g.reshape(1, d), w_gate.astype(BF16), w_up.astype(BF16),
                  w_down.astype(BF16), rows=rows)
    fg = jnp.ones((1, d), F32) if final_g is None else final_g.reshape(1, d)
    return _combine(dest2, x2, gates, fg, ys, final_norm=final_g is not None, tm=tm)


def kernel(x, mix_norm_g, w_in, pool_w_grp, pool_scale, ml_conv_w, ml_b_i, ml_b_f, ml_norm_g, fox_b_f,
           w_br_pool, w_br_ml, w_br_fox, w_out, ffn_norm_g, ff_w_gate, ff_w_up, ff_w_down,
           moe_w_router, moe_b_router, moe_w_gate, moe_w_up, moe_w_down, final_norm_g):
    b, s, d = x.shape
    depth = mix_norm_g.shape[0]
    fused_final = False
    for l in range(depth):
        x = _token_mixing(x, mix_norm_g[l], w_in[l], pool_w_grp[l], pool_scale[l], ml_conv_w[l], ml_b_i[l],
                          ml_b_f[l], ml_norm_g[l], fox_b_f[l], w_br_pool[l], w_br_ml[l], w_br_fox[l], w_out[l])
        x2 = x.reshape(b * s, d)
        if l % 2 == 0:
            x2 = _swiglu(x2, ffn_norm_g[l].reshape(1, d), ff_w_gate[l // 2].astype(BF16),
                         ff_w_up[l // 2].astype(BF16), ff_w_down[l // 2].astype(BF16))
        else:
            fused_final = l == depth - 1
            x2 = _moe(x2, ffn_norm_g[l], moe_w_router[l // 2], moe_b_router[l // 2], moe_w_gate[l // 2],
                      moe_w_up[l // 2], moe_w_down[l // 2], final_norm_g if fused_final else None)
        x = x2.reshape(b, s, d)
    if not fused_final:
        x = _final_norm(x.reshape(b * s, d), final_norm_g.reshape(1, d)).reshape(b, s, d)
    return x
```
